```python
import math
import jax, jax.numpy as jnp
from jax import lax
import numpy as np

D_MODEL = 1024
BATCH = 32
SEQ = 256
DEPTH = 4
DEC_BATCH = 4
DEC_SEQ = 1024
PAST_LEN = 512

GRID_W = 64
N_HEADS = 8
N_KV_HEADS = 2
HEAD_DIM = 64
KV_GROUP = N_HEADS // N_KV_HEADS
WINDOW = 128
ATT_BLOCK = 128
ROPE_BASE = 10000.0
SSM_WIDTH = 512
SSM_GROUP = 16
SSM_GROUPS = SSM_WIDTH // SSM_GROUP
SSM_STATE = 64
HG_HEADS = 4
HG_KDIM = 128
HG_VDIM = 128
HG_CHUNK = 32
N_BRANCHES = 3
N_GROUPS = 4
EXPERTS_PER_GROUP = 4
N_EXPERTS = N_GROUPS * EXPERTS_PER_GROUP
TOP_K = 2
EXPERT_FF = 256
EPS = 1e-6
Q_W = N_HEADS * HEAD_DIM
KV_W = N_KV_HEADS * HEAD_DIM
HG_KW = HG_HEADS * HG_KDIM
HG_VW = HG_HEADS * HG_VDIM
SPLIT_SIZES = (Q_W, KV_W, KV_W, SSM_WIDTH, HG_KW, HG_KW, HG_KW, HG_VW, HG_VW, N_BRANCHES * D_MODEL)
SPLIT_POINTS = tuple(int(v) for v in np.cumsum(SPLIT_SIZES)[:-1])
IN_WIDTH = int(sum(SPLIT_SIZES))

kernel_name = 'hybrid_diffusion_prefix_trunk_step'


def rmsnorm(x, g):
    xf = x.astype(jnp.float32)
    y = xf * lax.rsqrt(jnp.mean(xf * xf, axis=-1, keepdims=True) + EPS)
    return (y * g.astype(jnp.float32)).astype(x.dtype)


def axial_rope(x):
    L = x.shape[1]
    n_rows = L // GRID_W
    rows = jnp.repeat(jnp.arange(n_rows), GRID_W)
    cols = jnp.tile(jnp.arange(GRID_W), n_rows)
    half = HEAD_DIM // 2
    freqs = ROPE_BASE ** (-jnp.arange(0, half, 2, dtype=jnp.float32) / half)

    def rot(xh, pos):
        ang = pos.astype(jnp.float32)[:, None] * freqs[None, :]
        cos = jnp.cos(ang)[None, :, None, :]
        sin = jnp.sin(ang)[None, :, None, :]
        x1, x2 = jnp.split(xh.astype(jnp.float32), 2, axis=-1)
        return jnp.concatenate([x1 * cos - x2 * sin, x1 * sin + x2 * cos], axis=-1)

    out = jnp.concatenate([rot(x[..., :half], rows), rot(x[..., half:], cols)], axis=-1)
    return out.astype(x.dtype)


def sink_attention(q, k, v, mask, sink):
    B, Tq = q.shape[:2]
    qg = q.reshape(B, Tq, N_KV_HEADS, KV_GROUP, HEAD_DIM)
    s = jnp.einsum('bqkgd,blkd->bkgql', qg, k).astype(jnp.float32) / math.sqrt(HEAD_DIM)
    if mask is not None:
        s = jnp.where(mask, s, -jnp.inf)
    sk = sink.astype(jnp.float32).reshape(1, N_KV_HEADS, KV_GROUP, 1, 1)
    m = jnp.maximum(jnp.max(s, axis=-1, keepdims=True), sk)
    p = jnp.exp(s - m)
    p = p / (jnp.sum(p, axis=-1, keepdims=True) + jnp.exp(sk - m))
    o = jnp.einsum('bkgql,blkd->bqkgd', p.astype(v.dtype), v)
    return o.reshape(B, Tq, Q_W)


def context_attention(q, k, v, sink):
    B, L = q.shape[:2]
    nb = L // ATT_BLOCK
    qb = q.reshape(B, nb, ATT_BLOCK, N_HEADS, HEAD_DIM).swapaxes(0, 1)
    out = lax.map(lambda qq: sink_attention(qq, k, v, None, sink), qb)
    return out.swapaxes(0, 1).reshape(B, L, Q_W)


def latent_attention(q, k, v, k_ctx, v_ctx, sink):
    B, L = q.shape[:2]
    nb = L // ATT_BLOCK
    n_ctx = k_ctx.shape[1]
    pad = ((0, 0), (WINDOW, WINDOW), (0, 0), (0, 0))
    kp = jnp.pad(k, pad)
    vp = jnp.pad(v, pad)
    span = ATT_BLOCK + 2 * WINDOW
    qb = q.reshape(B, nb, ATT_BLOCK, N_HEADS, HEAD_DIM).swapaxes(0, 1)
    ctx_mask = jnp.ones((ATT_BLOCK, n_ctx), dtype=bool)

    def block(args):
        qq, j = args
        start = j * ATT_BLOCK
        kb = lax.dynamic_slice_in_dim(kp, start, span, axis=1)
        vb = lax.dynamic_slice_in_dim(vp, start, span, axis=1)
        qpos = start + jnp.arange(ATT_BLOCK)
        kpos = start - WINDOW + jnp.arange(span)
        band = (jnp.abs(qpos[:, None] - kpos[None, :]) <= WINDOW) & ((kpos >= 0) & (kpos < L))[None, :]
        mask = jnp.concatenate([band, ctx_mask], axis=1)
        kk = jnp.concatenate([kb, k_ctx], axis=1)
        vv = jnp.concatenate([vb, v_ctx], axis=1)
        return sink_attention(qq, kk, vv, mask, sink)

    out = lax.map(block, (qb, jnp.arange(nb)))
    return out.swapaxes(0, 1).reshape(B, L, Q_W)


def zoh_discretize(a_re, a_im, log_dt, b_re, b_im):
    a_re = a_re.astype(jnp.float32)
    a_im = a_im.astype(jnp.float32)
    b_re = b_re.astype(jnp.float32)
    b_im = b_im.astype(jnp.float32)
    dt = jnp.exp(log_dt.astype(jnp.float32))[:, None]
    mag = jnp.exp(a_re * dt)
    lam_re = mag * jnp.cos(a_im * dt)
    lam_im = mag * jnp.sin(a_im * dt)
    nr, ni = lam_re - 1.0, lam_im
    den = a_re * a_re + a_im * a_im
    f_re = (nr * a_re + ni * a_im) / den
    f_im = (ni * a_re - nr * a_im) / den
    bb_re = f_re[..., None] * b_re - f_im[..., None] * b_im
    bb_im = f_re[..., None] * b_im + f_im[..., None] * b_re
    return lam_re, lam_im, bb_re, bb_im


def ssm_scan(u, h0_re, h0_im, lam_re, lam_im, bb_re, bb_im):
    b_re = jnp.einsum('blgc,gpc->blgp', u, bb_re)
    b_im = jnp.einsum('blgc,gpc->blgp', u, bb_im)
    b_re = b_re.at[:, 0].add(lam_re * h0_re - lam_im * h0_im)
    b_im = b_im.at[:, 0].add(lam_re * h0_im + lam_im * h0_re)
    a_re = jnp.broadcast_to(lam_re, b_re.shape)
    a_im = jnp.broadcast_to(lam_im, b_im.shape)

    def combine(x, y):
        ar1, ai1, br1, bi1 = x
        ar2, ai2, br2, bi2 = y
        return (ar1 * ar2 - ai1 * ai2, ar1 * ai2 + ai1 * ar2,
                ar2 * br1 - ai2 * bi1 + br2, ar2 * bi1 + ai2 * br1 + bi2)

    _, _, s_re, s_im = lax.associative_scan(combine, (a_re, a_im, b_re, b_im), axis=1)
    return s_re, s_im


def ssm_mixer(u, h0_re, h0_im, a_re, a_im, log_dt, b_re, b_im, c_re, c_im, d_skip, w_glu_a, w_glu_b):
    B, L, _ = u.shape
    uf = u.astype(jnp.float32).reshape(B, L, SSM_GROUPS, SSM_GROUP)
    h0_re = h0_re.astype(jnp.float32)
    h0_im = h0_im.astype(jnp.float32)
    cr = c_re.astype(jnp.float32)
    ci = c_im.astype(jnp.float32)
    y = d_skip.astype(jnp.float32) * uf
    fin_re, fin_im = [], []
    for dr in range(2):
        lam_re, lam_im, bb_re, bb_im = zoh_discretize(a_re[dr], a_im[dr], log_dt[dr], b_re, b_im)
        ud = uf if dr == 0 else jnp.flip(uf, axis=1)
        s_re, s_im = ssm_scan(ud, h0_re[:, dr], h0_im[:, dr], lam_re, lam_im, bb_re, bb_im)
        yd = jnp.einsum('blgp,gcp->blgc', s_re, cr) - jnp.einsum('blgp,gcp->blgc', s_im, ci)
        y = y + (yd if dr == 0 else jnp.flip(yd, axis=1))
        fin_re.append(s_re[:, -1])
        fin_im.append(s_im[:, -1])
    z = jax.nn.gelu(y.reshape(B, L, SSM_WIDTH)).astype(u.dtype)
    out = (z @ w_glu_a) * jax.nn.sigmoid(z @ w_glu_b)
    return out, jnp.stack(fin_re, axis=1), jnp.stack(fin_im, axis=1)


def hgrn_chunk_scan(q, k, v, logf, s0):
    B, L = q.shape[:2]
    nc = L // HG_CHUNK

    def to_chunks(t):
        return t.reshape(B, nc, HG_CHUNK, HG_HEADS, t.shape[-1]).transpose(1, 0, 3, 2, 4)

    tri = jnp.tril(jnp.ones((HG_CHUNK, HG_CHUNK), dtype=bool))

    def step(S, xs):
        qc, kc, vc, lc = xs
        b = jnp.cumsum(lc, axis=2)
        o_inter = jnp.einsum('bhtd,bhdv->bhtv', qc * jnp.exp(b), S)
        diff = b[:, :, :, None, :] - b[:, :, None, :, :]
        decay = jnp.exp(jnp.where(tri[:, :, None], diff, -jnp.inf))
        att = jnp.einsum('bhtd,bhsd,bhtsd->bhts', qc, kc, decay)
        o = o_inter + jnp.einsum('bhts,bhsv->bhtv', att, vc)
        b_last = b[:, :, -1:, :]
        S = jnp.exp(b_last[:, :, 0, :])[..., None] * S + jnp.einsum('bhsd,bhsv->bhdv', kc * jnp.exp(b_last - b), vc)
        return S, o

    S, o = lax.scan(step, s0, (to_chunks(q), to_chunks(k), to_chunks(v), to_chunks(logf)))
    o = o.transpose(1, 0, 3, 2, 4).reshape(B, L, HG_HEADS, HG_VDIM)
    return o, S


def hgrn_mixer(hq, hf_fwd, hf_bwd, hi, hgate, lb, s0, norm_g, w_o):
    B, L, _ = hq.shape
    kshape = (B, L, HG_HEADS, HG_KDIM)
    vshape = (B, L, HG_HEADS, HG_VDIM)
    q = jax.nn.silu(hq.astype(jnp.float32)).reshape(kshape)
    v = hi.astype(jnp.float32).reshape(vshape)
    s0 = s0.astype(jnp.float32)
    o_sum = jnp.zeros(vshape, jnp.float32)
    finals = []
    for d, hf in enumerate((hf_fwd, hf_bwd)):
        fgate = (lb[d] + (1.0 - lb[d]) * jax.nn.sigmoid(hf.astype(jnp.float32))).reshape(kshape)
        seq = (q, 1.0 - fgate, v, jnp.log(fgate))
        if d == 1:
            seq = tuple(jnp.flip(t, axis=1) for t in seq)
        o, s_fin = hgrn_chunk_scan(seq[0], seq[1], seq[2], seq[3], s0[:, d])
        o_sum = o_sum + (o if d == 0 else jnp.flip(o, axis=1))
        finals.append(s_fin)
    o = rmsnorm(o_sum, norm_g) * jax.nn.silu(hgate.astype(jnp.float32).reshape(vshape))
    y = o.reshape(B, L, HG_VW).astype(hq.dtype) @ w_o
    return y, jnp.stack(finals, axis=1)


def hier_moe(h, w_group, b_group, w_router, b_router, w_e_gate, w_e_up, w_e_down):
    B, L, D = h.shape
    x = h.reshape(B * L, D)
    gl = (x @ w_group + b_group).astype(jnp.float32)
    gp = jax.nn.softmax(gl, axis=-1)
    g_idx = jnp.argmax(gl, axis=-1)
    g_prob = jnp.take_along_axis(gp, g_idx[:, None], axis=-1)
    el = (x @ w_router + b_router).astype(jnp.float32).reshape(-1, N_GROUPS, EXPERTS_PER_GROUP)
    el = jnp.take_along_axis(el, g_idx[:, None, None], axis=1)[:, 0]
    top_p, top_i = lax.top_k(jax.nn.softmax(el, axis=-1), TOP_K)
    top_p = top_p / jnp.sum(top_p, axis=-1, keepdims=True)
    expert_id = g_idx[:, None] * EXPERTS_PER_GROUP + top_i
    combine = jnp.sum(jax.nn.one_hot(expert_id, N_EXPERTS, dtype=jnp.float32) * (g_prob * top_p)[..., None], axis=1)
    hg = jnp.einsum('nd,edf->enf', x, w_e_gate)
    hu = jnp.einsum('nd,edf->enf', x, w_e_up)
    act = jax.nn.silu(hg) * hu * combine.T[..., None].astype(x.dtype)
    y = jnp.einsum('enf,efd->nd', act, w_e_down)
    return y.reshape(B, L, D)


def trunk_layer(x, cond, p, lb, cache):
    B, L, _ = x.shape
    mod = (jax.nn.silu(cond) @ p['w_mod'] + p['b_mod'])[:, None, :]
    sh1, sc1, g1, sh2, sc2, g2 = jnp.split(mod, 6, axis=-1)
    h = rmsnorm(x, p['norm1_g']) * (1.0 + sc1) + sh1
    aq, ak, av, su, hq, hff, hfb, hi, hgt, mg = jnp.split(h @ p['w_in'], SPLIT_POINTS, axis=-1)
    q = rmsnorm(aq.reshape(B, L, N_HEADS, HEAD_DIM), p['q_norm_g'])
    k = rmsnorm(ak.reshape(B, L, N_KV_HEADS, HEAD_DIM), p['k_norm_g'])
    v = av.reshape(B, L, N_KV_HEADS, HEAD_DIM)
    if cache is None:
        att = context_attention(q, k, v, p['attn_sink'])
        ssm_h0_re = jnp.zeros((B, 2, SSM_GROUPS, SSM_STATE), jnp.float32)
        ssm_h0_im = ssm_h0_re
        hg_s0 = jnp.zeros((B, 2, HG_HEADS, HG_KDIM, HG_VDIM), jnp.float32)
    else:
        k_ctx, v_ctx, ssm_h0_re, ssm_h0_im, hg_s0 = cache
        att = latent_attention(axial_rope(q), axial_rope(k), v, k_ctx.astype(k.dtype), v_ctx.astype(v.dtype), p['attn_sink'])
    y_att = att @ p['w_attn_o']
    y_ssm, ssm_re, ssm_im = ssm_mixer(su, ssm_h0_re, ssm_h0_im, p['ssm_a_re'], p['ssm_a_im'], p['ssm_log_dt'],
                                      p['ssm_b_re'], p['ssm_b_im'], p['ssm_c_re'], p['ssm_c_im'], p['ssm_d'],
                                      p['w_glu_a'], p['w_glu_b'])
    y_hg, hg_fin = hgrn_mixer(hq, hff, hfb, hi, hgt, lb, hg_s0, p['hg_norm_g'], p['w_hg_o'])
    gates = jax.nn.sigmoid(mg.astype(jnp.float32)).reshape(B, L, N_BRANCHES, D_MODEL)
    merged = gates[:, :, 0] * y_att + gates[:, :, 1] * y_ssm + gates[:, :, 2] * y_hg
    x = x + g1 * (merged.astype(x.dtype) @ p['w_out'])
    h2 = rmsnorm(x, p['norm2_g']) * (1.0 + sc2) + sh2
    x = x + g2 * hier_moe(h2, p['w_group'], p['b_group'], p['w_router'], p['b_router'],
                          p['w_e_gate'], p['w_e_up'], p['w_e_down'])
    ctx_tensors = (k, v, ssm_re, ssm_im, hg_fin) if cache is None else None
    return x, ctx_tensors


def setup_inputs(seed: int = 0) -> dict:
    key = jax.random.key(seed)
    ks = iter(jax.random.split(key, 48))
    f32 = jnp.float32

    def nrm(shape, scale):
        return jax.random.normal(next(ks), shape, f32) * scale

    x_prompt = nrm((BATCH, SEQ, D_MODEL), 1.0)
    x_sample = nrm((DEC_BATCH, DEC_SEQ, D_MODEL), 1.0)
    cache_k = nrm((DEC_BATCH, DEPTH, PAST_LEN, N_KV_HEADS, HEAD_DIM), 1.0)
    cache_v = nrm((DEC_BATCH, DEPTH, PAST_LEN, N_KV_HEADS, HEAD_DIM), 1.0)
    state_ssm_re = nrm((DEC_BATCH, DEPTH, 2, SSM_GROUPS, SSM_STATE), 0.1)
    state_ssm_im = nrm((DEC_BATCH, DEPTH, 2, SSM_GROUPS, SSM_STATE), 0.1)
    state_hgrn = nrm((DEC_BATCH, DEPTH, 2, HG_HEADS, HG_KDIM, HG_VDIM), 0.3)
    c = nrm((DEC_BATCH, D_MODEL), 1.0)
    c_ctx = nrm((D_MODEL,), 1.0)
    w_mod = nrm((DEPTH, D_MODEL, 6 * D_MODEL), 0.5 * D_MODEL ** -0.5)
    b_mod = nrm((DEPTH, 6 * D_MODEL), 0.02)
    norm1_g = 1.0 + nrm((DEPTH, D_MODEL), 0.02)
    norm2_g = 1.0 + nrm((DEPTH, D_MODEL), 0.02)
    w_in = nrm((DEPTH, D_MODEL, IN_WIDTH), D_MODEL ** -0.5)
    q_norm_g = 1.0 + nrm((DEPTH, HEAD_DIM), 0.02)
    k_norm_g = 1.0 + nrm((DEPTH, HEAD_DIM), 0.02)
    attn_sink = nrm((DEPTH, N_HEADS), 0.1)
    w_attn_o = nrm((DEPTH, Q_W, D_MODEL), Q_W ** -0.5)
    ssm_a_re = -0.5 + nrm((DEPTH, 2, SSM_GROUPS, SSM_STATE), 0.01)
    ssm_a_im = jnp.pi * jnp.arange(SSM_STATE, dtype=f32) + nrm((DEPTH, 2, SSM_GROUPS, SSM_STATE), 0.01)
    ssm_log_dt = jax.random.uniform(next(ks), (DEPTH, 2, SSM_GROUPS), f32, math.log(1e-3), math.log(1e-1))
    ssm_b_re = nrm((DEPTH, SSM_GROUPS, SSM_STATE, SSM_GROUP), (2 * SSM_GROUP) ** -0.5)
    ssm_b_im = nrm((DEPTH, SSM_GROUPS, SSM_STATE, SSM_GROUP), (2 * SSM_GROUP) ** -0.5)
    ssm_c_re = nrm((DEPTH, SSM_GROUPS, SSM_GROUP, SSM_STATE), (2 * SSM_STATE) ** -0.5)
    ssm_c_im = nrm((DEPTH, SSM_GROUPS, SSM_GROUP, SSM_STATE), (2 * SSM_STATE) ** -0.5)
    ssm_d = nrm((DEPTH, SSM_GROUPS, SSM_GROUP), 1.0)
    w_glu_a = nrm((DEPTH, SSM_WIDTH, D_MODEL), SSM_WIDTH ** -0.5)
    w_glu_b = nrm((DEPTH, SSM_WIDTH, D_MODEL), SSM_WIDTH ** -0.5)
    hg_lb = nrm((DEPTH, 2, HG_KW), 0.1)
    hg_norm_g = 1.0 + nrm((DEPTH, HG_VDIM), 0.02)
    w_hg_o = nrm((DEPTH, HG_VW, D_MODEL), HG_VW ** -0.5)
    w_out = nrm((DEPTH, D_MODEL, D_MODEL), D_MODEL ** -0.5)
    w_group = nrm((DEPTH, D_MODEL, N_GROUPS), D_MODEL ** -0.5)
    b_group = nrm((DEPTH, N_GROUPS), 0.01)
    w_router = nrm((DEPTH, D_MODEL, N_EXPERTS), D_MODEL ** -0.5)
    b_router = nrm((DEPTH, N_EXPERTS), 0.01)
    w_e_gate = nrm((DEPTH, N_EXPERTS, D_MODEL, EXPERT_FF), D_MODEL ** -0.5)
    w_e_up = nrm((DEPTH, N_EXPERTS, D_MODEL, EXPERT_FF), D_MODEL ** -0.5)
    w_e_down = nrm((DEPTH, N_EXPERTS, EXPERT_FF, D_MODEL), EXPERT_FF ** -0.5)
    return {'x_prompt': x_prompt, 'x_sample': x_sample, 'cache_k': cache_k, 'cache_v': cache_v,
            'state_ssm_re': state_ssm_re, 'state_ssm_im': state_ssm_im, 'state_hgrn': state_hgrn,
            'c': c, 'c_ctx': c_ctx, 'w_mod': w_mod, 'b_mod': b_mod, 'norm1_g': norm1_g, 'norm2_g': norm2_g,
            'w_in': w_in, 'q_norm_g': q_norm_g, 'k_norm_g': k_norm_g, 'attn_sink': attn_sink, 'w_attn_o': w_attn_o,
            'ssm_a_re': ssm_a_re, 'ssm_a_im': ssm_a_im, 'ssm_log_dt': ssm_log_dt, 'ssm_b_re': ssm_b_re,
            'ssm_b_im': ssm_b_im, 'ssm_c_re': ssm_c_re, 'ssm_c_im': ssm_c_im, 'ssm_d': ssm_d,
            'w_glu_a': w_glu_a, 'w_glu_b': w_glu_b, 'hg_lb': hg_lb, 'hg_norm_g': hg_norm_g, 'w_hg_o': w_hg_o,
            'w_out': w_out, 'w_group': w_group, 'b_group': b_group, 'w_router': w_router, 'b_router': b_router,
            'w_e_gate': w_e_gate, 'w_e_up': w_e_up, 'w_e_down': w_e_down}


def reference(x_prompt, x_sample, cache_k, cache_v, state_ssm_re, state_ssm_im, state_hgrn, c, c_ctx,
              w_mod, b_mod, norm1_g, norm2_g, w_in, q_norm_g, k_norm_g, attn_sink, w_attn_o,
              ssm_a_re, ssm_a_im, ssm_log_dt, ssm_b_re, ssm_b_im, ssm_c_re, ssm_c_im, ssm_d, w_glu_a, w_glu_b,
              hg_lb, hg_norm_g, w_hg_o, w_out, w_group, b_group, w_router, b_router, w_e_gate, w_e_up, w_e_down):
    lb_soft = jax.nn.softmax(hg_lb.astype(jnp.float32), axis=0)
    lower_bounds = jnp.cumsum(lb_soft, axis=0) - lb_soft[0]
    xp = x_prompt
    xs = x_sample
    new_k, new_v, new_re, new_im, new_hg = [], [], [], [], []
    for l in range(DEPTH):
        p = {'w_mod': w_mod[l], 'b_mod': b_mod[l], 'norm1_g': norm1_g[l], 'norm2_g': norm2_g[l],
             'w_in': w_in[l], 'q_norm_g': q_norm_g[l], 'k_norm_g': k_norm_g[l], 'attn_sink': attn_sink[l],
             'w_attn_o': w_attn_o[l], 'ssm_a_re': ssm_a_re[l], 'ssm_a_im': ssm_a_im[l], 'ssm_log_dt': ssm_log_dt[l],
             'ssm_b_re': ssm_b_re[l], 'ssm_b_im': ssm_b_im[l], 'ssm_c_re': ssm_c_re[l], 'ssm_c_im': ssm_c_im[l],
             'ssm_d': ssm_d[l], 'w_glu_a': w_glu_a[l], 'w_glu_b': w_glu_b[l], 'hg_norm_g': hg_norm_g[l],
             'w_hg_o': w_hg_o[l], 'w_out': w_out[l], 'w_group': w_group[l], 'b_group': b_group[l],
             'w_router': w_router[l], 'b_router': b_router[l], 'w_e_gate': w_e_gate[l], 'w_e_up': w_e_up[l],
             'w_e_down': w_e_down[l]}
        xp, ctx = trunk_layer(xp, c_ctx[None, :], p, lower_bounds[l], None)
        new_k.append(ctx[0])
        new_v.append(ctx[1])
        new_re.append(ctx[2])
        new_im.append(ctx[3])
        new_hg.append(ctx[4])
        xs, _ = trunk_layer(xs, c, p, lower_bounds[l],
                            (cache_k[:, l], cache_v[:, l], state_ssm_re[:, l], state_ssm_im[:, l], state_hgrn[:, l]))
    return (xp, xs, jnp.stack(new_k, axis=1), jnp.stack(new_v, axis=1), jnp.stack(new_re, axis=1),
            jnp.stack(new_im, axis=1), jnp.stack(new_hg, axis=1))
```

```python
import functools
import math

import numpy as np
import jax
import jax.numpy as jnp
from jax import lax
from jax.experimental import pallas as pl
from jax.experimental.pallas import tpu as pltpu

F32 = jnp.float32
BF16 = jnp.bfloat16

D_MODEL = 1024
BATCH = 32
SEQ = 256
DEPTH = 4
DEC_BATCH = 4
DEC_SEQ = 1024
PAST_LEN = 512
GRID_W = 64
N_HEADS = 8
N_KV_HEADS = 2
HEAD_DIM = 64
KV_GROUP = N_HEADS // N_KV_HEADS
WINDOW = 128
ATT_BLOCK = 128
ROPE_BASE = 10000.0
SSM_WIDTH = 512
SSM_GROUP = 16
SSM_GROUPS = SSM_WIDTH // SSM_GROUP
SSM_STATE = 64
HG_HEADS = 4
HG_KDIM = 128
HG_VDIM = 128
N_BRANCHES = 3
N_GROUPS = 4
EXPERTS_PER_GROUP = 4
N_EXPERTS = N_GROUPS * EXPERTS_PER_GROUP
EXPERT_FF = 256
EPS = 1e-6
Q_W = N_HEADS * HEAD_DIM
KV_W = N_KV_HEADS * HEAD_DIM
HG_KW = HG_HEADS * HG_KDIM
HG_VW = HG_HEADS * HG_VDIM

LANES = 128
SUBLANES = 8
VMEM_LIMIT = 56 * 1024 * 1024

MG_W = N_BRANCHES * D_MODEL
COL_MG = 0
COL_HG = COL_MG + MG_W
COL_AQ = COL_HG + 3 * HG_KW + 2 * HG_VW
COL_AK = COL_AQ + Q_W
COL_SU = COL_AK + 2 * KV_W
IN_WIDTH = COL_SU + SSM_WIDTH

SSM_SG = SSM_WIDTH // LANES
SSM_SG_STATES = (LANES // SSM_GROUP) * SSM_STATE
SSM_TC = 32
HG_CHUNK = 128
HG_LEVELS = 7
ROUTER_OFF = N_GROUPS
MASK_NEG = -1e30


def _cparams(sem):
    return pltpu.CompilerParams(dimension_semantics=sem, vmem_limit_bytes=VMEM_LIMIT)


def _sigmoid(x):
    return 1.0 / (1.0 + jnp.exp(-x))


def _silu(x):
    return x * _sigmoid(x)


def _gelu_tanh(x):
    return 0.5 * x * (1.0 + jnp.tanh(math.sqrt(2.0 / math.pi) * (x + 0.044715 * (x * x * x))))


def _dot(a, b):
    return jnp.dot(a, b, preferred_element_type=F32)


def _dot_nt(a, b):
    return lax.dot_general(a, b, (((1,), (1,)), ((), ())), preferred_element_type=F32)


def _dot_tn(a, b):
    return lax.dot_general(a, b, (((0,), (0,)), ((), ())), preferred_element_type=F32)


def _ssm_prep_kernel(are_ref, aim_ref, ldt_ref, bre_ref, bim_ref, lre_ref, lim_ref, bbre_ref, bbim_ref):
    a_re = are_ref[...]
    a_im = aim_ref[...]
    dt = jnp.exp(ldt_ref[...])
    mag = jnp.exp(a_re * dt)
    lam_re = mag * jnp.cos(a_im * dt)
    lam_im = mag * jnp.sin(a_im * dt)
    nr = lam_re - 1.0
    ni = lam_im
    den = a_re * a_re + a_im * a_im
    f_re = (nr * a_re + ni * a_im) / den
    f_im = (ni * a_re - nr * a_im) / den
    b_re = bre_ref[...]
    b_im = bim_ref[...]
    lre_ref[...] = lam_re
    lim_ref[...] = lam_im
    bbre_ref[...] = f_re * b_re - f_im * b_im
    bbim_ref[...] = f_re * b_im + f_im * b_re


def _ssm_prep(ssm_a_re, ssm_a_im, ssm_log_dt, ssm_b_re, ssm_b_im):
    shp = (DEPTH, 2, SSM_GROUPS, SSM_GROUP, SSM_STATE)
    rows = DEPTH * 2
    width = SSM_GROUPS * SSM_GROUP * SSM_STATE
    bc = lambda t: jnp.broadcast_to(t, shp).reshape(rows, width)
    a_re = bc(ssm_a_re[:, :, :, None, :])
    a_im = bc(ssm_a_im[:, :, :, None, :])
    ldt = bc(ssm_log_dt[:, :, :, None, None])
    b_re = bc(jnp.swapaxes(ssm_b_re, -1, -2)[:, None])
    b_im = bc(jnp.swapaxes(ssm_b_im, -1, -2)[:, None])
    tile = 4096
    spec = pl.BlockSpec((rows, tile), lambda i: (0, i))
    outs = pl.pallas_call(
        _ssm_prep_kernel,
        grid=(width // tile,),
        in_specs=[spec] * 5,
        out_specs=[spec] * 4,
        out_shape=[jax.ShapeDtypeStruct((rows, width), F32)] * 4,
        name="ssm_prep",
    )(a_re, a_im, ldt, b_re, b_im)
    lam_re, lam_im, bb_re, bb_im = [o.reshape(shp) for o in outs]
    gl = LANES // SSM_GROUP
    eye = jnp.eye(gl, dtype=F32)
    lam = jnp.stack([lam_re[:, :, :, 0, :], lam_im[:, :, :, 0, :]], axis=2)
    lam = lam.reshape(DEPTH, 2, 2, SSM_SG, SSM_SG_STATES).transpose(0, 1, 3, 2, 4)
    bbs = jnp.stack([bb_re, bb_im], axis=0).reshape(2, DEPTH, 2, SSM_SG, gl, SSM_GROUP, SSM_STATE)
    bd = jnp.einsum('rldsgcp,gh->ldsgcrhp', bbs, eye).reshape(DEPTH, 2, SSM_SG, LANES, 2 * SSM_SG_STATES)
    return lam, bd.astype(BF16)


def _ssm_out_map(ssm_c_re, ssm_c_im):
    gl = LANES // SSM_GROUP
    eye = jnp.eye(gl, dtype=F32)
    cs = jnp.stack([ssm_c_re, -ssm_c_im], axis=0).reshape(2, DEPTH, SSM_SG, gl, SSM_GROUP, SSM_STATE)
    cd = jnp.einsum('rlsgcp,gh->lsrgphc', cs, eye).reshape(DEPTH, SSM_SG, 2 * SSM_SG_STATES, LANES)
    return cd.astype(BF16)


def _lb_kernel(x_ref, o_ref):
    x = x_ref[...]
    m = jnp.max(x, axis=0, keepdims=True)
    e = jnp.exp(x - m)
    s = e / jnp.sum(e, axis=0, keepdims=True)
    run = jnp.zeros_like(s[0:1])
    o_ref[0:1, :] = run
    for l in range(1, DEPTH):
        run = run + s[l:l + 1]
        o_ref[l:l + 1, :] = run


def _lower_bounds(hg_lb):
    w = 2 * HG_KW
    out = pl.pallas_call(
        _lb_kernel,
        out_shape=jax.ShapeDtypeStruct((DEPTH, w), F32),
        name="hgrn_lower_bounds",
    )(hg_lb.reshape(DEPTH, w))
    return out.reshape(DEPTH, 2, HG_HEADS, HG_KDIM).transpose(0, 2, 1, 3)


def _mod_kernel(c_ref, w_ref, b_ref, o_ref):
    c = c_ref[...]
    a = _silu(c).astype(BF16)
    o_ref[...] = _dot(a, w_ref[...].astype(BF16)) + b_ref[...]


def _modulation(c, c_ctx, w_mod, b_mod):
    rows = SUBLANES
    cond = jnp.concatenate([c_ctx[None, :], c, jnp.zeros((rows - 1 - DEC_BATCH, D_MODEL), F32)], axis=0)
    tn = D_MODEL
    out = pl.pallas_call(
        _mod_kernel,
        grid=(DEPTH, 6),
        in_specs=[
            pl.BlockSpec((rows, D_MODEL), lambda l, j: (0, 0)),
            pl.BlockSpec((None, D_MODEL, tn), lambda l, j: (l, 0, j)),
            pl.BlockSpec((None, 1, tn), lambda l, j: (l, 0, j)),
        ],
        out_specs=pl.BlockSpec((None, rows, tn), lambda l, j: (l, 0, j)),
        out_shape=jax.ShapeDtypeStruct((DEPTH, rows, 6 * D_MODEL), F32),
        compiler_params=_cparams(("parallel", "parallel")),
        name="modulation",
    )(cond, w_mod, b_mod.reshape(DEPTH, 1, 6 * D_MODEL))
    return out.reshape(DEPTH, rows, 6, D_MODEL)


def _inproj_kernel(x_ref, mod_ref, g_ref, w_ref, o_ref, h_scr):
    @pl.when(pl.program_id(1) == 0)
    def _():
        x = x_ref[...]
        ms = jnp.mean(x * x, axis=-1, keepdims=True)
        y = x * lax.rsqrt(ms + EPS) * g_ref[...]
        h = y * (1.0 + mod_ref[1:2, :]) + mod_ref[0:1, :]
        h_scr[...] = h.astype(BF16)

    o_ref[...] = _dot(h_scr[...], w_ref[...])


def _in_proj(x, mod, norm_g, w_in, layer, seq_len, cond_row0):
    n = x.shape[0]
    tm = 512
    tn = 2304
    rows_per_cond = seq_len if cond_row0 else n
    mod_idx = (lambda i, j: (layer, cond_row0 + (i * tm) // rows_per_cond, 0, 0))
    return pl.pallas_call(
        _inproj_kernel,
        grid=(n // tm, IN_WIDTH // tn),
        in_specs=[
            pl.BlockSpec((tm, D_MODEL), lambda i, j: (i, 0)),
            pl.BlockSpec((None, None, 6, D_MODEL), mod_idx),
            pl.BlockSpec((None, 1, D_MODEL), lambda i, j: (layer, 0, 0)),
            pl.BlockSpec((None, D_MODEL, tn), lambda i, j: (layer, 0, j)),
        ],
        out_specs=pl.BlockSpec((tm, tn), lambda i, j: (i, j)),
        out_shape=jax.ShapeDtypeStruct((n, IN_WIDTH), F32),
        scratch_shapes=[pltpu.VMEM((tm, D_MODEL), BF16)],
        compiler_params=_cparams(("parallel", "arbitrary")),
        name="in_proj",
    )(x, mod, norm_g, w_in)


def _pair_norm(x, g):
    xx = x * x
    s_a = jnp.sum(xx[:, :HEAD_DIM], axis=-1, keepdims=True)
    s_b = jnp.sum(xx[:, HEAD_DIM:], axis=-1, keepdims=True)
    lane = lax.broadcasted_iota(jnp.int32, x.shape, 1)
    ms = jnp.where(lane < HEAD_DIM, s_a, s_b) * (1.0 / HEAD_DIM)
    return x * lax.rsqrt(ms + EPS) * g


def _rope(x, cos, sin_signed):
    lane = lax.broadcasted_iota(jnp.int32, x.shape, 1)
    first = (lane & 31) < 16
    partner = jnp.where(first, pltpu.roll(x, LANES - 16, 1), pltpu.roll(x, 16, 1))
    return x * cos + partner * sin_signed


def _sink_column(sink_ref, layer, kh, t):
    row = lax.broadcasted_iota(jnp.int32, (KV_GROUP * t, 1), 0)
    col = jnp.full((KV_GROUP * t, 1), sink_ref[layer, kh * KV_GROUP], F32)
    for g in range(1, KV_GROUP):
        col = jnp.where(row >= g * t, sink_ref[layer, kh * KV_GROUP + g], col)
    return col


def _sink_softmax_av(scores, values, sink_col):
    m = sink_col
    for s in scores:
        m = jnp.maximum(m, jnp.max(s, axis=-1, keepdims=True))
    den = jnp.exp(sink_col - m)
    o = None
    for s, v in zip(scores, values):
        p = jnp.exp(s - m)
        den = den + jnp.sum(p, axis=-1, keepdims=True)
        pv = _dot(p.astype(BF16), v.astype(BF16))
        o = pv if o is None else o + pv
    return o / den


def _attn_ctx_kernel(sink_ref, q_ref, kv_ref, gq_ref, gk_ref, att_ref, k_ref, v_ref, *, layer):
    t = q_ref.shape[0]
    kv = kv_ref[...]
    k = _pair_norm(kv[:, :KV_W], gk_ref[...])
    v = kv[:, KV_W:]
    k_ref[...] = k
    v_ref[...] = v
    q = q_ref[...]
    scale = 1.0 / math.sqrt(HEAD_DIM)
    outs = []
    for kh in range(N_KV_HEADS):
        heads = []
        for j in range(KV_GROUP // 2):
            c0 = (kh * (KV_GROUP // 2) + j) * LANES
            qn = _pair_norm(q[:, c0:c0 + LANES], gq_ref[...])
            heads += [qn[:, :HEAD_DIM], qn[:, HEAD_DIM:]]
        qs = jnp.concatenate(heads, axis=0).astype(BF16)
        kk = k[:, kh * HEAD_DIM:(kh + 1) * HEAD_DIM]
        vv = v[:, kh * HEAD_DIM:(kh + 1) * HEAD_DIM]
        s = _dot_nt(qs, kk.astype(BF16)) * scale
        sk = _sink_column(sink_ref, layer, kh, t)
        o = _sink_softmax_av([s], [vv], sk)
        outs += [o[g * t:(g + 1) * t] for g in range(KV_GROUP)]
    att_ref[...] = jnp.concatenate(outs, axis=-1).astype(BF16)


def _attn_ctx(sink, p, gq, gk, layer):
    n = p.shape[0]
    t = SEQ
    kern = functools.partial(_attn_ctx_kernel, layer=layer)
    return pl.pallas_call(
        kern,
        grid=(n // t,),
        in_specs=[
            pl.BlockSpec(memory_space=pltpu.SMEM),
            pl.BlockSpec((t, Q_W), lambda b: (b, COL_AQ // Q_W)),
            pl.BlockSpec((t, 2 * KV_W), lambda b: (b, COL_AK // (2 * KV_W))),
            pl.BlockSpec((None, 1, LANES), lambda b: (layer, 0, 0)),
            pl.BlockSpec((None, 1, LANES), lambda b: (layer, 0, 0)),
        ],
        out_specs=[
            pl.BlockSpec((t, Q_W), lambda b: (b, 0)),
            pl.BlockSpec((t, KV_W), lambda b: (b, 0)),
            pl.BlockSpec((t, KV_W), lambda b: (b, 0)),
        ],
        out_shape=[
            jax.ShapeDtypeStruct((n, Q_W), BF16),
            jax.ShapeDtypeStruct((n, KV_W), F32),
            jax.ShapeDtypeStruct((n, KV_W), F32),
        ],
        compiler_params=_cparams(("parallel",)),
        name="attn_ctx",
    )(sink, p, p, gq, gk)


def _attn_lat_kernel(sink_ref, q_ref, kv_ref, kc_ref, vc_ref, gq_ref, gk_ref, cos_ref, sin_ref, att_ref, *, layer):
    t = ATT_BLOCK
    span = ATT_BLOCK + 2 * WINDOW
    start = pl.program_id(1) * ATT_BLOCK
    ks = pl.multiple_of(jnp.clip(start - WINDOW, 0, DEC_SEQ - span), ATT_BLOCK)
    q0 = pl.multiple_of(start, ATT_BLOCK)
    kvw = kv_ref[pl.ds(ks, span), :]
    k = _rope(_pair_norm(kvw[:, :KV_W], gk_ref[...]), cos_ref[pl.ds(ks, span), :], sin_ref[pl.ds(ks, span), :])
    v = kvw[:, KV_W:]
    cos_q = cos_ref[pl.ds(q0, t), :]
    sin_q = sin_ref[pl.ds(q0, t), :]
    qpos = start + (lax.broadcasted_iota(jnp.int32, (KV_GROUP * t, span), 0) & (t - 1))
    kpos = ks + lax.broadcasted_iota(jnp.int32, (KV_GROUP * t, span), 1)
    band = jnp.abs(qpos - kpos) <= WINDOW
    q = q_ref[...]
    kc = kc_ref[...]
    vc = vc_ref[...]
    scale = 1.0 / math.sqrt(HEAD_DIM)
    outs = []
    for kh in range(N_KV_HEADS):
        heads = []
        for j in range(KV_GROUP // 2):
            c0 = (kh * (KV_GROUP // 2) + j) * LANES
            qn = _rope(_pair_norm(q[:, c0:c0 + LANES], gq_ref[...]), cos_q, sin_q)
            heads += [qn[:, :HEAD_DIM], qn[:, HEAD_DIM:]]
        qs = jnp.concatenate(heads, axis=0).astype(BF16)
        sl = slice(kh * HEAD_DIM, (kh + 1) * HEAD_DIM)
        s_w = jnp.where(band, _dot_nt(qs, k[:, sl].astype(BF16)) * scale, MASK_NEG)
        s_c = _dot_nt(qs, kc[:, sl].astype(BF16)) * scale
        sk = _sink_column(sink_ref, layer, kh, t)
        o = _sink_softmax_av([s_w, s_c], [v[:, sl], vc[:, sl]], sk)
        outs += [o[g * t:(g + 1) * t] for g in range(KV_GROUP)]
    att_ref[...] = jnp.concatenate(outs, axis=-1).astype(BF16)


def _attn_lat(sink, p, cache_k, cache_v, gq, gk, cos_t, sin_t, layer):
    n = p.shape[0]
    t = ATT_BLOCK
    nb = DEC_SEQ // t
    kern = functools.partial(_attn_lat_kernel, layer=layer)
    return pl.pallas_call(
        kern,
        grid=(DEC_BATCH, nb),
        in_specs=[
            pl.BlockSpec(memory_space=pltpu.SMEM),
            pl.BlockSpec((t, Q_W), lambda b, j: (b * nb + j, COL_AQ // Q_W)),
            pl.BlockSpec((DEC_SEQ, 2 * KV_W), lambda b, j: (b, COL_AK // (2 * KV_W))),
            pl.BlockSpec((None, None, PAST_LEN, KV_W), lambda b, j: (b, layer, 0, 0)),
            pl.BlockSpec((None, None, PAST_LEN, KV_W), lambda b, j: (b, layer, 0, 0)),
            pl.BlockSpec((None, 1, LANES), lambda b, j: (layer, 0, 0)),
            pl.BlockSpec((None, 1, LANES), lambda b, j: (layer, 0, 0)),
            pl.BlockSpec((DEC_SEQ, LANES), lambda b, j: (0, 0)),
            pl.BlockSpec((DEC_SEQ, LANES), lambda b, j: (0, 0)),
        ],
        out_specs=pl.BlockSpec((t, Q_W), lambda b, j: (b * nb + j, 0)),
        out_shape=jax.ShapeDtypeStruct((n, Q_W), BF16),
        compiler_params=_cparams(("parallel", "parallel")),
        name="attn_lat",
    )(sink, p, p, cache_k, cache_v, gq, gk, cos_t, sin_t)


def _rope_tables():
    pos = jnp.arange(DEC_SEQ)
    rows = (pos // GRID_W).astype(F32)
    cols = (pos % GRID_W).astype(F32)
    half = HEAD_DIM // 2
    freqs = ROPE_BASE ** (-jnp.arange(0, half, 2, dtype=F32) / half)
    ang_r = rows[:, None] * freqs[None, :]
    ang_c = cols[:, None] * freqs[None, :]
    ang = jnp.concatenate([ang_r, ang_r, ang_c, ang_c], axis=-1)
    sign = jnp.tile(jnp.concatenate([-jnp.ones((16,), F32), jnp.ones((16,), F32)]), 2)
    cos_t = jnp.tile(jnp.cos(ang), (1, LANES // HEAD_DIM))
    sin_t = jnp.tile(jnp.sin(ang) * sign[None, :], (1, LANES // HEAD_DIM))
    return cos_t, sin_t


def _ssm_kernel(*refs, nseq, seq_len, has_h0):
    if has_h0:
        (u_ref, bd_ref, cd_ref, lam_ref, d_ref, h0re_ref, h0im_ref,
         z_ref, lhs_scr, bu_scr, s_scr, y_scr) = refs
    else:
        (u_ref, bd_ref, cd_ref, lam_ref, d_ref,
         z_ref, fre_ref, fim_ref, lhs_scr, bu_scr, s_scr, y_scr) = refs
    tc = SSM_TC
    ns = SSM_SG_STATES
    dr = pl.program_id(2)
    lam_re = jnp.broadcast_to(lam_ref[0:1, :], (SUBLANES, ns))
    lam_im = jnp.broadcast_to(lam_ref[1:2, :], (SUBLANES, ns))

    @pl.when(dr == 0)
    def _():
        y_scr[...] = jnp.zeros_like(y_scr)

    lhs_scr[...] = jnp.zeros_like(lhs_scr)

    def tok(step):
        return jnp.where(dr == 0, step, seq_len - 1 - step)

    def chunk(ci, carry):
        s_re, s_im = carry
        base = ci * tc
        for i in range(tc):
            lhs_scr[i * SUBLANES:i * SUBLANES + nseq, :] = u_ref[pl.ds(tok(base + i), nseq, stride=seq_len), :]
        bu_scr[...] = _dot(lhs_scr[...].astype(BF16), bd_ref[...])
        for i in range(tc):
            r = slice(i * SUBLANES, (i + 1) * SUBLANES)
            b_re = bu_scr[r, 0:ns]
            b_im = bu_scr[r, ns:2 * ns]
            n_re = lam_re * s_re - lam_im * s_im + b_re
            n_im = lam_re * s_im + lam_im * s_re + b_im
            s_re, s_im = n_re, n_im
            s_scr[r, 0:ns] = s_re
            s_scr[r, ns:2 * ns] = s_im
        y = _dot(s_scr[...].astype(BF16), cd_ref[...])
        for i in range(tc):
            idx = pl.ds(tok(base + i), nseq, stride=seq_len)
            y_scr[idx, :] = y_scr[idx, :] + y[i * SUBLANES:i * SUBLANES + nseq, :]
        return s_re, s_im

    if has_h0:
        init = (h0re_ref[...], h0im_ref[...])
    else:
        init = (jnp.zeros((SUBLANES, ns), F32), jnp.zeros((SUBLANES, ns), F32))
    s_re, s_im = lax.fori_loop(0, seq_len // tc, chunk, init)
    if not has_h0:
        fre_ref[...] = s_re
        fim_ref[...] = s_im

    @pl.when(dr == 1)
    def _():
        z_ref[...] = _gelu_tanh(y_scr[...] + d_ref[...] * u_ref[...]).astype(BF16)


def _ssm(p, bd, cd, lam, d_skip, layer, nseq, seq_len, h0=None):
    n = p.shape[0]
    rows = nseq * seq_len
    nslab = n // rows
    has_h0 = h0 is not None
    ns = SSM_SG_STATES
    kern = functools.partial(_ssm_kernel, nseq=nseq, seq_len=seq_len, has_h0=has_h0)
    in_specs = [
        pl.BlockSpec((rows, LANES), lambda s, g, d: (s, COL_SU // LANES + g)),
        pl.BlockSpec((None, None, None, LANES, 2 * ns), lambda s, g, d: (layer, d, g, 0, 0)),
        pl.BlockSpec((None, None, 2 * ns, LANES), lambda s, g, d: (layer, g, 0, 0)),
        pl.BlockSpec((None, None, None, 2, ns), lambda s, g, d: (layer, d, g, 0, 0)),
        pl.BlockSpec((None, None, 1, LANES), lambda s, g, d: (layer, g, 0, 0)),
    ]
    args = [p, bd, cd, lam, d_skip]
    out_specs = [pl.BlockSpec((rows, LANES), lambda s, g, d: (s, g))]
    out_shape = [jax.ShapeDtypeStruct((n, SSM_WIDTH), BF16)]
    if has_h0:
        h_spec = pl.BlockSpec((None, None, None, SUBLANES, ns), lambda s, g, d: (layer, d, g, 0, 0))
        in_specs += [h_spec, h_spec]
        args += [h0[0], h0[1]]
    else:
        f_spec = pl.BlockSpec((None, SUBLANES, ns), lambda s, g, d: (d, s, g))
        out_specs += [f_spec, f_spec]
        out_shape += [jax.ShapeDtypeStruct((2, n // seq_len, SSM_SG * ns), F32)] * 2
    return pl.pallas_call(
        kern,
        grid=(nslab, SSM_SG, 2),
        in_specs=in_specs,
        out_specs=out_specs,
        out_shape=out_shape,
        scratch_shapes=[
            pltpu.VMEM((SSM_TC * SUBLANES, LANES), F32),
            pltpu.VMEM((SSM_TC * SUBLANES, 2 * ns), F32),
            pltpu.VMEM((SSM_TC * SUBLANES, 2 * ns), F32),
            pltpu.VMEM((rows, LANES), F32),
        ],
        compiler_params=_cparams(("parallel", "parallel", "arbitrary")),
        name="ssm",
    )(*args)


def _hgrn_consts():
    c = HG_CHUNK
    i = np.arange(c)[:, None]
    r = np.arange(c)[None, :]
    m_all = np.zeros((2, HG_LEVELS + 2, c, c), np.float32)
    lev_of = np.full((2, c, c), -1, np.int32)
    for lev in range(HG_LEVELS):
        n = 2 << lev
        half = n // 2
        mid = (i // n) * n + half
        upper = (i % n) >= half
        m_all[0, lev] = np.where(upper, (r >= mid) & (r <= i), (r > i) & (r < mid))
        m_all[1, lev] = np.where(upper, (r >= mid) & (r < i), (r >= i) & (r < mid))
        same = (i // n) == (r // n)
        r_upper = (r % n) >= half
        lev_of[0][same & upper & ~r_upper] = lev
        lev_of[1][same & ~upper & r_upper] = lev
    m_all[0, HG_LEVELS] = r <= i
    m_all[0, HG_LEVELS + 1] = r > i
    m_all[1, HG_LEVELS] = r >= i
    m_all[1, HG_LEVELS + 1] = r < i
    return (jnp.asarray(m_all.reshape(2, (HG_LEVELS + 2) * c, c), BF16), jnp.asarray(lev_of))


def _hgrn_kernel(*refs, seq_len, has_s0):
    if has_s0:
        (hq_ref, ff_ref, fb_ref, hi_ref, hgt_ref, lb_ref, ng_ref, m_ref, lev_ref, s0_ref,
         o_ref, osum_scr) = refs
    else:
        (hq_ref, ff_ref, fb_ref, hi_ref, hgt_ref, lb_ref, ng_ref, m_ref, lev_ref,
         o_ref, sfin_ref, osum_scr) = refs
    c = HG_CHUNK
    nch = seq_len // c
    row = lax.broadcasted_iota(jnp.int32, (c, HG_KDIM), 0)

    for d in range(2):
        lb = lb_ref[d:d + 1, :]
        f_ref = ff_ref if d == 0 else fb_ref

        def chunk(ci, st, d=d, lb=lb, f_ref=f_ref):
            cidx = ci if d == 0 else nch - 1 - ci
            r0 = pl.multiple_of(cidx * c, c)
            q = _silu(hq_ref[pl.ds(r0, c), :])
            fg = lb + (1.0 - lb) * _sigmoid(f_ref[pl.ds(r0, c), :])
            k = 1.0 - fg
            logf = jnp.log(fg)
            v = hi_ref[pl.ds(r0, c), :]
            vb = v.astype(BF16)
            l1 = logf.astype(BF16)
            r1 = logf - l1.astype(F32)
            l2 = r1.astype(BF16)
            l3 = (r1 - l2.astype(F32)).astype(BF16)
            mm = m_ref[d]
            g_all = jnp.exp(_dot(mm, l1) + _dot(mm, l2) + _dot(mm, l3))
            lev_of = lev_ref[d]
            att = jnp.zeros((c, c), F32)
            for lev in range(HG_LEVELS):
                half = 1 << lev
                gl = g_all[lev * c:(lev + 1) * c]
                is_upper = (row & half) != 0
                q_side = is_upper if d == 0 else jnp.logical_not(is_upper)
                qt = jnp.where(q_side, q * gl, 0.0).astype(BF16)
                kt = jnp.where(q_side, 0.0, k * gl).astype(BF16)
                att = att + jnp.where(lev_of == lev, _dot_nt(qt, kt), 0.0)
            gq = g_all[HG_LEVELS * c:(HG_LEVELS + 1) * c]
            gk = g_all[(HG_LEVELS + 1) * c:(HG_LEVELS + 2) * c]
            diag = jnp.sum(q * k, axis=-1, keepdims=True)
            o = (_dot(att.astype(BF16), vb) + diag * v
                 + _dot_nt((q * gq).astype(BF16), st.astype(BF16)))
            if d == 0:
                osum_scr[pl.ds(r0, c), :] = o
            else:
                osum_scr[pl.ds(r0, c), :] = osum_scr[pl.ds(r0, c), :] + o
            total = gq[c - 1:c, :] if d == 0 else gq[0:1, :]
            return st * total + _dot_tn(vb, (k * gk).astype(BF16))

        if has_s0:
            st0 = s0_ref[d].T
        else:
            st0 = jnp.zeros((HG_VDIM, HG_KDIM), F32)
        st = lax.fori_loop(0, nch, chunk, st0)
        if not has_s0:
            sfin_ref[d] = st.T

    o = osum_scr[...]
    ms = jnp.mean(o * o, axis=-1, keepdims=True)
    y = o * lax.rsqrt(ms + EPS) * ng_ref[...]
    o_ref[...] = (y * _silu(hgt_ref[...])).astype(BF16)


def _hgrn(p, lb, norm_g, m_all, lev_of, layer, seq_len, s0=None):
    n = p.shape[0]
    nseq = n // seq_len
    has_s0 = s0 is not None
    kern = functools.partial(_hgrn_kernel, seq_len=seq_len, has_s0=has_s0)
    base = COL_HG // LANES

    def col(k):
        return pl.BlockSpec((seq_len, LANES), lambda b, h, k=k: (b, base + k * HG_HEADS + h))

    c = HG_CHUNK
    in_specs = [col(0), col(1), col(2), col(3), col(4),
                pl.BlockSpec((None, None, 2, HG_KDIM), lambda b, h: (layer, h, 0, 0)),
                pl.BlockSpec((None, 1, HG_VDIM), lambda b, h: (layer, 0, 0)),
                pl.BlockSpec((2, (HG_LEVELS + 2) * c, c), lambda b, h: (0, 0, 0)),
                pl.BlockSpec((2, c, c), lambda b, h: (0, 0, 0))]
    args = [p, p, p, p, p, lb, norm_g, m_all, lev_of]
    out_specs = [pl.BlockSpec((seq_len, HG_VDIM), lambda b, h: (b, h))]
    out_shape = [jax.ShapeDtypeStruct((n, HG_VW), BF16)]
    if has_s0:
        in_specs.append(pl.BlockSpec((None, None, 2, None, HG_KDIM, HG_VDIM), lambda b, h: (b, layer, 0, h, 0, 0)))
        args.append(s0)
    else:
        out_specs.append(pl.BlockSpec((None, 2, None, HG_KDIM, HG_VDIM), lambda b, h: (b, 0, h, 0, 0)))
        out_shape.append(jax.ShapeDtypeStruct((nseq, 2, HG_HEADS, HG_KDIM, HG_VDIM), F32))
    return pl.pallas_call(
        kern,
        grid=(nseq, HG_HEADS),
        in_specs=in_specs,
        out_specs=out_specs,
        out_shape=out_shape,
        scratch_shapes=[pltpu.VMEM((seq_len, HG_VDIM), F32)],
        compiler_params=_cparams(("parallel", "parallel")),
        name="hgrn",
    )(*args)


def _router(logits):
    lane = lax.broadcasted_iota(jnp.int32, logits.shape, 1)
    big = jnp.int32(LANES)
    is_g = lane < N_GROUPS
    gl = jnp.where(is_g, logits, -jnp.inf)
    gmax = jnp.max(gl, axis=-1, keepdims=True)
    gsum = jnp.sum(jnp.exp(gl - gmax), axis=-1, keepdims=True)
    g_idx = jnp.min(jnp.where(is_g & (gl == gmax), lane, big), axis=-1, keepdims=True)
    g_prob = 1.0 / gsum
    e_lane = lane - ROUTER_OFF
    sel = (e_lane >= 0) & (e_lane < N_EXPERTS) & ((e_lane >> 2) == g_idx)
    el = jnp.where(sel, logits, -jnp.inf)
    emax = jnp.max(el, axis=-1, keepdims=True)
    eexp = jnp.exp(el - emax)
    ep = eexp / jnp.sum(eexp, axis=-1, keepdims=True)
    p1 = jnp.max(ep, axis=-1, keepdims=True)
    i1 = jnp.min(jnp.where(sel & (ep == p1), lane, big), axis=-1, keepdims=True)
    rest = sel & (lane != i1)
    ep2 = jnp.where(rest, ep, -1.0)
    p2 = jnp.max(ep2, axis=-1, keepdims=True)
    i2 = jnp.min(jnp.where(rest & (ep2 == p2), lane, big), axis=-1, keepdims=True)
    den = p1 + p2
    w1 = g_prob * (p1 / den)
    w2 = g_prob * (p2 / den)
    return jnp.where(lane == i1, w1, jnp.where(lane == i2, w2, 0.0))


def _merge_kernel(x_ref, att_ref, z_ref, hg_ref, mg_ref, mod_ref, n2_ref,
                  wao_ref, wga_ref, wgb_ref, who_ref, wout_ref, wr_ref, br_ref,
                  xm_ref, h2_ref, comb_ref):
    y_att = _dot(att_ref[...], wao_ref[...])
    z = z_ref[...]
    y_ssm = _dot(z, wga_ref[...]) * _sigmoid(_dot(z, wgb_ref[...]))
    y_hg = _dot(hg_ref[...], who_ref[...])
    merged = (_sigmoid(mg_ref[:, 0:D_MODEL]) * y_att
              + _sigmoid(mg_ref[:, D_MODEL:2 * D_MODEL]) * y_ssm
              + _sigmoid(mg_ref[:, 2 * D_MODEL:3 * D_MODEL]) * y_hg)
    xm = x_ref[...] + mod_ref[2:3, :] * _dot(merged.astype(BF16), wout_ref[...])
    xm_ref[...] = xm
    ms = jnp.mean(xm * xm, axis=-1, keepdims=True)
    h2 = xm * lax.rsqrt(ms + EPS) * n2_ref[...] * (1.0 + mod_ref[4:5, :]) + mod_ref[3:4, :]
    h2_ref[...] = h2.astype(BF16)
    logits = jnp.dot(h2, wr_ref[...], preferred_element_type=F32, precision=lax.Precision.HIGHEST) + br_ref[...]
    comb_ref[...] = _router(logits)


def _merge(x, att, z, hg, p, mod, n2g, w, layer, seq_len, cond_row0):
    n = x.shape[0]
    tm = 256
    rows_per_cond = seq_len if cond_row0 else n
    mod_idx = (lambda i: (layer, cond_row0 + (i * tm) // rows_per_cond, 0, 0))
    row = lambda width: pl.BlockSpec((tm, width), lambda i: (i, 0))
    wspec = lambda a, b: pl.BlockSpec((None, a, b), lambda i: (layer, 0, 0))
    return pl.pallas_call(
        _merge_kernel,
        grid=(n // tm,),
        in_specs=[
            row(D_MODEL), row(Q_W), row(SSM_WIDTH), row(HG_VW),
            pl.BlockSpec((tm, MG_W), lambda i: (i, COL_MG // MG_W)),
            pl.BlockSpec((None, None, 6, D_MODEL), mod_idx),
            wspec(1, D_MODEL),
            wspec(Q_W, D_MODEL), wspec(SSM_WIDTH, D_MODEL), wspec(SSM_WIDTH, D_MODEL),
            wspec(HG_VW, D_MODEL), wspec(D_MODEL, D_MODEL), wspec(D_MODEL, LANES), wspec(1, LANES),
        ],
        out_specs=[row(D_MODEL), row(D_MODEL), row(LANES)],
        out_shape=[
            jax.ShapeDtypeStruct((n, D_MODEL), F32),
            jax.ShapeDtypeStruct((n, D_MODEL), BF16),
            jax.ShapeDtypeStruct((n, LANES), F32),
        ],
        compiler_params=_cparams(("parallel",)),
        name="merge",
    )(x, att, z, hg, p, mod, n2g, w['w_attn_o'], w['w_glu_a'], w['w_glu_b'], w['w_hg_o'], w['w_out'],
      w['w_route'], w['b_route'])


def _moe_kernel(h_ref, comb_ref, xm_ref, mod_ref, wg_ref, wu_ref, wd_ref, o_ref, acc_scr):
    e = pl.program_id(1)

    @pl.when(e == 0)
    def _():
        acc_scr[...] = jnp.zeros_like(acc_scr)

    comb = comb_ref[...]
    lane = lax.broadcasted_iota(jnp.int32, comb.shape, 1)
    cw = jnp.sum(jnp.where(lane == e + ROUTER_OFF, comb, 0.0), axis=-1, keepdims=True)
    h = h_ref[...]
    act = _silu(_dot(h, wg_ref[...])) * _dot(h, wu_ref[...]) * cw
    acc_scr[...] += _dot(act.astype(BF16), wd_ref[...])

    @pl.when(e == N_EXPERTS - 1)
    def _():
        o_ref[...] = xm_ref[...] + mod_ref[5:6, :] * acc_scr[...]


def _moe(h2, comb, xm, mod, w, layer, seq_len, cond_row0):
    n = h2.shape[0]
    tm = 512
    rows_per_cond = seq_len if cond_row0 else n
    mod_idx = (lambda i, e: (layer, cond_row0 + (i * tm) // rows_per_cond, 0, 0))
    return pl.pallas_call(
        _moe_kernel,
        grid=(n // tm, N_EXPERTS),
        in_specs=[
            pl.BlockSpec((tm, D_MODEL), lambda i, e: (i, 0)),
            pl.BlockSpec((tm, LANES), lambda i, e: (i, 0)),
            pl.BlockSpec((tm, D_MODEL), lambda i, e: (i, 0)),
            pl.BlockSpec((None, None, 6, D_MODEL), mod_idx),
            pl.BlockSpec((None, None, D_MODEL, EXPERT_FF), lambda i, e: (layer, e, 0, 0)),
            pl.BlockSpec((None, None, D_MODEL, EXPERT_FF), lambda i, e: (layer, e, 0, 0)),
            pl.BlockSpec((None, None, EXPERT_FF, D_MODEL), lambda i, e: (layer, e, 0, 0)),
        ],
        out_specs=pl.BlockSpec((tm, D_MODEL), lambda i, e: (i, 0)),
        out_shape=jax.ShapeDtypeStruct((n, D_MODEL), F32),
        scratch_shapes=[pltpu.VMEM((tm, D_MODEL), F32)],
        compiler_params=_cparams(("parallel", "arbitrary")),
        name="moe",
    )(h2, comb, xm, mod, w['w_e_gate'], w['w_e_up'], w['w_e_down'])


def _layer(x, layer, w, seq_len, cond_row0, cache):
    p = _in_proj(x, w['mod'], w['norm1_g'], w['w_in'], layer, seq_len, cond_row0)
    nseq = x.shape[0] // seq_len
    if cache is None:
        att, k_new, v_new = _attn_ctx(w['attn_sink'], p, w['q_norm_g'], w['k_norm_g'], layer)
        z, f_re, f_im = _ssm(p, w['ssm_bd'], w['ssm_cd'], w['ssm_lam'], w['ssm_d'], layer, SUBLANES, seq_len)
        hg, hg_fin = _hgrn(p, w['hg_lb'], w['hg_norm_g'], w['hg_m'], w['hg_lev'], layer, seq_len)
        ctx = (k_new, v_new, f_re, f_im, hg_fin)
    else:
        cache_k, cache_v, h0_re, h0_im, s0 = cache
        att = _attn_lat(w['attn_sink'], p, cache_k, cache_v, w['q_norm_g'], w['k_norm_g'],
                        w['rope_cos'], w['rope_sin'], layer)
        (z,) = _ssm(p, w['ssm_bd'], w['ssm_cd'], w['ssm_lam'], w['ssm_d'], layer, nseq, seq_len, h0=(h0_re, h0_im))
        (hg,) = _hgrn(p, w['hg_lb'], w['hg_norm_g'], w['hg_m'], w['hg_lev'], layer, seq_len, s0=s0)
        ctx = None
    xm, h2, comb = _merge(x, att, z, hg, p, w['mod'], w['norm2_g'], w, layer, seq_len, cond_row0)
    x = _moe(h2, comb, xm, w['mod'], w, layer, seq_len, cond_row0)
    return x, ctx


def kernel(x_prompt, x_sample, cache_k, cache_v, state_ssm_re, state_ssm_im, state_hgrn, c, c_ctx, w_mod, b_mod, norm1_g, norm2_g, w_in, q_norm_g, k_norm_g, attn_sink, w_attn_o, ssm_a_re, ssm_a_im, ssm_log_dt, ssm_b_re, ssm_b_im, ssm_c_re, ssm_c_im, ssm_d, w_glu_a, w_glu_b, hg_lb, hg_norm_g, w_hg_o, w_out, w_group, b_group, w_router, b_router, w_e_gate, w_e_up, w_e_down):
    w = {}
    w['mod'] = _modulation(c, c_ctx, w_mod, b_mod)
    w['norm1_g'] = norm1_g.reshape(DEPTH, 1, D_MODEL)
    w['norm2_g'] = norm2_g.reshape(DEPTH, 1, D_MODEL)
    w['w_in'] = jnp.concatenate(
        [w_in[:, :, 3840:], w_in[:, :, 1280:3840], w_in[:, :, 0:768], w_in[:, :, 768:1280]], axis=-1).astype(BF16)
    w['q_norm_g'] = jnp.tile(q_norm_g, (1, LANES // HEAD_DIM)).reshape(DEPTH, 1, LANES)
    w['k_norm_g'] = jnp.tile(k_norm_g, (1, LANES // HEAD_DIM)).reshape(DEPTH, 1, LANES)
    w['attn_sink'] = attn_sink
    w['rope_cos'], w['rope_sin'] = _rope_tables()
    w['ssm_lam'], w['ssm_bd'] = _ssm_prep(ssm_a_re, ssm_a_im, ssm_log_dt, ssm_b_re, ssm_b_im)
    w['ssm_cd'] = _ssm_out_map(ssm_c_re, ssm_c_im)
    w['ssm_d'] = ssm_d.reshape(DEPTH, SSM_SG, 1, LANES)
    w['hg_lb'] = _lower_bounds(hg_lb)
    w['hg_norm_g'] = hg_norm_g.reshape(DEPTH, 1, HG_VDIM)
    w['hg_m'], w['hg_lev'] = _hgrn_consts()
    for name, val in (('w_attn_o', w_attn_o), ('w_glu_a', w_glu_a), ('w_glu_b', w_glu_b), ('w_hg_o', w_hg_o),
                      ('w_out', w_out), ('w_e_gate', w_e_gate), ('w_e_up', w_e_up), ('w_e_down', w_e_down)):
        w[name] = val.astype(BF16)
    pad = LANES - N_GROUPS - N_EXPERTS
    w['w_route'] = jnp.concatenate([w_group, w_router, jnp.zeros((DEPTH, D_MODEL, pad), F32)], axis=-1)
    w['b_route'] = jnp.concatenate([b_group, b_router, jnp.zeros((DEPTH, pad), F32)], axis=-1).reshape(DEPTH, 1, LANES)

    ck = cache_k.reshape(DEC_BATCH, DEPTH, PAST_LEN, KV_W)
    cv = cache_v.reshape(DEC_BATCH, DEPTH, PAST_LEN, KV_W)

    def h0_layout(s):
        s = s.reshape(DEC_BATCH, DEPTH, 2, SSM_SG, SSM_SG_STATES).transpose(1, 2, 3, 0, 4)
        return jnp.pad(s, ((0, 0), (0, 0), (0, 0), (0, SUBLANES - DEC_BATCH), (0, 0)))

    h0_re = h0_layout(state_ssm_re)
    h0_im = h0_layout(state_ssm_im)

    xp = x_prompt.reshape(BATCH * SEQ, D_MODEL)
    xs = x_sample.reshape(DEC_BATCH * DEC_SEQ, D_MODEL)
    new_k, new_v, new_re, new_im, new_hg = [], [], [], [], []
    for l in range(DEPTH):
        xp, ctx = _layer(xp, l, w, SEQ, 0, None)
        new_k.append(ctx[0].reshape(BATCH, SEQ, N_KV_HEADS, HEAD_DIM))
        new_v.append(ctx[1].reshape(BATCH, SEQ, N_KV_HEADS, HEAD_DIM))
        new_re.append(ctx[2].transpose(1, 0, 2).reshape(BATCH, 2, SSM_GROUPS, SSM_STATE))
        new_im.append(ctx[3].transpose(1, 0, 2).reshape(BATCH, 2, SSM_GROUPS, SSM_STATE))
        new_hg.append(ctx[4])
        xs, _ = _layer(xs, l, w, DEC_SEQ, 1, (ck, cv, h0_re, h0_im, state_hgrn))
    return (xp.reshape(BATCH, SEQ, D_MODEL), xs.reshape(DEC_BATCH, DEC_SEQ, D_MODEL),
            jnp.stack(new_k, axis=1), jnp.stack(new_v, axis=1), jnp.stack(new_re, axis=1),
            jnp.stack(new_im, axis=1), jnp.stack(new_hg, axis=1))
```

```python
import functools
import math

import numpy as np
import jax
import jax.numpy as jnp
from jax import lax
from jax.experimental import pallas as pl
from jax.experimental.pallas import tpu as pltpu

F32 = jnp.float32
BF16 = jnp.bfloat16

D_MODEL = 1024
BATCH = 32
SEQ = 256
DEPTH = 4
DEC_BATCH = 4
DEC_SEQ = 1024
PAST_LEN = 512
GRID_W = 64
N_HEADS = 8
N_KV_HEADS = 2
HEAD_DIM = 64
KV_GROUP = N_HEADS // N_KV_HEADS
WINDOW = 128
ATT_BLOCK = 128
ROPE_BASE = 10000.0
SSM_WIDTH = 512
SSM_GROUP = 16
SSM_GROUPS = SSM_WIDTH // SSM_GROUP
SSM_STATE = 64
HG_HEADS = 4
HG_KDIM = 128
HG_VDIM = 128
N_BRANCHES = 3
N_GROUPS = 4
EXPERTS_PER_GROUP = 4
N_EXPERTS = N_GROUPS * EXPERTS_PER_GROUP
EXPERT_FF = 256
EPS = 1e-6
Q_W = N_HEADS * HEAD_DIM
KV_W = N_KV_HEADS * HEAD_DIM
HG_KW = HG_HEADS * HG_KDIM
HG_VW = HG_HEADS * HG_VDIM

LANES = 128
SUBLANES = 8
VMEM_LIMIT = 56 * 1024 * 1024

MG_W = N_BRANCHES * D_MODEL
COL_MG = 0
COL_HG = COL_MG + MG_W
COL_AQ = COL_HG + 3 * HG_KW + 2 * HG_VW
COL_AK = COL_AQ + Q_W
COL_SU = COL_AK + 2 * KV_W
IN_WIDTH = COL_SU + SSM_WIDTH

SSM_T = 16
HG_CHUNK = 128
HG_LEVELS = 7
HG_UNROLL = 2
HG_MXU_LEVELS = (1, 2)
ROUTER_OFF = N_GROUPS
MASK_NEG = -1e30


def _cparams(sem):
    return pltpu.CompilerParams(dimension_semantics=sem, vmem_limit_bytes=VMEM_LIMIT)


def _sigmoid(x):
    return 1.0 / (1.0 + jnp.exp(-x))


def _silu(x):
    return x * _sigmoid(x)


def _gelu_tanh(x):
    return 0.5 * x * (1.0 + jnp.tanh(math.sqrt(2.0 / math.pi) * (x + 0.044715 * (x * x * x))))


def _dot(a, b):
    return jnp.dot(a, b, preferred_element_type=F32)


def _dot_nt(a, b):
    return lax.dot_general(a, b, (((1,), (1,)), ((), ())), preferred_element_type=F32)


def _dot_tn(a, b):
    return lax.dot_general(a, b, (((0,), (0,)), ((), ())), preferred_element_type=F32)


def _ssm_prep_kernel(are_ref, aim_ref, ldt_ref, btre_ref, btim_ref, cre_ref, cim_ref,
                     k_ref, bst_ref, cst_ref, lamt_ref):
    t_blk = SSM_T
    ns = SSM_STATE
    jf = lax.broadcasted_iota(jnp.int32, (2 * t_blk, ns), 0).astype(F32)
    bt_re = btre_ref[...]
    bt_im = btim_ref[...]
    c_re = cre_ref[...]
    c_im = cim_ref[...]
    for d in range(2):
        a_re = are_ref[d:d + 1, :]
        a_im = aim_ref[d:d + 1, :]
        dt = jnp.exp(ldt_ref[d:d + 1, :])
        mag = jnp.exp(jf * (a_re * dt))
        ang = jf * (a_im * dt)
        p_re = mag * jnp.cos(ang)
        p_im = mag * jnp.sin(ang)
        nr = p_re[1:2] - 1.0
        ni = p_im[1:2]
        den = a_re * a_re + a_im * a_im
        f_re = (nr * a_re + ni * a_im) / den
        f_im = (ni * a_re - nr * a_im) / den
        bb_re = f_re * bt_re - f_im * bt_im
        bb_im = f_re * bt_im + f_im * bt_re
        a_parts_re, a_parts_im = [], []
        for j in range(t_blk):
            a_parts_re.append(c_re * p_re[j:j + 1] - c_im * p_im[j:j + 1])
            a_parts_im.append(c_re * p_im[j:j + 1] + c_im * p_re[j:j + 1])
        hp = lax.Precision.HIGHEST
        k_ref[d] = (lax.dot_general(jnp.concatenate(a_parts_re, axis=0), bb_re, (((1,), (1,)), ((), ())),
                                    precision=hp, preferred_element_type=F32)
                    - lax.dot_general(jnp.concatenate(a_parts_im, axis=0), bb_im, (((1,), (1,)), ((), ())),
                                      precision=hp, preferred_element_type=F32))
        for t in range(t_blk):
            rows = slice(t * SSM_GROUP, (t + 1) * SSM_GROUP)
            e = t_blk - 1 - t if d == 0 else t
            bst_ref[rows, d * ns:(d + 1) * ns] = bb_re * p_re[e:e + 1] - bb_im * p_im[e:e + 1]
            bst_ref[rows, (2 + d) * ns:(3 + d) * ns] = bb_re * p_im[e:e + 1] + bb_im * p_re[e:e + 1]
            e = t + 1 if d == 0 else t_blk - t
            cst_ref[rows, d * ns:(d + 1) * ns] = c_re * p_re[e:e + 1] - c_im * p_im[e:e + 1]
            cst_ref[rows, (2 + d) * ns:(3 + d) * ns] = -(c_re * p_im[e:e + 1] + c_im * p_re[e:e + 1])
        lamt_ref[0:1, d * ns:(d + 1) * ns] = p_re[t_blk:t_blk + 1]
        lamt_ref[1:2, d * ns:(d + 1) * ns] = p_im[t_blk:t_blk + 1]


def _ssm_prep(ssm_a_re, ssm_a_im, ssm_log_dt, ssm_b_re, ssm_b_im, ssm_c_re, ssm_c_im, ssm_d):
    t_blk = SSM_T
    bw = t_blk * SSM_GROUP
    lgd = lambda a: a.transpose(0, 2, 1, 3)
    ldt = jnp.broadcast_to(ssm_log_dt[..., None], ssm_a_re.shape)
    vec = pl.BlockSpec((None, None, 2, SSM_STATE), lambda l, g: (l, g, 0, 0))
    mat = pl.BlockSpec((None, None, SSM_GROUP, SSM_STATE), lambda l, g: (l, g, 0, 0))
    k, bst, cst, lamt = pl.pallas_call(
        _ssm_prep_kernel,
        grid=(DEPTH, SSM_GROUPS),
        in_specs=[vec, vec, vec, mat, mat, mat, mat],
        out_specs=[
            pl.BlockSpec((None, None, 2, bw, SSM_GROUP), lambda l, g: (l, g, 0, 0, 0)),
            pl.BlockSpec((None, None, bw, 4 * SSM_STATE), lambda l, g: (l, g, 0, 0)),
            pl.BlockSpec((None, None, bw, 4 * SSM_STATE), lambda l, g: (l, g, 0, 0)),
            pl.BlockSpec((None, None, 2, 2 * SSM_STATE), lambda l, g: (l, g, 0, 0)),
        ],
        out_shape=[
            jax.ShapeDtypeStruct((DEPTH, SSM_GROUPS, 2, bw, SSM_GROUP), F32),
            jax.ShapeDtypeStruct((DEPTH, SSM_GROUPS, bw, 4 * SSM_STATE), F32),
            jax.ShapeDtypeStruct((DEPTH, SSM_GROUPS, bw, 4 * SSM_STATE), F32),
            jax.ShapeDtypeStruct((DEPTH, SSM_GROUPS, 2, 2 * SSM_STATE), F32),
        ],
        compiler_params=_cparams(("parallel", "parallel")),
        name="ssm_prep",
    )(lgd(ssm_a_re), lgd(ssm_a_im), lgd(ldt), jnp.swapaxes(ssm_b_re, -1, -2), jnp.swapaxes(ssm_b_im, -1, -2),
      ssm_c_re, ssm_c_im)
    k = k.reshape(DEPTH, SSM_GROUPS, 2, t_blk, SSM_GROUP, SSM_GROUP)
    t_in = np.arange(t_blk)[:, None]
    t_out = np.arange(t_blk)[None, :]
    w = 0.0
    for d, lag in ((0, t_out - t_in), (1, t_in - t_out)):
        blk = k[:, :, d][:, :, np.clip(lag, 0, t_blk - 1)]
        blk = blk * jnp.asarray(lag >= 0, F32)[:, :, None, None]
        w = w + blk.transpose(0, 1, 2, 5, 3, 4).reshape(DEPTH, SSM_GROUPS, bw, bw)
    d_row = jnp.tile(ssm_d, (1, 1, t_blk)).reshape(DEPTH, SSM_GROUPS, 1, bw)
    return dict(w=w.astype(BF16), bst=bst.astype(BF16), cst=cst.astype(BF16), lamt=lamt, d=d_row)


def _lb_kernel(x_ref, o_ref):
    x = x_ref[...]
    m = jnp.max(x, axis=0, keepdims=True)
    e = jnp.exp(x - m)
    s = e / jnp.sum(e, axis=0, keepdims=True)
    run = jnp.zeros_like(s[0:1])
    o_ref[0:1, :] = run
    for l in range(1, DEPTH):
        run = run + s[l:l + 1]
        o_ref[l:l + 1, :] = run


def _lower_bounds(hg_lb):
    w = 2 * HG_KW
    out = pl.pallas_call(
        _lb_kernel,
        out_shape=jax.ShapeDtypeStruct((DEPTH, w), F32),
        name="hgrn_lower_bounds",
    )(hg_lb.reshape(DEPTH, w))
    return out.reshape(DEPTH, 2, HG_HEADS, HG_KDIM).transpose(0, 2, 1, 3)


def _mod_kernel(c_ref, w_ref, b_ref, o_ref):
    c = c_ref[...]
    a = _silu(c).astype(BF16)
    o_ref[...] = _dot(a, w_ref[...].astype(BF16)) + b_ref[...]


def _modulation(c, c_ctx, w_mod, b_mod):
    rows = SUBLANES
    cond = jnp.concatenate([c_ctx[None, :], c, jnp.zeros((rows - 1 - DEC_BATCH, D_MODEL), F32)], axis=0)
    tn = D_MODEL
    out = pl.pallas_call(
        _mod_kernel,
        grid=(DEPTH, 6),
        in_specs=[
            pl.BlockSpec((rows, D_MODEL), lambda l, j: (0, 0)),
            pl.BlockSpec((None, D_MODEL, tn), lambda l, j: (l, 0, j)),
            pl.BlockSpec((None, 1, tn), lambda l, j: (l, 0, j)),
        ],
        out_specs=pl.BlockSpec((None, rows, tn), lambda l, j: (l, 0, j)),
        out_shape=jax.ShapeDtypeStruct((DEPTH, rows, 6 * D_MODEL), F32),
        compiler_params=_cparams(("parallel", "parallel")),
        name="modulation",
    )(cond, w_mod, b_mod.reshape(DEPTH, 1, 6 * D_MODEL))
    return out.reshape(DEPTH, rows, 6, D_MODEL)


def _inproj_kernel(x_ref, mod_ref, g_ref, w_ref, o_ref, h_scr):
    @pl.when(pl.program_id(1) == 0)
    def _():
        x = x_ref[...]
        ms = jnp.mean(x * x, axis=-1, keepdims=True)
        y = x * lax.rsqrt(ms + EPS) * g_ref[...]
        h = y * (1.0 + mod_ref[1:2, :]) + mod_ref[0:1, :]
        h_scr[...] = h.astype(BF16)

    o_ref[...] = _dot(h_scr[...], w_ref[...])


def _in_proj(x, mod, norm_g, w_in, layer, seq_len, cond_row0):
    n = x.shape[0]
    tm = 512
    tn = 2304
    rows_per_cond = seq_len if cond_row0 else n
    mod_idx = (lambda i, j: (layer, cond_row0 + (i * tm) // rows_per_cond, 0, 0))
    return pl.pallas_call(
        _inproj_kernel,
        grid=(n // tm, IN_WIDTH // tn),
        in_specs=[
            pl.BlockSpec((tm, D_MODEL), lambda i, j: (i, 0)),
            pl.BlockSpec((None, None, 6, D_MODEL), mod_idx),
            pl.BlockSpec((None, 1, D_MODEL), lambda i, j: (layer, 0, 0)),
            pl.BlockSpec((None, D_MODEL, tn), lambda i, j: (layer, 0, j)),
        ],
        out_specs=pl.BlockSpec((tm, tn), lambda i, j: (i, j)),
        out_shape=jax.ShapeDtypeStruct((n, IN_WIDTH), F32),
        scratch_shapes=[pltpu.VMEM((tm, D_MODEL), BF16)],
        compiler_params=_cparams(("parallel", "arbitrary")),
        name="in_proj",
    )(x, mod, norm_g, w_in)


def _pair_norm(x, g):
    xx = x * x
    s_a = jnp.sum(xx[:, :HEAD_DIM], axis=-1, keepdims=True)
    s_b = jnp.sum(xx[:, HEAD_DIM:], axis=-1, keepdims=True)
    lane = lax.broadcasted_iota(jnp.int32, x.shape, 1)
    ms = jnp.where(lane < HEAD_DIM, s_a, s_b) * (1.0 / HEAD_DIM)
    return x * lax.rsqrt(ms + EPS) * g


def _rope(x, cos, sin_signed):
    lane = lax.broadcasted_iota(jnp.int32, x.shape, 1)
    first = (lane & 31) < 16
    partner = jnp.where(first, pltpu.roll(x, LANES - 16, 1), pltpu.roll(x, 16, 1))
    return x * cos + partner * sin_signed


def _sink_column(sink_ref, layer, kh, t):
    row = lax.broadcasted_iota(jnp.int32, (KV_GROUP * t, 1), 0)
    col = jnp.full((KV_GROUP * t, 1), sink_ref[layer, kh * KV_GROUP], F32)
    for g in range(1, KV_GROUP):
        col = jnp.where(row >= g * t, sink_ref[layer, kh * KV_GROUP + g], col)
    return col


def _sink_softmax_av(scores, values, sink_col):
    m = sink_col
    for s in scores:
        m = jnp.maximum(m, jnp.max(s, axis=-1, keepdims=True))
    den = jnp.exp(sink_col - m)
    o = None
    for s, v in zip(scores, values):
        p = jnp.exp(s - m)
        den = den + jnp.sum(p, axis=-1, keepdims=True)
        pv = _dot(p.astype(BF16), v.astype(BF16))
        o = pv if o is None else o + pv
    return o / den


def _attn_ctx_kernel(sink_ref, q_ref, kv_ref, gq_ref, gk_ref, att_ref, k_ref, v_ref, *, layer):
    t = q_ref.shape[0]
    kv = kv_ref[...]
    k = _pair_norm(kv[:, :KV_W], gk_ref[...])
    v = kv[:, KV_W:]
    k_ref[...] = k
    v_ref[...] = v
    q = q_ref[...]
    scale = 1.0 / math.sqrt(HEAD_DIM)
    outs = []
    for kh in range(N_KV_HEADS):
        heads = []
        for j in range(KV_GROUP // 2):
            c0 = (kh * (KV_GROUP // 2) + j) * LANES
            qn = _pair_norm(q[:, c0:c0 + LANES], gq_ref[...])
            heads += [qn[:, :HEAD_DIM], qn[:, HEAD_DIM:]]
        qs = jnp.concatenate(heads, axis=0).astype(BF16)
        kk = k[:, kh * HEAD_DIM:(kh + 1) * HEAD_DIM]
        vv = v[:, kh * HEAD_DIM:(kh + 1) * HEAD_DIM]
        s = _dot_nt(qs, kk.astype(BF16)) * scale
        sk = _sink_column(sink_ref, layer, kh, t)
        o = _sink_softmax_av([s], [vv], sk)
        outs += [o[g * t:(g + 1) * t] for g in range(KV_GROUP)]
    att_ref[...] = jnp.concatenate(outs, axis=-1).astype(BF16)


def _attn_ctx(sink, p, gq, gk, layer):
    n = p.shape[0]
    t = SEQ
    kern = functools.partial(_attn_ctx_kernel, layer=layer)
    return pl.pallas_call(
        kern,
        grid=(n // t,),
        in_specs=[
            pl.BlockSpec(memory_space=pltpu.SMEM),
            pl.BlockSpec((t, Q_W), lambda b: (b, COL_AQ // Q_W)),
            pl.BlockSpec((t, 2 * KV_W), lambda b: (b, COL_AK // (2 * KV_W))),
            pl.BlockSpec((None, 1, LANES), lambda b: (layer, 0, 0)),
            pl.BlockSpec((None, 1, LANES), lambda b: (layer, 0, 0)),
        ],
        out_specs=[
            pl.BlockSpec((t, Q_W), lambda b: (b, 0)),
            pl.BlockSpec((t, KV_W), lambda b: (b, 0)),
            pl.BlockSpec((t, KV_W), lambda b: (b, 0)),
        ],
        out_shape=[
            jax.ShapeDtypeStruct((n, Q_W), BF16),
            jax.ShapeDtypeStruct((n, KV_W), F32),
            jax.ShapeDtypeStruct((n, KV_W), F32),
        ],
        compiler_params=_cparams(("parallel",)),
        name="attn_ctx",
    )(sink, p, p, gq, gk)


def _attn_lat_kernel(sink_ref, q_ref, kv_ref, kc_ref, vc_ref, gq_ref, gk_ref, cos_ref, sin_ref, att_ref, *, layer):
    t = ATT_BLOCK
    span = ATT_BLOCK + 2 * WINDOW
    start = pl.program_id(1) * ATT_BLOCK
    ks = pl.multiple_of(jnp.clip(start - WINDOW, 0, DEC_SEQ - span), ATT_BLOCK)
    q0 = pl.multiple_of(start, ATT_BLOCK)
    kvw = kv_ref[pl.ds(ks, span), :]
    k = _rope(_pair_norm(kvw[:, :KV_W], gk_ref[...]), cos_ref[pl.ds(ks, span), :], sin_ref[pl.ds(ks, span), :])
    v = kvw[:, KV_W:]
    cos_q = cos_ref[pl.ds(q0, t), :]
    sin_q = sin_ref[pl.ds(q0, t), :]
    qpos = start + (lax.broadcasted_iota(jnp.int32, (KV_GROUP * t, span), 0) & (t - 1))
    kpos = ks + lax.broadcasted_iota(jnp.int32, (KV_GROUP * t, span), 1)
    band = jnp.abs(qpos - kpos) <= WINDOW
    q = q_ref[...]
    kc = kc_ref[...]
    vc = vc_ref[...]
    scale = 1.0 / math.sqrt(HEAD_DIM)
    outs = []
    for kh in range(N_KV_HEADS):
        heads = []
        for j in range(KV_GROUP // 2):
            c0 = (kh * (KV_GROUP // 2) + j) * LANES
            qn = _rope(_pair_norm(q[:, c0:c0 + LANES], gq_ref[...]), cos_q, sin_q)
            heads += [qn[:, :HEAD_DIM], qn[:, HEAD_DIM:]]
        qs = jnp.concatenate(heads, axis=0).astype(BF16)
        sl = slice(kh * HEAD_DIM, (kh + 1) * HEAD_DIM)
        s_w = jnp.where(band, _dot_nt(qs, k[:, sl].astype(BF16)) * scale, MASK_NEG)
        s_c = _dot_nt(qs, kc[:, sl].astype(BF16)) * scale
        sk = _sink_column(sink_ref, layer, kh, t)
        o = _sink_softmax_av([s_w, s_c], [v[:, sl], vc[:, sl]], sk)
        outs += [o[g * t:(g + 1) * t] for g in range(KV_GROUP)]
    att_ref[...] = jnp.concatenate(outs, axis=-1).astype(BF16)


def _attn_lat(sink, p, cache_k, cache_v, gq, gk, cos_t, sin_t, layer):
    n = p.shape[0]
    t = ATT_BLOCK
    nb = DEC_SEQ // t
    kern = functools.partial(_attn_lat_kernel, layer=layer)
    return pl.pallas_call(
        kern,
        grid=(DEC_BATCH, nb),
        in_specs=[
            pl.BlockSpec(memory_space=pltpu.SMEM),
            pl.BlockSpec((t, Q_W), lambda b, j: (b * nb + j, COL_AQ // Q_W)),
            pl.BlockSpec((DEC_SEQ, 2 * KV_W), lambda b, j: (b, COL_AK // (2 * KV_W))),
            pl.BlockSpec((None, None, PAST_LEN, KV_W), lambda b, j: (b, layer, 0, 0)),
            pl.BlockSpec((None, None, PAST_LEN, KV_W), lambda b, j: (b, layer, 0, 0)),
            pl.BlockSpec((None, 1, LANES), lambda b, j: (layer, 0, 0)),
            pl.BlockSpec((None, 1, LANES), lambda b, j: (layer, 0, 0)),
            pl.BlockSpec((DEC_SEQ, LANES), lambda b, j: (0, 0)),
            pl.BlockSpec((DEC_SEQ, LANES), lambda b, j: (0, 0)),
        ],
        out_specs=pl.BlockSpec((t, Q_W), lambda b, j: (b * nb + j, 0)),
        out_shape=jax.ShapeDtypeStruct((n, Q_W), BF16),
        compiler_params=_cparams(("parallel", "parallel")),
        name="attn_lat",
    )(sink, p, p, cache_k, cache_v, gq, gk, cos_t, sin_t)


def _rope_tables():
    pos = jnp.arange(DEC_SEQ)
    rows = (pos // GRID_W).astype(F32)
    cols = (pos % GRID_W).astype(F32)
    half = HEAD_DIM // 2
    freqs = ROPE_BASE ** (-jnp.arange(0, half, 2, dtype=F32) / half)
    ang_r = rows[:, None] * freqs[None, :]
    ang_c = cols[:, None] * freqs[None, :]
    ang = jnp.concatenate([ang_r, ang_r, ang_c, ang_c], axis=-1)
    sign = jnp.tile(jnp.concatenate([-jnp.ones((16,), F32), jnp.ones((16,), F32)]), 2)
    cos_t = jnp.tile(jnp.cos(ang), (1, LANES // HEAD_DIM))
    sin_t = jnp.tile(jnp.sin(ang) * sign[None, :], (1, LANES // HEAD_DIM))
    return cos_t, sin_t


def _ssm_kernel(*refs, nseq, nblk, has_h0):
    if has_h0:
        x_ref, w_ref, bst_ref, cst_ref, lamt_ref, d_ref, h0_ref, z_ref, loc_scr, ent_scr = refs
    else:
        x_ref, w_ref, bst_ref, cst_ref, lamt_ref, d_ref, z_ref, fin_ref, loc_scr, ent_scr = refs
    ns = SSM_STATE
    x = x_ref[...]
    xb = x.astype(BF16)
    loc_scr[...] = _dot(xb, bst_ref[...])
    l_re = jnp.broadcast_to(lamt_ref[0:1, :], (nseq, 2 * ns))
    l_im = jnp.broadcast_to(lamt_ref[1:2, :], (nseq, 2 * ns))
    is_fwd = lax.broadcasted_iota(jnp.int32, (nseq, 2 * ns), 1) < ns
    if has_h0:
        s_re, s_im = h0_ref[0], h0_ref[1]
    else:
        s_re = jnp.zeros((nseq, 2 * ns), F32)
        s_im = jnp.zeros((nseq, 2 * ns), F32)
    for j in range(nblk):
        rf = slice(j * nseq, (j + 1) * nseq)
        rb = slice((nblk - 1 - j) * nseq, (nblk - j) * nseq)
        ent_scr[rf, 0:ns] = s_re[:, 0:ns]
        ent_scr[rb, ns:2 * ns] = s_re[:, ns:2 * ns]
        ent_scr[rf, 2 * ns:3 * ns] = s_im[:, 0:ns]
        ent_scr[rb, 3 * ns:4 * ns] = s_im[:, ns:2 * ns]
        loc_re = jnp.where(is_fwd, loc_scr[rf, 0:2 * ns], loc_scr[rb, 0:2 * ns])
        loc_im = jnp.where(is_fwd, loc_scr[rf, 2 * ns:4 * ns], loc_scr[rb, 2 * ns:4 * ns])
        n_re = l_re * s_re - l_im * s_im + loc_re
        n_im = l_re * s_im + l_im * s_re + loc_im
        s_re, s_im = n_re, n_im
    if not has_h0:
        fin_ref[0] = s_re
        fin_ref[1] = s_im
    y = _dot(xb, w_ref[...]) + _dot_nt(ent_scr[...].astype(BF16), cst_ref[...]) + d_ref[...] * x
    z_ref[...] = _gelu_tanh(y).astype(BF16)


def _ssm(p, sw, layer, nseq, seq_len, h0=None):
    n = p.shape[0]
    t_blk = SSM_T
    nblk = seq_len // t_blk
    bw = t_blk * SSM_GROUP
    rows = nblk * nseq
    has_h0 = h0 is not None
    su = p[:, COL_SU:COL_SU + SSM_WIDTH].reshape(nseq, nblk, t_blk, SSM_GROUPS, SSM_GROUP)
    x = su.transpose(1, 0, 3, 2, 4).reshape(rows, SSM_GROUPS * bw)
    kern = functools.partial(_ssm_kernel, nseq=nseq, nblk=nblk, has_h0=has_h0)
    wspec = lambda a, b: pl.BlockSpec((None, None, a, b), lambda g: (layer, g, 0, 0))
    in_specs = [pl.BlockSpec((rows, bw), lambda g: (0, g)),
                wspec(bw, bw), wspec(bw, 4 * SSM_STATE), wspec(bw, 4 * SSM_STATE), wspec(2, 2 * SSM_STATE),
                wspec(1, bw)]
    args = [x, sw['w'], sw['bst'], sw['cst'], sw['lamt'], sw['d']]
    out_specs = [pl.BlockSpec((rows, bw), lambda g: (0, g))]
    out_shape = [jax.ShapeDtypeStruct((rows, SSM_GROUPS * bw), BF16)]
    if has_h0:
        in_specs.append(pl.BlockSpec((None, None, 2, nseq, 2 * SSM_STATE), lambda g: (layer, g, 0, 0, 0)))
        args.append(h0)
    else:
        out_specs.append(pl.BlockSpec((None, 2, nseq, 2 * SSM_STATE), lambda g: (g, 0, 0, 0)))
        out_shape.append(jax.ShapeDtypeStruct((SSM_GROUPS, 2, nseq, 2 * SSM_STATE), F32))
    outs = pl.pallas_call(
        kern,
        grid=(SSM_GROUPS,),
        in_specs=in_specs,
        out_specs=out_specs,
        out_shape=out_shape,
        scratch_shapes=[pltpu.VMEM((rows, 4 * SSM_STATE), F32), pltpu.VMEM((rows, 4 * SSM_STATE), F32)],
        compiler_params=_cparams(("parallel",)),
        name="ssm",
    )(*args)
    z = outs[0].reshape(nblk, nseq, SSM_GROUPS, t_blk, SSM_GROUP).transpose(1, 0, 3, 2, 4).reshape(n, SSM_WIDTH)
    return (z,) + tuple(outs[1:])


def _hgrn_consts():
    c = HG_CHUNK
    i = np.arange(c)[:, None]
    r = np.arange(c)[None, :]
    m_all = np.zeros((2, HG_LEVELS, c, c), np.float32)
    lev_of = np.full((2, c, c), -1, np.int32)
    for lev in range(HG_LEVELS):
        n = 2 << lev
        half = n // 2
        mid = (i // n) * n + half
        upper = (i % n) >= half
        m_all[0, lev] = np.where(upper, (r >= mid) & (r <= i), (r > i) & (r < mid))
        m_all[1, lev] = np.where(upper, (r >= mid) & (r < i), (r >= i) & (r < mid))
        same = (i // n) == (r // n)
        r_upper = (r % n) >= half
        lev_of[0][same & upper & ~r_upper] = lev
        lev_of[1][same & ~upper & r_upper] = lev
    cum = np.stack([r <= i, r >= i]).astype(np.float32)
    m_small = np.concatenate([cum] + [m_all[:, lev] for lev in HG_MXU_LEVELS], axis=1)
    return (jnp.asarray(np.concatenate([m_small, m_small], axis=-1), BF16), jnp.asarray(lev_of))


def _hgrn_chunks(chains, states, hq_ref, hi_ref, m_ref, lev_ref):
    c = HG_CHUNK
    nb = c // SUBLANES
    w = []
    for d, r0, f_ref, lb, cum_ref in chains:
        q = _silu(hq_ref[pl.ds(r0, c), :])
        fg = lb + (1.0 - lb) * _sigmoid(f_ref[pl.ds(r0, c), :])
        k = 1.0 - fg
        log2f = jnp.log(fg) * (1.0 / math.log(2.0))
        v = hi_ref[pl.ds(r0, c), :]
        hi = log2f.astype(BF16)
        lo = (log2f - hi.astype(F32)).astype(BF16)
        sums = _dot(m_ref[d], jnp.concatenate([hi, lo], axis=0))
        w.append(dict(d=d, q=q, fg=fg, k=k, v=v, vb=v.astype(BF16), sums=sums, cum=sums[0:c],
                      cum_ref=cum_ref, att=[None] * nb))
    for x in w:
        x['cum_ref'][...] = x['cum']

    def select_into(x, rows, lev, a):
        att = x['att']
        for j in range(rows.stop // SUBLANES - rows.start // SUBLANES):
            i = rows.start // SUBLANES + j
            blk = a[j * SUBLANES:(j + 1) * SUBLANES]
            mask = lev_ref[x['d'], i * SUBLANES:(i + 1) * SUBLANES, :] == lev
            att[i] = jnp.where(mask, blk, 0.0 if att[i] is None else att[i])

    full = slice(0, c)
    prods = [_dot_nt((x['q'] * x['fg']).astype(BF16), x['k'].astype(BF16)) for x in w]
    for x, a in zip(w, prods):
        select_into(x, full, 0, a)
    for n_lev, lev in enumerate(HG_MXU_LEVELS):
        prods = []
        for x in w:
            g = jnp.exp2(x['sums'][(n_lev + 1) * c:(n_lev + 2) * c])
            prods.append(_dot_nt((x['q'] * g).astype(BF16), (x['k'] * g).astype(BF16)))
        for x, a in zip(w, prods):
            select_into(x, full, lev, a)
    for lev in range(HG_MXU_LEVELS[-1] + 1, HG_LEVELS):
        half = 1 << lev
        n = 2 * half
        prods = []
        for x in w:
            d, q, k, cum = x['d'], x['q'], x['k'], x['cum']
            q_parts, k_parts, q_rows = [], [], []
            for j in range(c // n):
                lower = slice(j * n, j * n + half)
                upper = slice(j * n + half, (j + 1) * n)
                qr, kr = (upper, lower) if d == 0 else (lower, upper)
                ref_row = j * n + half - 1 if d == 0 else j * n + half
                ref = jnp.broadcast_to(x['cum_ref'][ref_row:ref_row + 1, :], (half, HG_KDIM))
                q_parts.append(q[qr] * jnp.exp2(cum[qr] - ref))
                k_scaled = k[kr] * jnp.exp2(ref - cum[kr])
                k_parts += [k_scaled, k[qr]] if d == 0 else [k[qr], k_scaled]
                q_rows.append(qr)
            prods.append((q_rows, _dot_nt(jnp.concatenate(q_parts, axis=0).astype(BF16),
                                          jnp.concatenate(k_parts, axis=0).astype(BF16))))
        for x, (q_rows, a) in zip(w, prods):
            for j, qr in enumerate(q_rows):
                select_into(x, qr, lev, a[j * half:(j + 1) * half])
    for x in w:
        d, q, k, cum = x['d'], x['q'], x['k'], x['cum']
        last = c - 1 if d == 0 else 0
        tot_row = cum[last:last + 1, :]
        x['lhs'] = jnp.concatenate([jnp.concatenate(x['att'], axis=0).astype(BF16),
                                    (q * jnp.exp2(cum)).astype(BF16)], axis=1)
        x['kg'] = (k * jnp.exp2(tot_row - cum)).astype(BF16)
        x['dv'] = jnp.sum(q * k, axis=-1, keepdims=True) * x['v']
        x['tot_col'] = jnp.broadcast_to(jnp.exp2(tot_row), (c, HG_KDIM)).T
    states = list(states)
    outs = []
    for x in w:
        st = states[x['d']]
        outs.append(x['dv'] + _dot(x['lhs'], jnp.concatenate([x['vb'], st.astype(BF16)], axis=0)))
        states[x['d']] = st * x['tot_col'] + _dot_tn(x['kg'], x['vb'])
    return outs, states


def _hgrn_kernel(*refs, seq_len, has_s0):
    if has_s0:
        (hq_ref, ff_ref, fb_ref, hi_ref, hgt_ref, lb_ref, ng_ref, m_ref, lev_ref, s0_ref,
         o_ref, of_scr, ob_scr, cum_scr) = refs
    else:
        (hq_ref, ff_ref, fb_ref, hi_ref, hgt_ref, lb_ref, ng_ref, m_ref, lev_ref,
         o_ref, sfin_ref, of_scr, ob_scr, cum_scr) = refs
    c = HG_CHUNK
    nch = seq_len // c
    lb_f = lb_ref[0:1, :]
    lb_b = lb_ref[1:2, :]

    def step(ci, carry):
        chains, rows = [], []
        for u in range(HG_UNROLL):
            cf = ci * HG_UNROLL + u
            rf = pl.multiple_of(cf * c, c)
            rb = pl.multiple_of((nch - 1 - cf) * c, c)
            chains += [(0, rf, ff_ref, lb_f, cum_scr.at[2 * u]), (1, rb, fb_ref, lb_b, cum_scr.at[2 * u + 1])]
            rows += [(of_scr, rf), (ob_scr, rb)]
        outs, states = _hgrn_chunks(chains, carry, hq_ref, hi_ref, m_ref, lev_ref)
        for (scr, r0), o in zip(rows, outs):
            scr[pl.ds(r0, c), :] = o
        return tuple(states)

    if has_s0:
        init = (s0_ref[0], s0_ref[1])
    else:
        init = (jnp.zeros((HG_KDIM, HG_VDIM), F32), jnp.zeros((HG_KDIM, HG_VDIM), F32))
    if nch == HG_UNROLL:
        st_f, st_b = step(0, init)
    else:
        st_f, st_b = lax.fori_loop(0, nch // HG_UNROLL, step, init)
    if not has_s0:
        sfin_ref[0] = st_f
        sfin_ref[1] = st_b

    o = of_scr[...] + ob_scr[...]
    ms = jnp.mean(o * o, axis=-1, keepdims=True)
    y = o * lax.rsqrt(ms + EPS) * ng_ref[...]
    o_ref[...] = (y * _silu(hgt_ref[...])).astype(BF16)


def _hgrn(p, lb, norm_g, m_all, lev_of, layer, seq_len, s0=None):
    n = p.shape[0]
    nseq = n // seq_len
    has_s0 = s0 is not None
    kern = functools.partial(_hgrn_kernel, seq_len=seq_len, has_s0=has_s0)
    base = COL_HG // LANES

    def col(k):
        return pl.BlockSpec((seq_len, LANES), lambda b, h, k=k: (b, base + k * HG_HEADS + h))

    c = HG_CHUNK
    in_specs = [col(0), col(1), col(2), col(3), col(4),
                pl.BlockSpec((None, None, 2, HG_KDIM), lambda b, h: (layer, h, 0, 0)),
                pl.BlockSpec((None, 1, HG_VDIM), lambda b, h: (layer, 0, 0)),
                pl.BlockSpec((2, (len(HG_MXU_LEVELS) + 1) * c, 2 * c), lambda b, h: (0, 0, 0)),
                pl.BlockSpec((2, c, c), lambda b, h: (0, 0, 0))]
    args = [p, p, p, p, p, lb, norm_g, m_all, lev_of]
    out_specs = [pl.BlockSpec((seq_len, HG_VDIM), lambda b, h: (b, h))]
    out_shape = [jax.ShapeDtypeStruct((n, HG_VW), BF16)]
    if has_s0:
        in_specs.append(pl.BlockSpec((None, None, 2, None, HG_KDIM, HG_VDIM), lambda b, h: (b, layer, 0, h, 0, 0)))
        args.append(s0)
    else:
        out_specs.append(pl.BlockSpec((None, 2, None, HG_KDIM, HG_VDIM), lambda b, h: (b, 0, h, 0, 0)))
        out_shape.append(jax.ShapeDtypeStruct((nseq, 2, HG_HEADS, HG_KDIM, HG_VDIM), F32))
    return pl.pallas_call(
        kern,
        grid=(nseq, HG_HEADS),
        in_specs=in_specs,
        out_specs=out_specs,
        out_shape=out_shape,
        scratch_shapes=[pltpu.VMEM((seq_len, HG_VDIM), F32), pltpu.VMEM((seq_len, HG_VDIM), F32),
                        pltpu.VMEM((2 * HG_UNROLL, HG_CHUNK, HG_KDIM), F32)],
        compiler_params=_cparams(("parallel", "parallel")),
        name="hgrn",
    )(*args)


def _router(logits):
    lane = lax.broadcasted_iota(jnp.int32, logits.shape, 1)
    big = jnp.int32(LANES)
    is_g = lane < N_GROUPS
    gl = jnp.where(is_g, logits, -jnp.inf)
    gmax = jnp.max(gl, axis=-1, keepdims=True)
    gsum = jnp.sum(jnp.exp(gl - gmax), axis=-1, keepdims=True)
    g_idx = jnp.min(jnp.where(is_g & (gl == gmax), lane, big), axis=-1, keepdims=True)
    g_prob = 1.0 / gsum
    e_lane = lane - ROUTER_OFF
    sel = (e_lane >= 0) & (e_lane < N_EXPERTS) & ((e_lane >> 2) == g_idx)
    el = jnp.where(sel, logits, -jnp.inf)
    emax = jnp.max(el, axis=-1, keepdims=True)
    eexp = jnp.exp(el - emax)
    ep = eexp / jnp.sum(eexp, axis=-1, keepdims=True)
    p1 = jnp.max(ep, axis=-1, keepdims=True)
    i1 = jnp.min(jnp.where(sel & (ep == p1), lane, big), axis=-1, keepdims=True)
    rest = sel & (lane != i1)
    ep2 = jnp.where(rest, ep, -1.0)
    p2 = jnp.max(ep2, axis=-1, keepdims=True)
    i2 = jnp.min(jnp.where(rest & (ep2 == p2), lane, big), axis=-1, keepdims=True)
    den = p1 + p2
    w1 = g_prob * (p1 / den)
    w2 = g_prob * (p2 / den)
    return jnp.where(lane == i1, w1, jnp.where(lane == i2, w2, 0.0))


def _merge_kernel(x_ref, att_ref, z_ref, hg_ref, mg_ref, mod_ref, n2_ref,
                  wao_ref, wga_ref, wgb_ref, who_ref, wout_ref, wr_ref, br_ref,
                  xm_ref, h2_ref, comb_ref):
    y_att = _dot(att_ref[...], wao_ref[...])
    z = z_ref[...]
    y_ssm = _dot(z, wga_ref[...]) * _sigmoid(_dot(z, wgb_ref[...]))
    y_hg = _dot(hg_ref[...], who_ref[...])
    merged = (_sigmoid(mg_ref[:, 0:D_MODEL]) * y_att
              + _sigmoid(mg_ref[:, D_MODEL:2 * D_MODEL]) * y_ssm
              + _sigmoid(mg_ref[:, 2 * D_MODEL:3 * D_MODEL]) * y_hg)
    xm = x_ref[...] + mod_ref[2:3, :] * _dot(merged.astype(BF16), wout_ref[...])
    xm_ref[...] = xm
    ms = jnp.mean(xm * xm, axis=-1, keepdims=True)
    h2 = xm * lax.rsqrt(ms + EPS) * n2_ref[...] * (1.0 + mod_ref[4:5, :]) + mod_ref[3:4, :]
    h2_ref[...] = h2.astype(BF16)
    logits = jnp.dot(h2, wr_ref[...], preferred_element_type=F32, precision=lax.Precision.HIGHEST) + br_ref[...]
    comb_ref[...] = _router(logits)


def _merge(x, att, z, hg, p, mod, n2g, w, layer, seq_len, cond_row0):
    n = x.shape[0]
    tm = 256
    rows_per_cond = seq_len if cond_row0 else n
    mod_idx = (lambda i: (layer, cond_row0 + (i * tm) // rows_per_cond, 0, 0))
    row = lambda width: pl.BlockSpec((tm, width), lambda i: (i, 0))
    wspec = lambda a, b: pl.BlockSpec((None, a, b), lambda i: (layer, 0, 0))
    return pl.pallas_call(
        _merge_kernel,
        grid=(n // tm,),
        in_specs=[
            row(D_MODEL), row(Q_W), row(SSM_WIDTH), row(HG_VW),
            pl.BlockSpec((tm, MG_W), lambda i: (i, COL_MG // MG_W)),
            pl.BlockSpec((None, None, 6, D_MODEL), mod_idx),
            wspec(1, D_MODEL),
            wspec(Q_W, D_MODEL), wspec(SSM_WIDTH, D_MODEL), wspec(SSM_WIDTH, D_MODEL),
            wspec(HG_VW, D_MODEL), wspec(D_MODEL, D_MODEL), wspec(D_MODEL, LANES), wspec(1, LANES),
        ],
        out_specs=[row(D_MODEL), row(D_MODEL), row(LANES)],
        out_shape=[
            jax.ShapeDtypeStruct((n, D_MODEL), F32),
            jax.ShapeDtypeStruct((n, D_MODEL), BF16),
            jax.ShapeDtypeStruct((n, LANES), F32),
        ],
        compiler_params=_cparams(("parallel",)),
        name="merge",
    )(x, att, z, hg, p, mod, n2g, w['w_attn_o'], w['w_glu_a'], w['w_glu_b'], w['w_hg_o'], w['w_out'],
      w['w_route'], w['b_route'])


def _moe_kernel(h_ref, comb_ref, xm_ref, mod_ref, wg_ref, wu_ref, wd_ref, o_ref, acc_scr):
    e = pl.program_id(1)

    @pl.when(e == 0)
    def _():
        acc_scr[...] = jnp.zeros_like(acc_scr)

    comb = comb_ref[...]
    lane = lax.broadcasted_iota(jnp.int32, comb.shape, 1)
    cw = jnp.sum(jnp.where(lane == e + ROUTER_OFF, comb, 0.0), axis=-1, keepdims=True)
    h = h_ref[...]
    act = _silu(_dot(h, wg_ref[...])) * _dot(h, wu_ref[...]) * cw
    acc_scr[...] += _dot(act.astype(BF16), wd_ref[...])

    @pl.when(e == N_EXPERTS - 1)
    def _():
        o_ref[...] = xm_ref[...] + mod_ref[5:6, :] * acc_scr[...]


def _moe(h2, comb, xm, mod, w, layer, seq_len, cond_row0):
    n = h2.shape[0]
    tm = 512
    rows_per_cond = seq_len if cond_row0 else n
    mod_idx = (lambda i, e: (layer, cond_row0 + (i * tm) // rows_per_cond, 0, 0))
    return pl.pallas_call(
        _moe_kernel,
        grid=(n // tm, N_EXPERTS),
        in_specs=[
            pl.BlockSpec((tm, D_MODEL), lambda i, e: (i, 0)),
            pl.BlockSpec((tm, LANES), lambda i, e: (i, 0)),
            pl.BlockSpec((tm, D_MODEL), lambda i, e: (i, 0)),
            pl.BlockSpec((None, None, 6, D_MODEL), mod_idx),
            pl.BlockSpec((None, None, D_MODEL, EXPERT_FF), lambda i, e: (layer, e, 0, 0)),
            pl.BlockSpec((None, None, D_MODEL, EXPERT_FF), lambda i, e: (layer, e, 0, 0)),
            pl.BlockSpec((None, None, EXPERT_FF, D_MODEL), lambda i, e: (layer, e, 0, 0)),
        ],
        out_specs=pl.BlockSpec((tm, D_MODEL), lambda i, e: (i, 0)),
        out_shape=jax.ShapeDtypeStruct((n, D_MODEL), F32),
        scratch_shapes=[pltpu.VMEM((tm, D_MODEL), F32)],
        compiler_params=_cparams(("parallel", "arbitrary")),
        name="moe",
    )(h2, comb, xm, mod, w['w_e_gate'], w['w_e_up'], w['w_e_down'])


def _layer(x, layer, w, seq_len, cond_row0, cache):
    p = _in_proj(x, w['mod'], w['norm1_g'], w['w_in'], layer, seq_len, cond_row0)
    nseq = x.shape[0] // seq_len
    if cache is None:
        att, k_new, v_new = _attn_ctx(w['attn_sink'], p, w['q_norm_g'], w['k_norm_g'], layer)
        z, ssm_fin = _ssm(p, w['ssm'], layer, nseq, seq_len)
        hg, hg_fin = _hgrn(p, w['hg_lb'], w['hg_norm_g'], w['hg_m'], w['hg_lev'], layer, seq_len)
        ctx = (k_new, v_new, ssm_fin, hg_fin)
    else:
        cache_k, cache_v, h0, s0 = cache
        att = _attn_lat(w['attn_sink'], p, cache_k, cache_v, w['q_norm_g'], w['k_norm_g'],
                        w['rope_cos'], w['rope_sin'], layer)
        (z,) = _ssm(p, w['ssm'], layer, nseq, seq_len, h0=h0)
        (hg,) = _hgrn(p, w['hg_lb'], w['hg_norm_g'], w['hg_m'], w['hg_lev'], layer, seq_len, s0=s0)
        ctx = None
    xm, h2, comb = _merge(x, att, z, hg, p, w['mod'], w['norm2_g'], w, layer, seq_len, cond_row0)
    x = _moe(h2, comb, xm, w['mod'], w, layer, seq_len, cond_row0)
    return x, ctx


def kernel(x_prompt, x_sample, cache_k, cache_v, state_ssm_re, state_ssm_im, state_hgrn, c, c_ctx, w_mod, b_mod, norm1_g, norm2_g, w_in, q_norm_g, k_norm_g, attn_sink, w_attn_o, ssm_a_re, ssm_a_im, ssm_log_dt, ssm_b_re, ssm_b_im, ssm_c_re, ssm_c_im, ssm_d, w_glu_a, w_glu_b, hg_lb, hg_norm_g, w_hg_o, w_out, w_group, b_group, w_router, b_router, w_e_gate, w_e_up, w_e_down):
    w = {}
    w['mod'] = _modulation(c, c_ctx, w_mod, b_mod)
    w['norm1_g'] = norm1_g.reshape(DEPTH, 1, D_MODEL)
    w['norm2_g'] = norm2_g.reshape(DEPTH, 1, D_MODEL)
    w['w_in'] = jnp.concatenate(
        [w_in[:, :, 3840:], w_in[:, :, 1280:3840], w_in[:, :, 0:768], w_in[:, :, 768:1280]], axis=-1).astype(BF16)
    w['q_norm_g'] = jnp.tile(q_norm_g, (1, LANES // HEAD_DIM)).reshape(DEPTH, 1, LANES)
    w['k_norm_g'] = jnp.tile(k_norm_g, (1, LANES // HEAD_DIM)).reshape(DEPTH, 1, LANES)
    w['attn_sink'] = attn_sink
    w['rope_cos'], w['rope_sin'] = _rope_tables()
    w['ssm'] = _ssm_prep(ssm_a_re, ssm_a_im, ssm_log_dt, ssm_b_re, ssm_b_im, ssm_c_re, ssm_c_im, ssm_d)
    w['hg_lb'] = _lower_bounds(hg_lb)
    w['hg_norm_g'] = hg_norm_g.reshape(DEPTH, 1, HG_VDIM)
    w['hg_m'], w['hg_lev'] = _hgrn_consts()
    for name, val in (('w_attn_o', w_attn_o), ('w_glu_a', w_glu_a), ('w_glu_b', w_glu_b), ('w_hg_o', w_hg_o),
                      ('w_out', w_out), ('w_e_gate', w_e_gate), ('w_e_up', w_e_up), ('w_e_down', w_e_down)):
        w[name] = val.astype(BF16)
    pad = LANES - N_GROUPS - N_EXPERTS
    w['w_route'] = jnp.concatenate([w_group, w_router, jnp.zeros((DEPTH, D_MODEL, pad), F32)], axis=-1)
    w['b_route'] = jnp.concatenate([b_group, b_router, jnp.zeros((DEPTH, pad), F32)], axis=-1).reshape(DEPTH, 1, LANES)

    ck = cache_k.reshape(DEC_BATCH, DEPTH, PAST_LEN, KV_W)
    cv = cache_v.reshape(DEC_BATCH, DEPTH, PAST_LEN, KV_W)

    h0 = jnp.stack([state_ssm_re, state_ssm_im]).transpose(2, 4, 0, 1, 3, 5).reshape(
        DEPTH, SSM_GROUPS, 2, DEC_BATCH, 2 * SSM_STATE)

    xp = x_prompt.reshape(BATCH * SEQ, D_MODEL)
    xs = x_sample.reshape(DEC_BATCH * DEC_SEQ, D_MODEL)
    new_k, new_v, new_re, new_im, new_hg = [], [], [], [], []
    for l in range(DEPTH):
        xp, ctx = _layer(xp, l, w, SEQ, 0, None)
        new_k.append(ctx[0].reshape(BATCH, SEQ, N_KV_HEADS, HEAD_DIM))
        new_v.append(ctx[1].reshape(BATCH, SEQ, N_KV_HEADS, HEAD_DIM))
        fin = ctx[2].reshape(SSM_GROUPS, 2, BATCH, 2, SSM_STATE).transpose(1, 2, 3, 0, 4)
        new_re.append(fin[0])
        new_im.append(fin[1])
        new_hg.append(ctx[3])
        xs, _ = _layer(xs, l, w, DEC_SEQ, 1, (ck, cv, h0, state_hgrn))
    return (xp.reshape(BATCH, SEQ, D_MODEL), xs.reshape(DEC_BATCH, DEC_SEQ, D_MODEL),
            jnp.stack(new_k, axis=1), jnp.stack(new_v, axis=1), jnp.stack(new_re, axis=1),
            jnp.stack(new_im, axis=1), jnp.stack(new_hg, axis=1))
```

```python
import functools
import math

import numpy as np
import jax
import jax.numpy as jnp
from jax import lax
from jax.experimental import pallas as pl
from jax.experimental.pallas import tpu as pltpu

F32 = jnp.float32
BF16 = jnp.bfloat16

D_MODEL = 1024
BATCH = 32
SEQ = 256
DEPTH = 4
DEC_BATCH = 4
DEC_SEQ = 1024
PAST_LEN = 512
GRID_W = 64
N_HEADS = 8
N_KV_HEADS = 2
HEAD_DIM = 64
KV_GROUP = N_HEADS // N_KV_HEADS
WINDOW = 128
ATT_BLOCK = 128
ROPE_BASE = 10000.0
SSM_WIDTH = 512
SSM_GROUP = 16
SSM_GROUPS = SSM_WIDTH // SSM_GROUP
SSM_STATE = 64
HG_HEADS = 4
HG_KDIM = 128
HG_VDIM = 128
N_BRANCHES = 3
N_GROUPS = 4
EXPERTS_PER_GROUP = 4
N_EXPERTS = N_GROUPS * EXPERTS_PER_GROUP
EXPERT_FF = 256
EPS = 1e-6
Q_W = N_HEADS * HEAD_DIM
KV_W = N_KV_HEADS * HEAD_DIM
HG_KW = HG_HEADS * HG_KDIM
HG_VW = HG_HEADS * HG_VDIM

LANES = 128
SUBLANES = 8
VMEM_LIMIT = 56 * 1024 * 1024

MG_W = N_BRANCHES * D_MODEL
COL_MG = 0
COL_HG = COL_MG + MG_W
COL_AQ = COL_HG + 3 * HG_KW + 2 * HG_VW
COL_AK = COL_AQ + Q_W
COL_SU = COL_AK + 2 * KV_W
IN_WIDTH = COL_SU + SSM_WIDTH

SSM_T = 16
HG_CHUNK = 128
HG_LEVELS = 7
HG_UNROLL = 2
HG_MXU_LEVELS = (1, 2)
ROUTER_OFF = N_GROUPS
MASK_NEG = -1e30


def _cparams(sem):
    return pltpu.CompilerParams(dimension_semantics=sem, vmem_limit_bytes=VMEM_LIMIT)


def _sigmoid(x):
    return 1.0 / (1.0 + jnp.exp(-x))


def _silu(x):
    return x * _sigmoid(x)


def _gelu_tanh(x):
    return 0.5 * x * (1.0 + jnp.tanh(math.sqrt(2.0 / math.pi) * (x + 0.044715 * (x * x * x))))


def _dot(a, b):
    return jnp.dot(a, b, preferred_element_type=F32)


def _dot_nt(a, b):
    return lax.dot_general(a, b, (((1,), (1,)), ((), ())), preferred_element_type=F32)


def _dot_tn(a, b):
    return lax.dot_general(a, b, (((0,), (0,)), ((), ())), preferred_element_type=F32)


def _ssm_prep_kernel(are_ref, aim_ref, ldt_ref, btre_ref, btim_ref, cre_ref, cim_ref,
                     k_ref, bst_ref, cst_ref, lamt_ref):
    t_blk = SSM_T
    ns = SSM_STATE
    jf = lax.broadcasted_iota(jnp.int32, (2 * t_blk, ns), 0).astype(F32)
    bt_re = btre_ref[...]
    bt_im = btim_ref[...]
    c_re = cre_ref[...]
    c_im = cim_ref[...]
    for d in range(2):
        a_re = are_ref[d:d + 1, :]
        a_im = aim_ref[d:d + 1, :]
        dt = jnp.exp(ldt_ref[d:d + 1, :])
        mag = jnp.exp(jf * (a_re * dt))
        ang = jf * (a_im * dt)
        p_re = mag * jnp.cos(ang)
        p_im = mag * jnp.sin(ang)
        nr = p_re[1:2] - 1.0
        ni = p_im[1:2]
        den = a_re * a_re + a_im * a_im
        f_re = (nr * a_re + ni * a_im) / den
        f_im = (ni * a_re - nr * a_im) / den
        bb_re = f_re * bt_re - f_im * bt_im
        bb_im = f_re * bt_im + f_im * bt_re
        a_parts_re, a_parts_im = [], []
        for j in range(t_blk):
            a_parts_re.append(c_re * p_re[j:j + 1] - c_im * p_im[j:j + 1])
            a_parts_im.append(c_re * p_im[j:j + 1] + c_im * p_re[j:j + 1])
        hp = lax.Precision.HIGHEST
        k_ref[d] = (lax.dot_general(jnp.concatenate(a_parts_re, axis=0), bb_re, (((1,), (1,)), ((), ())),
                                    precision=hp, preferred_element_type=F32)
                    - lax.dot_general(jnp.concatenate(a_parts_im, axis=0), bb_im, (((1,), (1,)), ((), ())),
                                      precision=hp, preferred_element_type=F32))
        for t in range(t_blk):
            rows = slice(t * SSM_GROUP, (t + 1) * SSM_GROUP)
            e = t_blk - 1 - t if d == 0 else t
            bst_ref[rows, d * ns:(d + 1) * ns] = bb_re * p_re[e:e + 1] - bb_im * p_im[e:e + 1]
            bst_ref[rows, (2 + d) * ns:(3 + d) * ns] = bb_re * p_im[e:e + 1] + bb_im * p_re[e:e + 1]
            e = t + 1 if d == 0 else t_blk - t
            cst_ref[rows, d * ns:(d + 1) * ns] = c_re * p_re[e:e + 1] - c_im * p_im[e:e + 1]
            cst_ref[rows, (2 + d) * ns:(3 + d) * ns] = -(c_re * p_im[e:e + 1] + c_im * p_re[e:e + 1])
        lamt_ref[0:1, d * ns:(d + 1) * ns] = p_re[t_blk:t_blk + 1]
        lamt_ref[1:2, d * ns:(d + 1) * ns] = p_im[t_blk:t_blk + 1]


def _ssm_prep(ssm_a_re, ssm_a_im, ssm_log_dt, ssm_b_re, ssm_b_im, ssm_c_re, ssm_c_im, ssm_d):
    t_blk = SSM_T
    bw = t_blk * SSM_GROUP
    lgd = lambda a: a.transpose(0, 2, 1, 3)
    ldt = jnp.broadcast_to(ssm_log_dt[..., None], ssm_a_re.shape)
    vec = pl.BlockSpec((None, None, 2, SSM_STATE), lambda l, g: (l, g, 0, 0))
    mat = pl.BlockSpec((None, None, SSM_GROUP, SSM_STATE), lambda l, g: (l, g, 0, 0))
    k, bst, cst, lamt = pl.pallas_call(
        _ssm_prep_kernel,
        grid=(DEPTH, SSM_GROUPS),
        in_specs=[vec, vec, vec, mat, mat, mat, mat],
        out_specs=[
            pl.BlockSpec((None, None, 2, bw, SSM_GROUP), lambda l, g: (l, g, 0, 0, 0)),
            pl.BlockSpec((None, None, bw, 4 * SSM_STATE), lambda l, g: (l, g, 0, 0)),
            pl.BlockSpec((None, None, bw, 4 * SSM_STATE), lambda l, g: (l, g, 0, 0)),
            pl.BlockSpec((None, None, 2, 2 * SSM_STATE), lambda l, g: (l, g, 0, 0)),
        ],
        out_shape=[
            jax.ShapeDtypeStruct((DEPTH, SSM_GROUPS, 2, bw, SSM_GROUP), F32),
            jax.ShapeDtypeStruct((DEPTH, SSM_GROUPS, bw, 4 * SSM_STATE), F32),
            jax.ShapeDtypeStruct((DEPTH, SSM_GROUPS, bw, 4 * SSM_STATE), F32),
            jax.ShapeDtypeStruct((DEPTH, SSM_GROUPS, 2, 2 * SSM_STATE), F32),
        ],
        compiler_params=_cparams(("parallel", "parallel")),
        name="ssm_prep",
    )(lgd(ssm_a_re), lgd(ssm_a_im), lgd(ldt), jnp.swapaxes(ssm_b_re, -1, -2), jnp.swapaxes(ssm_b_im, -1, -2),
      ssm_c_re, ssm_c_im)
    k = k.reshape(DEPTH, SSM_GROUPS, 2, t_blk, SSM_GROUP, SSM_GROUP)
    t_in = np.arange(t_blk)[:, None]
    t_out = np.arange(t_blk)[None, :]
    w = 0.0
    for d, lag in ((0, t_out - t_in), (1, t_in - t_out)):
        blk = k[:, :, d][:, :, np.clip(lag, 0, t_blk - 1)]
        blk = blk * jnp.asarray(lag >= 0, F32)[:, :, None, None]
        w = w + blk.transpose(0, 1, 2, 5, 3, 4).reshape(DEPTH, SSM_GROUPS, bw, bw)
    d_row = jnp.tile(ssm_d, (1, 1, t_blk)).reshape(DEPTH, SSM_GROUPS, 1, bw)
    return dict(w=w.astype(BF16), bst=bst.astype(BF16), cst=cst.astype(BF16), lamt=lamt, d=d_row)


def _lb_kernel(x_ref, o_ref):
    x = x_ref[...]
    m = jnp.max(x, axis=0, keepdims=True)
    e = jnp.exp(x - m)
    s = e / jnp.sum(e, axis=0, keepdims=True)
    run = jnp.zeros_like(s[0:1])
    o_ref[0:1, :] = run
    for l in range(1, DEPTH):
        run = run + s[l:l + 1]
        o_ref[l:l + 1, :] = run


def _lower_bounds(hg_lb):
    w = 2 * HG_KW
    out = pl.pallas_call(
        _lb_kernel,
        out_shape=jax.ShapeDtypeStruct((DEPTH, w), F32),
        name="hgrn_lower_bounds",
    )(hg_lb.reshape(DEPTH, w))
    return out.reshape(DEPTH, 2, HG_HEADS, HG_KDIM).transpose(0, 2, 1, 3)


def _mod_kernel(c_ref, w_ref, b_ref, o_ref):
    c = c_ref[...]
    a = _silu(c).astype(BF16)
    o_ref[...] = _dot(a, w_ref[...].astype(BF16)) + b_ref[...]


def _modulation(c, c_ctx, w_mod, b_mod):
    rows = SUBLANES
    cond = jnp.concatenate([c_ctx[None, :], c, jnp.zeros((rows - 1 - DEC_BATCH, D_MODEL), F32)], axis=0)
    tn = D_MODEL
    out = pl.pallas_call(
        _mod_kernel,
        grid=(DEPTH, 6),
        in_specs=[
            pl.BlockSpec((rows, D_MODEL), lambda l, j: (0, 0)),
            pl.BlockSpec((None, D_MODEL, tn), lambda l, j: (l, 0, j)),
            pl.BlockSpec((None, 1, tn), lambda l, j: (l, 0, j)),
        ],
        out_specs=pl.BlockSpec((None, rows, tn), lambda l, j: (l, 0, j)),
        out_shape=jax.ShapeDtypeStruct((DEPTH, rows, 6 * D_MODEL), F32),
        compiler_params=_cparams(("parallel", "parallel")),
        name="modulation",
    )(cond, w_mod, b_mod.reshape(DEPTH, 1, 6 * D_MODEL))
    return out.reshape(DEPTH, rows, 6, D_MODEL)


def _inproj_kernel(x_ref, mod_ref, g_ref, w_ref, o_ref, h_scr):
    @pl.when(pl.program_id(1) == 0)
    def _():
        x = x_ref[...]
        ms = jnp.mean(x * x, axis=-1, keepdims=True)
        y = x * lax.rsqrt(ms + EPS) * g_ref[...]
        h = y * (1.0 + mod_ref[1:2, :]) + mod_ref[0:1, :]
        h_scr[...] = h.astype(BF16)

    o_ref[...] = _dot(h_scr[...], w_ref[...])


def _in_proj(x, mod, norm_g, w_in, layer, seq_len, cond_row0):
    n = x.shape[0]
    tm = 1024
    tn = 2304
    rows_per_cond = seq_len if cond_row0 else n
    mod_idx = (lambda i, j: (layer, cond_row0 + (i * tm) // rows_per_cond, 0, 0))
    return pl.pallas_call(
        _inproj_kernel,
        grid=(n // tm, IN_WIDTH // tn),
        in_specs=[
            pl.BlockSpec((tm, D_MODEL), lambda i, j: (i, 0)),
            pl.BlockSpec((None, None, 6, D_MODEL), mod_idx),
            pl.BlockSpec((None, 1, D_MODEL), lambda i, j: (layer, 0, 0)),
            pl.BlockSpec((None, D_MODEL, tn), lambda i, j: (layer, 0, j)),
        ],
        out_specs=pl.BlockSpec((tm, tn), lambda i, j: (i, j)),
        out_shape=jax.ShapeDtypeStruct((n, IN_WIDTH), F32),
        scratch_shapes=[pltpu.VMEM((tm, D_MODEL), BF16)],
        compiler_params=_cparams(("parallel", "arbitrary")),
        name="in_proj",
    )(x, mod, norm_g, w_in)


def _pair_norm(x, g):
    xx = x * x
    s_a = jnp.sum(xx[:, :HEAD_DIM], axis=-1, keepdims=True)
    s_b = jnp.sum(xx[:, HEAD_DIM:], axis=-1, keepdims=True)
    lane = lax.broadcasted_iota(jnp.int32, x.shape, 1)
    ms = jnp.where(lane < HEAD_DIM, s_a, s_b) * (1.0 / HEAD_DIM)
    return x * lax.rsqrt(ms + EPS) * g


def _rope(x, cos, sin_signed):
    lane = lax.broadcasted_iota(jnp.int32, x.shape, 1)
    first = (lane & 31) < 16
    partner = jnp.where(first, pltpu.roll(x, LANES - 16, 1), pltpu.roll(x, 16, 1))
    return x * cos + partner * sin_signed


def _sink_column(sink_ref, layer, kh, t):
    row = lax.broadcasted_iota(jnp.int32, (KV_GROUP * t, 1), 0)
    col = jnp.full((KV_GROUP * t, 1), sink_ref[layer, kh * KV_GROUP], F32)
    for g in range(1, KV_GROUP):
        col = jnp.where(row >= g * t, sink_ref[layer, kh * KV_GROUP + g], col)
    return col


def _sink_softmax_av(scores, values, sink_col):
    m = sink_col
    for s in scores:
        m = jnp.maximum(m, jnp.max(s, axis=-1, keepdims=True))
    den = jnp.exp(sink_col - m)
    o = None
    for s, v in zip(scores, values):
        p = jnp.exp(s - m)
        den = den + jnp.sum(p, axis=-1, keepdims=True)
        pv = _dot(p.astype(BF16), v.astype(BF16))
        o = pv if o is None else o + pv
    return o / den


def _attn_ctx_kernel(sink_ref, q_ref, kv_ref, gq_ref, gk_ref, att_ref, k_ref, v_ref, *, layer):
    t = q_ref.shape[0]
    kv = kv_ref[...]
    k = _pair_norm(kv[:, :KV_W], gk_ref[...])
    v = kv[:, KV_W:]
    k_ref[...] = k
    v_ref[...] = v
    q = q_ref[...]
    scale = 1.0 / math.sqrt(HEAD_DIM)
    outs = []
    for kh in range(N_KV_HEADS):
        heads = []
        for j in range(KV_GROUP // 2):
            c0 = (kh * (KV_GROUP // 2) + j) * LANES
            qn = _pair_norm(q[:, c0:c0 + LANES], gq_ref[...])
            heads += [qn[:, :HEAD_DIM], qn[:, HEAD_DIM:]]
        qs = jnp.concatenate(heads, axis=0).astype(BF16)
        kk = k[:, kh * HEAD_DIM:(kh + 1) * HEAD_DIM]
        vv = v[:, kh * HEAD_DIM:(kh + 1) * HEAD_DIM]
        s = _dot_nt(qs, kk.astype(BF16)) * scale
        sk = _sink_column(sink_ref, layer, kh, t)
        o = _sink_softmax_av([s], [vv], sk)
        outs += [o[g * t:(g + 1) * t] for g in range(KV_GROUP)]
    att_ref[...] = jnp.concatenate(outs, axis=-1).astype(BF16)


def _attn_ctx(sink, p, gq, gk, layer):
    n = p.shape[0]
    t = SEQ
    kern = functools.partial(_attn_ctx_kernel, layer=layer)
    return pl.pallas_call(
        kern,
        grid=(n // t,),
        in_specs=[
            pl.BlockSpec(memory_space=pltpu.SMEM),
            pl.BlockSpec((t, Q_W), lambda b: (b, COL_AQ // Q_W)),
            pl.BlockSpec((t, 2 * KV_W), lambda b: (b, COL_AK // (2 * KV_W))),
            pl.BlockSpec((None, 1, LANES), lambda b: (layer, 0, 0)),
            pl.BlockSpec((None, 1, LANES), lambda b: (layer, 0, 0)),
        ],
        out_specs=[
            pl.BlockSpec((t, Q_W), lambda b: (b, 0)),
            pl.BlockSpec((t, KV_W), lambda b: (b, 0)),
            pl.BlockSpec((t, KV_W), lambda b: (b, 0)),
        ],
        out_shape=[
            jax.ShapeDtypeStruct((n, Q_W), BF16),
            jax.ShapeDtypeStruct((n, KV_W), F32),
            jax.ShapeDtypeStruct((n, KV_W), F32),
        ],
        compiler_params=_cparams(("parallel",)),
        name="attn_ctx",
    )(sink, p, p, gq, gk)


def _attn_lat_kernel(sink_ref, q_ref, kv_ref, kc_ref, vc_ref, gq_ref, gk_ref, cos_ref, sin_ref, att_ref, *, layer):
    t = ATT_BLOCK
    span = ATT_BLOCK + 2 * WINDOW
    start = pl.program_id(1) * ATT_BLOCK
    ks = pl.multiple_of(jnp.clip(start - WINDOW, 0, DEC_SEQ - span), ATT_BLOCK)
    q0 = pl.multiple_of(start, ATT_BLOCK)
    kvw = kv_ref[pl.ds(ks, span), :]
    k = _rope(_pair_norm(kvw[:, :KV_W], gk_ref[...]), cos_ref[pl.ds(ks, span), :], sin_ref[pl.ds(ks, span), :])
    v = kvw[:, KV_W:]
    cos_q = cos_ref[pl.ds(q0, t), :]
    sin_q = sin_ref[pl.ds(q0, t), :]
    qpos = start + (lax.broadcasted_iota(jnp.int32, (KV_GROUP * t, span), 0) & (t - 1))
    kpos = ks + lax.broadcasted_iota(jnp.int32, (KV_GROUP * t, span), 1)
    band = jnp.abs(qpos - kpos) <= WINDOW
    q = q_ref[...]
    kc = kc_ref[...]
    vc = vc_ref[...]
    scale = 1.0 / math.sqrt(HEAD_DIM)
    outs = []
    for kh in range(N_KV_HEADS):
        heads = []
        for j in range(KV_GROUP // 2):
            c0 = (kh * (KV_GROUP // 2) + j) * LANES
            qn = _rope(_pair_norm(q[:, c0:c0 + LANES], gq_ref[...]), cos_q, sin_q)
            heads += [qn[:, :HEAD_DIM], qn[:, HEAD_DIM:]]
        qs = jnp.concatenate(heads, axis=0).astype(BF16)
        sl = slice(kh * HEAD_DIM, (kh + 1) * HEAD_DIM)
        s_w = jnp.where(band, _dot_nt(qs, k[:, sl].astype(BF16)) * scale, MASK_NEG)
        s_c = _dot_nt(qs, kc[:, sl].astype(BF16)) * scale
        sk = _sink_column(sink_ref, layer, kh, t)
        o = _sink_softmax_av([s_w, s_c], [v[:, sl], vc[:, sl]], sk)
        outs += [o[g * t:(g + 1) * t] for g in range(KV_GROUP)]
    att_ref[...] = jnp.concatenate(outs, axis=-1).astype(BF16)


def _attn_lat(sink, p, cache_k, cache_v, gq, gk, cos_t, sin_t, layer):
    n = p.shape[0]
    t = ATT_BLOCK
    nb = DEC_SEQ // t
    kern = functools.partial(_attn_lat_kernel, layer=layer)
    return pl.pallas_call(
        kern,
        grid=(DEC_BATCH, nb),
        in_specs=[
            pl.BlockSpec(memory_space=pltpu.SMEM),
            pl.BlockSpec((t, Q_W), lambda b, j: (b * nb + j, COL_AQ // Q_W)),
            pl.BlockSpec((DEC_SEQ, 2 * KV_W), lambda b, j: (b, COL_AK // (2 * KV_W))),
            pl.BlockSpec((None, None, PAST_LEN, KV_W), lambda b, j: (b, layer, 0, 0)),
            pl.BlockSpec((None, None, PAST_LEN, KV_W), lambda b, j: (b, layer, 0, 0)),
            pl.BlockSpec((None, 1, LANES), lambda b, j: (layer, 0, 0)),
            pl.BlockSpec((None, 1, LANES), lambda b, j: (layer, 0, 0)),
            pl.BlockSpec((DEC_SEQ, LANES), lambda b, j: (0, 0)),
            pl.BlockSpec((DEC_SEQ, LANES), lambda b, j: (0, 0)),
        ],
        out_specs=pl.BlockSpec((t, Q_W), lambda b, j: (b * nb + j, 0)),
        out_shape=jax.ShapeDtypeStruct((n, Q_W), BF16),
        compiler_params=_cparams(("parallel", "parallel")),
        name="attn_lat",
    )(sink, p, p, cache_k, cache_v, gq, gk, cos_t, sin_t)


def _rope_tables():
    pos = jnp.arange(DEC_SEQ)
    rows = (pos // GRID_W).astype(F32)
    cols = (pos % GRID_W).astype(F32)
    half = HEAD_DIM // 2
    freqs = ROPE_BASE ** (-jnp.arange(0, half, 2, dtype=F32) / half)
    ang_r = rows[:, None] * freqs[None, :]
    ang_c = cols[:, None] * freqs[None, :]
    ang = jnp.concatenate([ang_r, ang_r, ang_c, ang_c], axis=-1)
    sign = jnp.tile(jnp.concatenate([-jnp.ones((16,), F32), jnp.ones((16,), F32)]), 2)
    cos_t = jnp.tile(jnp.cos(ang), (1, LANES // HEAD_DIM))
    sin_t = jnp.tile(jnp.sin(ang) * sign[None, :], (1, LANES // HEAD_DIM))
    return cos_t, sin_t


def _ssm_kernel(*refs, nblk, has_h0):
    if has_h0:
        x_ref, w_ref, bst_ref, cst_ref, lamt_ref, d_ref, h0_ref, z_ref, loc_scr, ent_scr = refs
    else:
        x_ref, w_ref, bst_ref, cst_ref, lamt_ref, d_ref, z_ref, fin_ref, loc_scr, ent_scr = refs
    ns = SSM_STATE
    t_blk = SSM_T
    ng = LANES // SSM_GROUP
    rows = nblk * SUBLANES
    per_half = LANES // SSM_GROUP
    x = x_ref[...].reshape(rows, t_blk * LANES)
    cols = [x[:, t * LANES:(t + 1) * LANES] for t in range(t_blk)]
    chunk = lax.broadcasted_iota(jnp.int32, (rows, LANES), 1) // SSM_GROUP
    is_fwd = lax.broadcasted_iota(jnp.int32, (SUBLANES, 2 * ns), 1) < ns

    def roll(a, chunks):
        shift = (chunks * SSM_GROUP) % LANES
        return a if shift == 0 else pltpu.roll(a, shift, 1)

    xg = []
    for g in range(ng):
        halves = []
        for h in range(t_blk // per_half):
            acc = None
            for j in range(per_half):
                r = roll(cols[h * per_half + j], j - g)
                acc = r if acc is None else jnp.where(chunk == j, r, acc)
            halves.append(acc)
        xg.append(jnp.concatenate(halves, axis=1))
        loc_scr[g] = _dot(xg[g].astype(BF16), bst_ref[g])
    l_re = [jnp.broadcast_to(lamt_ref[g, 0:1, :], (SUBLANES, 2 * ns)) for g in range(ng)]
    l_im = [jnp.broadcast_to(lamt_ref[g, 1:2, :], (SUBLANES, 2 * ns)) for g in range(ng)]
    if has_h0:
        s_re = [h0_ref[g, 0] for g in range(ng)]
        s_im = [h0_ref[g, 1] for g in range(ng)]
    else:
        s_re = [jnp.zeros((SUBLANES, 2 * ns), F32) for _ in range(ng)]
        s_im = [jnp.zeros((SUBLANES, 2 * ns), F32) for _ in range(ng)]
    for j in range(nblk):
        rf = slice(j * SUBLANES, (j + 1) * SUBLANES)
        rb = slice((nblk - 1 - j) * SUBLANES, (nblk - j) * SUBLANES)
        for g in range(ng):
            ent_scr[g, rf, 0:ns] = s_re[g][:, 0:ns]
            ent_scr[g, rb, ns:2 * ns] = s_re[g][:, ns:2 * ns]
            ent_scr[g, rf, 2 * ns:3 * ns] = s_im[g][:, 0:ns]
            ent_scr[g, rb, 3 * ns:4 * ns] = s_im[g][:, ns:2 * ns]
            loc_re = jnp.where(is_fwd, loc_scr[g, rf, 0:2 * ns], loc_scr[g, rb, 0:2 * ns])
            loc_im = jnp.where(is_fwd, loc_scr[g, rf, 2 * ns:4 * ns], loc_scr[g, rb, 2 * ns:4 * ns])
            n_re = l_re[g] * s_re[g] - l_im[g] * s_im[g] + loc_re
            n_im = l_re[g] * s_im[g] + l_im[g] * s_re[g] + loc_im
            s_re[g], s_im[g] = n_re, n_im
    if not has_h0:
        for g in range(ng):
            fin_ref[g, 0] = s_re[g]
            fin_ref[g, 1] = s_im[g]
    out_cols = [None] * t_blk
    for g in range(ng):
        y = (_dot(xg[g].astype(BF16), w_ref[g]) + _dot_nt(ent_scr[g].astype(BF16), cst_ref[g])
             + d_ref[g] * xg[g])
        zg = _gelu_tanh(y)
        for h in range(t_blk // per_half):
            zh = zg[:, h * LANES:(h + 1) * LANES]
            for j in range(per_half):
                t = h * per_half + j
                r = roll(zh, g - j)
                out_cols[t] = r if out_cols[t] is None else jnp.where(chunk == g, r, out_cols[t])
    z_ref[...] = jnp.concatenate(out_cols, axis=1).astype(BF16).reshape(nblk, SUBLANES, t_blk * LANES)


def _ssm(p, sw, layer, nseq, seq_len, h0=None):
    n = p.shape[0]
    t_blk = SSM_T
    nblk = seq_len // t_blk
    bw = t_blk * SSM_GROUP
    nsg = SSM_WIDTH // LANES
    ng = LANES // SSM_GROUP
    npad = -(-nseq // SUBLANES) * SUBLANES
    has_h0 = h0 is not None
    su = p[:, COL_SU:COL_SU + SSM_WIDTH].reshape(nseq, nblk, t_blk, nsg, LANES)
    x = su.transpose(1, 0, 3, 2, 4).reshape(nblk, nseq, nsg * t_blk * LANES)
    if npad != nseq:
        x = jnp.pad(x, ((0, 0), (0, npad - nseq), (0, 0)))
    kern = functools.partial(_ssm_kernel, nblk=nblk, has_h0=has_h0)
    wspec = lambda a, b: pl.BlockSpec((None, None, ng, a, b), lambda s, g: (layer, g, 0, 0, 0))
    grp = lambda a: a.reshape((DEPTH, nsg, ng) + a.shape[2:])
    xspec = pl.BlockSpec((nblk, SUBLANES, t_blk * LANES), lambda s, g: (0, s, g))
    in_specs = [xspec, wspec(bw, bw), wspec(bw, 4 * SSM_STATE), wspec(bw, 4 * SSM_STATE),
                wspec(2, 2 * SSM_STATE), wspec(1, bw)]
    args = [x, grp(sw['w']), grp(sw['bst']), grp(sw['cst']), grp(sw['lamt']), grp(sw['d'])]
    out_specs = [xspec]
    out_shape = [jax.ShapeDtypeStruct((nblk, npad, nsg * t_blk * LANES), BF16)]
    st_block = (ng, 2, SUBLANES, 2 * SSM_STATE)
    if has_h0:
        in_specs.append(pl.BlockSpec((None, None) + st_block, lambda s, g: (layer, g, 0, 0, 0, 0)))
        args.append(h0)
    else:
        out_specs.append(pl.BlockSpec((None, None) + st_block, lambda s, g: (s, g, 0, 0, 0, 0)))
        out_shape.append(jax.ShapeDtypeStruct((npad // SUBLANES, nsg) + st_block, F32))
    rows = nblk * SUBLANES
    outs = pl.pallas_call(
        kern,
        grid=(npad // SUBLANES, nsg),
        in_specs=in_specs,
        out_specs=out_specs,
        out_shape=out_shape,
        scratch_shapes=[pltpu.VMEM((ng, rows, 4 * SSM_STATE), F32), pltpu.VMEM((ng, rows, 4 * SSM_STATE), F32)],
        compiler_params=_cparams(("parallel", "parallel")),
        name="ssm",
    )(*args)
    z = outs[0][:, :nseq].reshape(nblk, nseq, nsg, t_blk, LANES).transpose(1, 0, 3, 2, 4).reshape(n, SSM_WIDTH)
    return (z,) + tuple(outs[1:])


def _hgrn_consts():
    c = HG_CHUNK
    i = np.arange(c)[:, None]
    r = np.arange(c)[None, :]
    m_all = np.zeros((2, HG_LEVELS, c, c), np.float32)
    lev_of = np.full((2, c, c), -1, np.int32)
    for lev in range(HG_LEVELS):
        n = 2 << lev
        half = n // 2
        mid = (i // n) * n + half
        upper = (i % n) >= half
        m_all[0, lev] = np.where(upper, (r >= mid) & (r <= i), (r > i) & (r < mid))
        m_all[1, lev] = np.where(upper, (r >= mid) & (r < i), (r >= i) & (r < mid))
        same = (i // n) == (r // n)
        r_upper = (r % n) >= half
        lev_of[0][same & upper & ~r_upper] = lev
        lev_of[1][same & ~upper & r_upper] = lev
    cum = np.stack([r <= i, r >= i]).astype(np.float32)
    m_small = np.concatenate([cum] + [m_all[:, lev] for lev in HG_MXU_LEVELS], axis=1)
    return (jnp.asarray(np.concatenate([m_small, m_small], axis=-1), BF16), jnp.asarray(lev_of))


def _hgrn_chunks(chains, states, hq_ref, hi_ref, m_ref, lev_ref):
    c = HG_CHUNK
    nb = c // SUBLANES
    w = []
    for d, r0, f_ref, lb, cum_ref in chains:
        q = _silu(hq_ref[pl.ds(r0, c), :])
        fg = lb + (1.0 - lb) * _sigmoid(f_ref[pl.ds(r0, c), :])
        k = 1.0 - fg
        log2f = jnp.log(fg) * (1.0 / math.log(2.0))
        v = hi_ref[pl.ds(r0, c), :]
        hi = log2f.astype(BF16)
        lo = (log2f - hi.astype(F32)).astype(BF16)
        sums = _dot(m_ref[d], jnp.concatenate([hi, lo], axis=0))
        w.append(dict(d=d, q=q, fg=fg, k=k, v=v, vb=v.astype(BF16), sums=sums, cum=sums[0:c],
                      cum_ref=cum_ref, att=[None] * nb))
    for x in w:
        x['cum_ref'][...] = x['cum']

    def select_into(x, rows, lev, a):
        att = x['att']
        for j in range(rows.stop // SUBLANES - rows.start // SUBLANES):
            i = rows.start // SUBLANES + j
            blk = a[j * SUBLANES:(j + 1) * SUBLANES]
            mask = lev_ref[x['d'], i * SUBLANES:(i + 1) * SUBLANES, :] == lev
            att[i] = jnp.where(mask, blk, 0.0 if att[i] is None else att[i])

    full = slice(0, c)
    prods = [_dot_nt((x['q'] * x['fg']).astype(BF16), x['k'].astype(BF16)) for x in w]
    for x, a in zip(w, prods):
        select_into(x, full, 0, a)
    for n_lev, lev in enumerate(HG_MXU_LEVELS):
        prods = []
        for x in w:
            g = jnp.exp2(x['sums'][(n_lev + 1) * c:(n_lev + 2) * c])
            prods.append(_dot_nt((x['q'] * g).astype(BF16), (x['k'] * g).astype(BF16)))
        for x, a in zip(w, prods):
            select_into(x, full, lev, a)
    for lev in range(HG_MXU_LEVELS[-1] + 1, HG_LEVELS):
        half = 1 << lev
        n = 2 * half
        prods = []
        for x in w:
            d, q, k, cum = x['d'], x['q'], x['k'], x['cum']
            q_parts, k_parts, q_rows = [], [], []
            for j in range(c // n):
                lower = slice(j * n, j * n + half)
                upper = slice(j * n + half, (j + 1) * n)
                qr, kr = (upper, lower) if d == 0 else (lower, upper)
                ref_row = j * n + half - 1 if d == 0 else j * n + half
                ref = jnp.broadcast_to(x['cum_ref'][ref_row:ref_row + 1, :], (half, HG_KDIM))
                q_parts.append(q[qr] * jnp.exp2(cum[qr] - ref))
                k_scaled = k[kr] * jnp.exp2(ref - cum[kr])
                k_parts += [k_scaled, k[qr]] if d == 0 else [k[qr], k_scaled]
                q_rows.append(qr)
            prods.append((q_rows, _dot_nt(jnp.concatenate(q_parts, axis=0).astype(BF16),
                                          jnp.concatenate(k_parts, axis=0).astype(BF16))))
        for x, (q_rows, a) in zip(w, prods):
            for j, qr in enumerate(q_rows):
                select_into(x, qr, lev, a[j * half:(j + 1) * half])
    for x in w:
        d, q, k, cum = x['d'], x['q'], x['k'], x['cum']
        last = c - 1 if d == 0 else 0
        tot_row = cum[last:last + 1, :]
        x['lhs'] = jnp.concatenate([jnp.concatenate(x['att'], axis=0).astype(BF16),
                                    (q * jnp.exp2(cum)).astype(BF16)], axis=1)
        x['kg'] = (k * jnp.exp2(tot_row - cum)).astype(BF16)
        x['dv'] = jnp.sum(q * k, axis=-1, keepdims=True) * x['v']
        x['tot_col'] = jnp.broadcast_to(jnp.exp2(tot_row), (c, HG_KDIM)).T
    states = list(states)
    outs = []
    for x in w:
        st = states[x['d']]
        outs.append(x['dv'] + _dot(x['lhs'], jnp.concatenate([x['vb'], st.astype(BF16)], axis=0)))
        states[x['d']] = st * x['tot_col'] + _dot_tn(x['kg'], x['vb'])
    return outs, states


def _hgrn_kernel(*refs, seq_len, has_s0):
    if has_s0:
        (hq_ref, ff_ref, fb_ref, hi_ref, hgt_ref, lb_ref, ng_ref, m_ref, lev_ref, s0_ref,
         o_ref, of_scr, ob_scr, cum_scr) = refs
    else:
        (hq_ref, ff_ref, fb_ref, hi_ref, hgt_ref, lb_ref, ng_ref, m_ref, lev_ref,
         o_ref, sfin_ref, of_scr, ob_scr, cum_scr) = refs
    c = HG_CHUNK
    nch = seq_len // c
    lb_f = lb_ref[0:1, :]
    lb_b = lb_ref[1:2, :]

    def step(ci, carry):
        chains, rows = [], []
        for u in range(HG_UNROLL):
            cf = ci * HG_UNROLL + u
            rf = pl.multiple_of(cf * c, c)
            rb = pl.multiple_of((nch - 1 - cf) * c, c)
            chains += [(0, rf, ff_ref, lb_f, cum_scr.at[2 * u]), (1, rb, fb_ref, lb_b, cum_scr.at[2 * u + 1])]
            rows += [(of_scr, rf), (ob_scr, rb)]
        outs, states = _hgrn_chunks(chains, carry, hq_ref, hi_ref, m_ref, lev_ref)
        for (scr, r0), o in zip(rows, outs):
            scr[pl.ds(r0, c), :] = o
        return tuple(states)

    if has_s0:
        init = (s0_ref[0], s0_ref[1])
    else:
        init = (jnp.zeros((HG_KDIM, HG_VDIM), F32), jnp.zeros((HG_KDIM, HG_VDIM), F32))
    if nch == HG_UNROLL:
        st_f, st_b = step(0, init)
    else:
        st_f, st_b = lax.fori_loop(0, nch // HG_UNROLL, step, init)
    if not has_s0:
        sfin_ref[0] = st_f
        sfin_ref[1] = st_b

    o = of_scr[...] + ob_scr[...]
    ms = jnp.mean(o * o, axis=-1, keepdims=True)
    y = o * lax.rsqrt(ms + EPS) * ng_ref[...]
    o_ref[...] = (y * _silu(hgt_ref[...])).astype(BF16)


def _hgrn(p, lb, norm_g, m_all, lev_of, layer, seq_len, s0=None):
    n = p.shape[0]
    nseq = n // seq_len
    has_s0 = s0 is not None
    kern = functools.partial(_hgrn_kernel, seq_len=seq_len, has_s0=has_s0)
    base = COL_HG // LANES

    def col(k):
        return pl.BlockSpec((seq_len, LANES), lambda b, h, k=k: (b, base + k * HG_HEADS + h))

    c = HG_CHUNK
    in_specs = [col(0), col(1), col(2), col(3), col(4),
                pl.BlockSpec((None, None, 2, HG_KDIM), lambda b, h: (layer, h, 0, 0)),
                pl.BlockSpec((None, 1, HG_VDIM), lambda b, h: (layer, 0, 0)),
                pl.BlockSpec((2, (len(HG_MXU_LEVELS) + 1) * c, 2 * c), lambda b, h: (0, 0, 0)),
                pl.BlockSpec((2, c, c), lambda b, h: (0, 0, 0))]
    args = [p, p, p, p, p, lb, norm_g, m_all, lev_of]
    out_specs = [pl.BlockSpec((seq_len, HG_VDIM), lambda b, h: (b, h))]
    out_shape = [jax.ShapeDtypeStruct((n, HG_VW), BF16)]
    if has_s0:
        in_specs.append(pl.BlockSpec((None, None, 2, None, HG_KDIM, HG_VDIM), lambda b, h: (b, layer, 0, h, 0, 0)))
        args.append(s0)
    else:
        out_specs.append(pl.BlockSpec((None, 2, None, HG_KDIM, HG_VDIM), lambda b, h: (b, 0, h, 0, 0)))
        out_shape.append(jax.ShapeDtypeStruct((nseq, 2, HG_HEADS, HG_KDIM, HG_VDIM), F32))
    return pl.pallas_call(
        kern,
        grid=(nseq, HG_HEADS),
        in_specs=in_specs,
        out_specs=out_specs,
        out_shape=out_shape,
        scratch_shapes=[pltpu.VMEM((seq_len, HG_VDIM), F32), pltpu.VMEM((seq_len, HG_VDIM), F32),
                        pltpu.VMEM((2 * HG_UNROLL, HG_CHUNK, HG_KDIM), F32)],
        compiler_params=_cparams(("parallel", "parallel")),
        name="hgrn",
    )(*args)


def _router(logits):
    lane = lax.broadcasted_iota(jnp.int32, logits.shape, 1)
    big = jnp.int32(LANES)
    is_g = lane < N_GROUPS
    gl = jnp.where(is_g, logits, -jnp.inf)
    gmax = jnp.max(gl, axis=-1, keepdims=True)
    gsum = jnp.sum(jnp.exp(gl - gmax), axis=-1, keepdims=True)
    g_idx = jnp.min(jnp.where(is_g & (gl == gmax), lane, big), axis=-1, keepdims=True)
    g_prob = 1.0 / gsum
    e_lane = lane - ROUTER_OFF
    sel = (e_lane >= 0) & (e_lane < N_EXPERTS) & ((e_lane >> 2) == g_idx)
    el = jnp.where(sel, logits, -jnp.inf)
    emax = jnp.max(el, axis=-1, keepdims=True)
    eexp = jnp.exp(el - emax)
    ep = eexp / jnp.sum(eexp, axis=-1, keepdims=True)
    p1 = jnp.max(ep, axis=-1, keepdims=True)
    i1 = jnp.min(jnp.where(sel & (ep == p1), lane, big), axis=-1, keepdims=True)
    rest = sel & (lane != i1)
    ep2 = jnp.where(rest, ep, -1.0)
    p2 = jnp.max(ep2, axis=-1, keepdims=True)
    i2 = jnp.min(jnp.where(rest & (ep2 == p2), lane, big), axis=-1, keepdims=True)
    den = p1 + p2
    w1 = g_prob * (p1 / den)
    w2 = g_prob * (p2 / den)
    return jnp.where(lane == i1, w1, jnp.where(lane == i2, w2, 0.0))


def _merge_kernel(x_ref, att_ref, z_ref, hg_ref, mg_ref, mod_ref, n2_ref,
                  wao_ref, wga_ref, wgb_ref, who_ref, wout_ref, wr_ref, br_ref,
                  xm_ref, h2_ref, comb_ref):
    y_att = _dot(att_ref[...], wao_ref[...])
    z = z_ref[...]
    y_ssm = _dot(z, wga_ref[...]) * _sigmoid(_dot(z, wgb_ref[...]))
    y_hg = _dot(hg_ref[...], who_ref[...])
    merged = (_sigmoid(mg_ref[:, 0:D_MODEL]) * y_att
              + _sigmoid(mg_ref[:, D_MODEL:2 * D_MODEL]) * y_ssm
              + _sigmoid(mg_ref[:, 2 * D_MODEL:3 * D_MODEL]) * y_hg)
    xm = x_ref[...] + mod_ref[2:3, :] * _dot(merged.astype(BF16), wout_ref[...])
    xm_ref[...] = xm
    ms = jnp.mean(xm * xm, axis=-1, keepdims=True)
    h2 = xm * lax.rsqrt(ms + EPS) * n2_ref[...] * (1.0 + mod_ref[4:5, :]) + mod_ref[3:4, :]
    h2_hi = h2.astype(BF16)
    h2_ref[...] = h2_hi
    h2_lo = (h2 - h2_hi.astype(F32)).astype(BF16)
    logits = _dot(jnp.concatenate([h2_hi, h2_lo, h2_hi], axis=1), wr_ref[...]) + br_ref[...]
    comb_ref[...] = _router(logits)


def _merge(x, att, z, hg, p, mod, n2g, w, layer, seq_len, cond_row0):
    n = x.shape[0]
    tm = 256
    rows_per_cond = seq_len if cond_row0 else n
    mod_idx = (lambda i: (layer, cond_row0 + (i * tm) // rows_per_cond, 0, 0))
    row = lambda width: pl.BlockSpec((tm, width), lambda i: (i, 0))
    wspec = lambda a, b: pl.BlockSpec((None, a, b), lambda i: (layer, 0, 0))
    return pl.pallas_call(
        _merge_kernel,
        grid=(n // tm,),
        in_specs=[
            row(D_MODEL), row(Q_W), row(SSM_WIDTH), row(HG_VW),
            pl.BlockSpec((tm, MG_W), lambda i: (i, COL_MG // MG_W)),
            pl.BlockSpec((None, None, 6, D_MODEL), mod_idx),
            wspec(1, D_MODEL),
            wspec(Q_W, D_MODEL), wspec(SSM_WIDTH, D_MODEL), wspec(SSM_WIDTH, D_MODEL),
            wspec(HG_VW, D_MODEL), wspec(D_MODEL, D_MODEL), wspec(3 * D_MODEL, LANES), wspec(1, LANES),
        ],
        out_specs=[row(D_MODEL), row(D_MODEL), row(LANES)],
        out_shape=[
            jax.ShapeDtypeStruct((n, D_MODEL), F32),
            jax.ShapeDtypeStruct((n, D_MODEL), BF16),
            jax.ShapeDtypeStruct((n, LANES), F32),
        ],
        compiler_params=_cparams(("parallel",)),
        name="merge",
    )(x, att, z, hg, p, mod, n2g, w['w_attn_o'], w['w_glu_a'], w['w_glu_b'], w['w_hg_o'], w['w_out'],
      w['w_route'], w['b_route'])


def _moe_kernel(h_ref, comb_ref, xm_ref, mod_ref, wg_ref, wu_ref, wd_ref, o_ref, acc_scr):
    grp = pl.program_id(1)
    comb = comb_ref[...]
    lane = lax.broadcasted_iota(jnp.int32, comb.shape, 1)
    h = h_ref[...]
    hg = _dot(h, wg_ref[...])
    hu = _dot(h, wu_ref[...])
    parts = []
    for e in range(EXPERTS_PER_GROUP):
        cw = jnp.sum(jnp.where(lane == ROUTER_OFF + grp * EXPERTS_PER_GROUP + e, comb, 0.0),
                     axis=-1, keepdims=True)
        sl = slice(e * EXPERT_FF, (e + 1) * EXPERT_FF)
        parts.append((_silu(hg[:, sl]) * hu[:, sl] * cw).astype(BF16))
    y = _dot(jnp.concatenate(parts, axis=1), wd_ref[...])

    @pl.when(grp == 0)
    def _():
        acc_scr[...] = y

    @pl.when(grp > 0)
    def _():
        acc_scr[...] += y

    @pl.when(grp == N_GROUPS - 1)
    def _():
        o_ref[...] = xm_ref[...] + mod_ref[5:6, :] * acc_scr[...]


def _moe(h2, comb, xm, mod, w, layer, seq_len, cond_row0):
    n = h2.shape[0]
    tm = 512
    gw = EXPERTS_PER_GROUP * EXPERT_FF
    rows_per_cond = seq_len if cond_row0 else n
    mod_idx = (lambda i, g: (layer, cond_row0 + (i * tm) // rows_per_cond, 0, 0))
    return pl.pallas_call(
        _moe_kernel,
        grid=(n // tm, N_GROUPS),
        in_specs=[
            pl.BlockSpec((tm, D_MODEL), lambda i, g: (i, 0)),
            pl.BlockSpec((tm, LANES), lambda i, g: (i, 0)),
            pl.BlockSpec((tm, D_MODEL), lambda i, g: (i, 0)),
            pl.BlockSpec((None, None, 6, D_MODEL), mod_idx),
            pl.BlockSpec((None, D_MODEL, gw), lambda i, g: (layer, 0, g)),
            pl.BlockSpec((None, D_MODEL, gw), lambda i, g: (layer, 0, g)),
            pl.BlockSpec((None, gw, D_MODEL), lambda i, g: (layer, g, 0)),
        ],
        out_specs=pl.BlockSpec((tm, D_MODEL), lambda i, g: (i, 0)),
        out_shape=jax.ShapeDtypeStruct((n, D_MODEL), F32),
        scratch_shapes=[pltpu.VMEM((tm, D_MODEL), F32)],
        compiler_params=_cparams(("parallel", "arbitrary")),
        name="moe",
    )(h2, comb, xm, mod, w['w_e_gate'], w['w_e_up'], w['w_e_down'])


def _layer(x, layer, w, seq_len, cond_row0, cache):
    p = _in_proj(x, w['mod'], w['norm1_g'], w['w_in'], layer, seq_len, cond_row0)
    nseq = x.shape[0] // seq_len
    if cache is None:
        att, k_new, v_new = _attn_ctx(w['attn_sink'], p, w['q_norm_g'], w['k_norm_g'], layer)
        z, ssm_fin = _ssm(p, w['ssm'], layer, nseq, seq_len)
        hg, hg_fin = _hgrn(p, w['hg_lb'], w['hg_norm_g'], w['hg_m'], w['hg_lev'], layer, seq_len)
        ctx = (k_new, v_new, ssm_fin, hg_fin)
    else:
        cache_k, cache_v, h0, s0 = cache
        att = _attn_lat(w['attn_sink'], p, cache_k, cache_v, w['q_norm_g'], w['k_norm_g'],
                        w['rope_cos'], w['rope_sin'], layer)
        (z,) = _ssm(p, w['ssm'], layer, nseq, seq_len, h0=h0)
        (hg,) = _hgrn(p, w['hg_lb'], w['hg_norm_g'], w['hg_m'], w['hg_lev'], layer, seq_len, s0=s0)
        ctx = None
    xm, h2, comb = _merge(x, att, z, hg, p, w['mod'], w['norm2_g'], w, layer, seq_len, cond_row0)
    x = _moe(h2, comb, xm, w['mod'], w, layer, seq_len, cond_row0)
    return x, ctx


def kernel(x_prompt, x_sample, cache_k, cache_v, state_ssm_re, state_ssm_im, state_hgrn, c, c_ctx, w_mod, b_mod, norm1_g, norm2_g, w_in, q_norm_g, k_norm_g, attn_sink, w_attn_o, ssm_a_re, ssm_a_im, ssm_log_dt, ssm_b_re, ssm_b_im, ssm_c_re, ssm_c_im, ssm_d, w_glu_a, w_glu_b, hg_lb, hg_norm_g, w_hg_o, w_out, w_group, b_group, w_router, b_router, w_e_gate, w_e_up, w_e_down):
    w = {}
    w['mod'] = _modulation(c, c_ctx, w_mod, b_mod)
    w['norm1_g'] = norm1_g.reshape(DEPTH, 1, D_MODEL)
    w['norm2_g'] = norm2_g.reshape(DEPTH, 1, D_MODEL)
    w['w_in'] = jnp.concatenate(
        [w_in[:, :, 3840:], w_in[:, :, 1280:3840], w_in[:, :, 0:768], w_in[:, :, 768:1280]], axis=-1).astype(BF16)
    w['q_norm_g'] = jnp.tile(q_norm_g, (1, LANES // HEAD_DIM)).reshape(DEPTH, 1, LANES)
    w['k_norm_g'] = jnp.tile(k_norm_g, (1, LANES // HEAD_DIM)).reshape(DEPTH, 1, LANES)
    w['attn_sink'] = attn_sink
    w['rope_cos'], w['rope_sin'] = _rope_tables()
    w['ssm'] = _ssm_prep(ssm_a_re, ssm_a_im, ssm_log_dt, ssm_b_re, ssm_b_im, ssm_c_re, ssm_c_im, ssm_d)
    w['hg_lb'] = _lower_bounds(hg_lb)
    w['hg_norm_g'] = hg_norm_g.reshape(DEPTH, 1, HG_VDIM)
    w['hg_m'], w['hg_lev'] = _hgrn_consts()
    for name, val in (('w_attn_o', w_attn_o), ('w_glu_a', w_glu_a), ('w_glu_b', w_glu_b), ('w_hg_o', w_hg_o),
                      ('w_out', w_out)):
        w[name] = val.astype(BF16)
    ff_all = N_EXPERTS * EXPERT_FF
    w['w_e_gate'] = w_e_gate.astype(BF16).transpose(0, 2, 1, 3).reshape(DEPTH, D_MODEL, ff_all)
    w['w_e_up'] = w_e_up.astype(BF16).transpose(0, 2, 1, 3).reshape(DEPTH, D_MODEL, ff_all)
    w['w_e_down'] = w_e_down.astype(BF16).reshape(DEPTH, ff_all, D_MODEL)
    pad = LANES - N_GROUPS - N_EXPERTS
    w_route = jnp.concatenate([w_group, w_router, jnp.zeros((DEPTH, D_MODEL, pad), F32)], axis=-1)
    w_route_hi = w_route.astype(BF16)
    w_route_lo = (w_route - w_route_hi.astype(F32)).astype(BF16)
    w['w_route'] = jnp.concatenate([w_route_hi, w_route_hi, w_route_lo], axis=1)
    w['b_route'] = jnp.concatenate([b_group, b_router, jnp.zeros((DEPTH, pad), F32)], axis=-1).reshape(DEPTH, 1, LANES)

    ck = cache_k.reshape(DEC_BATCH, DEPTH, PAST_LEN, KV_W)
    cv = cache_v.reshape(DEC_BATCH, DEPTH, PAST_LEN, KV_W)

    ng = LANES // SSM_GROUP
    h0 = jnp.stack([state_ssm_re, state_ssm_im]).transpose(2, 4, 0, 1, 3, 5).reshape(
        DEPTH, SSM_GROUPS // ng, ng, 2, DEC_BATCH, 2 * SSM_STATE)
    h0 = jnp.pad(h0, ((0, 0),) * 4 + ((0, SUBLANES - DEC_BATCH), (0, 0)))

    xp = x_prompt.reshape(BATCH * SEQ, D_MODEL)
    xs = x_sample.reshape(DEC_BATCH * DEC_SEQ, D_MODEL)
    new_k, new_v, new_re, new_im, new_hg = [], [], [], [], []
    for l in range(DEPTH):
        xp, ctx = _layer(xp, l, w, SEQ, 0, None)
        new_k.append(ctx[0].reshape(BATCH, SEQ, N_KV_HEADS, HEAD_DIM))
        new_v.append(ctx[1].reshape(BATCH, SEQ, N_KV_HEADS, HEAD_DIM))
        fin = ctx[2].reshape(BATCH // SUBLANES, SSM_GROUPS // ng, ng, 2, SUBLANES, 2, SSM_STATE)
        fin = fin.transpose(3, 0, 4, 5, 1, 2, 6).reshape(2, BATCH, 2, SSM_GROUPS, SSM_STATE)
        new_re.append(fin[0])
        new_im.append(fin[1])
        new_hg.append(ctx[3])
        xs, _ = _layer(xs, l, w, DEC_SEQ, 1, (ck, cv, h0, state_hgrn))
    return (xp.reshape(BATCH, SEQ, D_MODEL), xs.reshape(DEC_BATCH, DEC_SEQ, D_MODEL),
            jnp.stack(new_k, axis=1), jnp.stack(new_v, axis=1), jnp.stack(new_re, axis=1),
            jnp.stack(new_im, axis=1), jnp.stack(new_hg, axis=1))
```

```python
import functools
import math

import numpy as np
import jax
import jax.numpy as jnp
from jax import lax
from jax.experimental import pallas as pl
from jax.experimental.pallas import tpu as pltpu

F32 = jnp.float32
BF16 = jnp.bfloat16

D_MODEL = 1024
BATCH = 32
SEQ = 256
DEPTH = 4
DEC_BATCH = 4
DEC_SEQ = 1024
PAST_LEN = 512
GRID_W = 64
N_HEADS = 8
N_KV_HEADS = 2
HEAD_DIM = 64
KV_GROUP = N_HEADS // N_KV_HEADS
WINDOW = 128
ATT_BLOCK = 128
ROPE_BASE = 10000.0
SSM_WIDTH = 512
SSM_GROUP = 16
SSM_GROUPS = SSM_WIDTH // SSM_GROUP
SSM_STATE = 64
HG_HEADS = 4
HG_KDIM = 128
HG_VDIM = 128
N_BRANCHES = 3
N_GROUPS = 4
EXPERTS_PER_GROUP = 4
N_EXPERTS = N_GROUPS * EXPERTS_PER_GROUP
EXPERT_FF = 256
EPS = 1e-6
Q_W = N_HEADS * HEAD_DIM
KV_W = N_KV_HEADS * HEAD_DIM
HG_KW = HG_HEADS * HG_KDIM
HG_VW = HG_HEADS * HG_VDIM

LANES = 128
SUBLANES = 8
VMEM_LIMIT = 56 * 1024 * 1024

MG_W = N_BRANCHES * D_MODEL
COL_MG = 0
COL_HG = COL_MG + MG_W
COL_AQ = COL_HG + 3 * HG_KW + 2 * HG_VW
COL_AK = COL_AQ + Q_W
COL_SU = COL_AK + 2 * KV_W
IN_WIDTH = COL_SU + SSM_WIDTH

SSM_T = 16
HG_CHUNK = 128
HG_LEVELS = 7
HG_CHAINS = 8
HG_MXU_LEVELS = (1, 2)
ROUTER_OFF = N_GROUPS
MASK_NEG = -1e30


def _cparams(sem):
    return pltpu.CompilerParams(dimension_semantics=sem, vmem_limit_bytes=VMEM_LIMIT)


def _sigmoid(x):
    return 1.0 / (1.0 + jnp.exp(-x))


def _silu(x):
    return x * _sigmoid(x)


def _gelu_tanh(x):
    return 0.5 * x * (1.0 + jnp.tanh(math.sqrt(2.0 / math.pi) * (x + 0.044715 * (x * x * x))))


def _dot(a, b):
    return jnp.dot(a, b, preferred_element_type=F32)


def _dot_nt(a, b):
    return lax.dot_general(a, b, (((1,), (1,)), ((), ())), preferred_element_type=F32)


def _dot_tn(a, b):
    return lax.dot_general(a, b, (((0,), (0,)), ((), ())), preferred_element_type=F32)


def _shift_lanes(a, s):
    if s == 0:
        return a
    lo, hi = a[:, :LANES], a[:, LANES:]
    zero = jnp.zeros_like(lo)
    lane = lax.broadcasted_iota(jnp.int32, lo.shape, 1)
    if s > 0:
        if s >= LANES:
            t = s - LANES
            out_hi = lo if t == 0 else jnp.where(lane >= t, pltpu.roll(lo, t, 1), 0.0)
            return jnp.concatenate([zero, out_hi], axis=1)
        r_lo, r_hi = pltpu.roll(lo, s, 1), pltpu.roll(hi, s, 1)
        return jnp.concatenate([jnp.where(lane >= s, r_lo, 0.0), jnp.where(lane >= s, r_hi, r_lo)], axis=1)
    s = -s
    if s >= LANES:
        t = s - LANES
        out_lo = hi if t == 0 else jnp.where(lane < LANES - t, pltpu.roll(hi, LANES - t, 1), 0.0)
        return jnp.concatenate([out_lo, zero], axis=1)
    r_lo, r_hi = pltpu.roll(lo, LANES - s, 1), pltpu.roll(hi, LANES - s, 1)
    keep = lane < LANES - s
    return jnp.concatenate([jnp.where(keep, r_lo, r_hi), jnp.where(keep, r_hi, 0.0)], axis=1)


def _ssm_prep_kernel(are_ref, aim_ref, ldt_ref, btre_ref, btim_ref, cre_ref, cim_ref,
                     w_ref, bst_ref, cst_ref, lamt_ref):
    t_blk = SSM_T
    ns = SSM_STATE
    hp = lax.Precision.HIGHEST
    jf = lax.broadcasted_iota(jnp.int32, (2 * t_blk, ns), 0).astype(F32)
    bt_re = btre_ref[...]
    bt_im = btim_ref[...]
    c_re = cre_ref[...]
    c_im = cim_ref[...]
    p_re, p_im, bb_re, bb_im, kt = [], [], [], [], []
    for d in range(2):
        a_re = are_ref[d:d + 1, :]
        a_im = aim_ref[d:d + 1, :]
        dt = jnp.exp(ldt_ref[d:d + 1, :])
        mag = jnp.exp(jf * (a_re * dt))
        ang = jf * (a_im * dt)
        p_re.append(mag * jnp.cos(ang))
        p_im.append(mag * jnp.sin(ang))
        nr = p_re[d][1:2] - 1.0
        ni = p_im[d][1:2]
        den = a_re * a_re + a_im * a_im
        f_re = (nr * a_re + ni * a_im) / den
        f_im = (ni * a_re - nr * a_im) / den
        bb_re.append(f_re * bt_re - f_im * bt_im)
        bb_im.append(f_re * bt_im + f_im * bt_re)
        lags = range(t_blk) if d == 0 else range(t_blk - 1, -1, -1)
        a_parts_re = [c_re * p_re[d][j:j + 1] - c_im * p_im[d][j:j + 1] for j in lags]
        a_parts_im = [c_re * p_im[d][j:j + 1] + c_im * p_re[d][j:j + 1] for j in lags]
        nt = (((1,), (1,)), ((), ()))
        kt.append(lax.dot_general(bb_re[d], jnp.concatenate(a_parts_re, axis=0), nt,
                                  precision=hp, preferred_element_type=F32)
                  - lax.dot_general(bb_im[d], jnp.concatenate(a_parts_im, axis=0), nt,
                                    precision=hp, preferred_element_type=F32))
        lamt_ref[0:1, d * ns:(d + 1) * ns] = p_re[d][t_blk:t_blk + 1]
        lamt_ref[1:2, d * ns:(d + 1) * ns] = p_im[d][t_blk:t_blk + 1]
    for t in range(t_blk):
        rows = slice(t * SSM_GROUP, (t + 1) * SSM_GROUP)
        w_ref[rows, :] = (_shift_lanes(kt[0], t * SSM_GROUP)
                          + _shift_lanes(kt[1], -(t_blk - 1 - t) * SSM_GROUP)).astype(BF16)
        ef, eb = t_blk - 1 - t, t
        bst = [bb_re[0] * p_re[0][ef:ef + 1] - bb_im[0] * p_im[0][ef:ef + 1],
               bb_re[1] * p_re[1][eb:eb + 1] - bb_im[1] * p_im[1][eb:eb + 1],
               bb_re[0] * p_im[0][ef:ef + 1] + bb_im[0] * p_re[0][ef:ef + 1],
               bb_re[1] * p_im[1][eb:eb + 1] + bb_im[1] * p_re[1][eb:eb + 1]]
        bst_ref[rows, :] = jnp.concatenate(bst, axis=1).astype(BF16)
        ef, eb = t + 1, t_blk - t
        cst = [c_re * p_re[0][ef:ef + 1] - c_im * p_im[0][ef:ef + 1],
               c_re * p_re[1][eb:eb + 1] - c_im * p_im[1][eb:eb + 1],
               -(c_re * p_im[0][ef:ef + 1] + c_im * p_re[0][ef:ef + 1]),
               -(c_re * p_im[1][eb:eb + 1] + c_im * p_re[1][eb:eb + 1])]
        cst_ref[rows, :] = jnp.concatenate(cst, axis=1).astype(BF16)


def _ssm_prep(ssm_a_re, ssm_a_im, ssm_log_dt, ssm_b_re, ssm_b_im, ssm_c_re, ssm_c_im, ssm_d):
    t_blk = SSM_T
    bw = t_blk * SSM_GROUP
    lgd = lambda a: a.transpose(0, 2, 1, 3)
    ldt = jnp.broadcast_to(ssm_log_dt[..., None], ssm_a_re.shape)
    vec = pl.BlockSpec((None, None, 2, SSM_STATE), lambda l, g: (l, g, 0, 0))
    mat = pl.BlockSpec((None, None, SSM_GROUP, SSM_STATE), lambda l, g: (l, g, 0, 0))
    op = lambda width: pl.BlockSpec((None, None, bw, width), lambda l, g: (l, g, 0, 0))
    w, bst, cst, lamt = pl.pallas_call(
        _ssm_prep_kernel,
        grid=(DEPTH, SSM_GROUPS),
        in_specs=[vec, vec, vec, mat, mat, mat, mat],
        out_specs=[op(bw), op(4 * SSM_STATE), op(4 * SSM_STATE),
                   pl.BlockSpec((None, None, 2, 2 * SSM_STATE), lambda l, g: (l, g, 0, 0))],
        out_shape=[
            jax.ShapeDtypeStruct((DEPTH, SSM_GROUPS, bw, bw), BF16),
            jax.ShapeDtypeStruct((DEPTH, SSM_GROUPS, bw, 4 * SSM_STATE), BF16),
            jax.ShapeDtypeStruct((DEPTH, SSM_GROUPS, bw, 4 * SSM_STATE), BF16),
            jax.ShapeDtypeStruct((DEPTH, SSM_GROUPS, 2, 2 * SSM_STATE), F32),
        ],
        compiler_params=_cparams(("parallel", "parallel")),
        name="ssm_prep",
    )(lgd(ssm_a_re), lgd(ssm_a_im), lgd(ldt), jnp.swapaxes(ssm_b_re, -1, -2), jnp.swapaxes(ssm_b_im, -1, -2),
      ssm_c_re, ssm_c_im)
    d_row = jnp.tile(ssm_d, (1, 1, t_blk)).reshape(DEPTH, SSM_GROUPS, 1, bw)
    return dict(w=w, bst=bst, cst=cst, lamt=lamt, d=d_row)


def _lb_kernel(x_ref, o_ref):
    x = x_ref[...]
    m = jnp.max(x, axis=0, keepdims=True)
    e = jnp.exp(x - m)
    s = e / jnp.sum(e, axis=0, keepdims=True)
    run = jnp.zeros_like(s[0:1])
    o_ref[0:1, :] = run
    for l in range(1, DEPTH):
        run = run + s[l:l + 1]
        o_ref[l:l + 1, :] = run


def _lower_bounds(hg_lb):
    w = 2 * HG_KW
    out = pl.pallas_call(
        _lb_kernel,
        out_shape=jax.ShapeDtypeStruct((DEPTH, w), F32),
        name="hgrn_lower_bounds",
    )(hg_lb.reshape(DEPTH, w))
    return out.reshape(DEPTH, 2, HG_HEADS, HG_KDIM).transpose(0, 2, 1, 3)


def _mod_kernel(c_ref, w_ref, b_ref, o_ref):
    c = c_ref[...]
    a = _silu(c).astype(BF16)
    o_ref[...] = _dot(a, w_ref[...].astype(BF16)) + b_ref[...]


def _modulation(c, c_ctx, w_mod, b_mod):
    rows = SUBLANES
    cond = jnp.concatenate([c_ctx[None, :], c, jnp.zeros((rows - 1 - DEC_BATCH, D_MODEL), F32)], axis=0)
    tn = D_MODEL
    out = pl.pallas_call(
        _mod_kernel,
        grid=(DEPTH, 6),
        in_specs=[
            pl.BlockSpec((rows, D_MODEL), lambda l, j: (0, 0)),
            pl.BlockSpec((None, D_MODEL, tn), lambda l, j: (l, 0, j)),
            pl.BlockSpec((None, 1, tn), lambda l, j: (l, 0, j)),
        ],
        out_specs=pl.BlockSpec((None, rows, tn), lambda l, j: (l, 0, j)),
        out_shape=jax.ShapeDtypeStruct((DEPTH, rows, 6 * D_MODEL), F32),
        compiler_params=_cparams(("parallel", "parallel")),
        name="modulation",
    )(cond, w_mod, b_mod.reshape(DEPTH, 1, 6 * D_MODEL))
    return out.reshape(DEPTH, rows, 6, D_MODEL)


def _inproj_kernel(x_ref, mod_ref, g_ref, w_ref, o_ref, h_scr):
    @pl.when(pl.program_id(1) == 0)
    def _():
        x = x_ref[...]
        ms = jnp.mean(x * x, axis=-1, keepdims=True)
        y = x * lax.rsqrt(ms + EPS) * g_ref[...]
        h = y * (1.0 + mod_ref[1:2, :]) + mod_ref[0:1, :]
        h_scr[...] = h.astype(BF16)

    o_ref[...] = _dot(h_scr[...], w_ref[...])


def _in_proj(x, mod, norm_g, w_in, layer, seq_len, cond_row0):
    n = x.shape[0]
    tm = 1024
    tn = 2304
    rows_per_cond = seq_len if cond_row0 else n
    mod_idx = (lambda i, j: (layer, cond_row0 + (i * tm) // rows_per_cond, 0, 0))
    return pl.pallas_call(
        _inproj_kernel,
        grid=(n // tm, IN_WIDTH // tn),
        in_specs=[
            pl.BlockSpec((tm, D_MODEL), lambda i, j: (i, 0)),
            pl.BlockSpec((None, None, 6, D_MODEL), mod_idx),
            pl.BlockSpec((None, 1, D_MODEL), lambda i, j: (layer, 0, 0)),
            pl.BlockSpec((None, D_MODEL, tn), lambda i, j: (layer, 0, j)),
        ],
        out_specs=pl.BlockSpec((tm, tn), lambda i, j: (i, j)),
        out_shape=jax.ShapeDtypeStruct((n, IN_WIDTH), F32),
        scratch_shapes=[pltpu.VMEM((tm, D_MODEL), BF16)],
        compiler_params=_cparams(("parallel", "arbitrary")),
        name="in_proj",
    )(x, mod, norm_g, w_in)


def _pair_norm(x, g):
    xx = x * x
    s_a = jnp.sum(xx[:, :HEAD_DIM], axis=-1, keepdims=True)
    s_b = jnp.sum(xx[:, HEAD_DIM:], axis=-1, keepdims=True)
    lane = lax.broadcasted_iota(jnp.int32, x.shape, 1)
    ms = jnp.where(lane < HEAD_DIM, s_a, s_b) * (1.0 / HEAD_DIM)
    return x * lax.rsqrt(ms + EPS) * g


def _rope(x, cos, sin_signed):
    lane = lax.broadcasted_iota(jnp.int32, x.shape, 1)
    first = (lane & 31) < 16
    partner = jnp.where(first, pltpu.roll(x, LANES - 16, 1), pltpu.roll(x, 16, 1))
    return x * cos + partner * sin_signed


def _sink_column(sink_ref, layer, kh, t):
    row = lax.broadcasted_iota(jnp.int32, (KV_GROUP * t, 1), 0)
    col = jnp.full((KV_GROUP * t, 1), sink_ref[layer, kh * KV_GROUP], F32)
    for g in range(1, KV_GROUP):
        col = jnp.where(row >= g * t, sink_ref[layer, kh * KV_GROUP + g], col)
    return col


def _sink_softmax_av(heads):
    ms = []
    for scores, _, sink_col in heads:
        m = sink_col
        for s in scores:
            m = jnp.maximum(m, jnp.max(s, axis=-1, keepdims=True))
        ms.append(m)
    ps = [[jnp.exp(s - m) for s in scores] for (scores, _, _), m in zip(heads, ms)]
    outs = []
    for (scores, values, sink_col), m, p_list in zip(heads, ms, ps):
        den = jnp.exp(sink_col - m)
        o = None
        for p, v in zip(p_list, values):
            den = den + jnp.sum(p, axis=-1, keepdims=True)
            pv = _dot(p.astype(BF16), v.astype(BF16))
            o = pv if o is None else o + pv
        outs.append(o / den)
    return outs


def _attn_ctx_kernel(sink_ref, q_ref, kv_ref, gq_ref, gk_ref, att_ref, k_ref, v_ref, *, layer):
    t = q_ref.shape[0]
    kv = kv_ref[...]
    k = _pair_norm(kv[:, :KV_W], gk_ref[...])
    v = kv[:, KV_W:]
    k_ref[...] = k
    v_ref[...] = v
    q = q_ref[...]
    scale = 1.0 / math.sqrt(HEAD_DIM)
    work = []
    for kh in range(N_KV_HEADS):
        heads = []
        for j in range(KV_GROUP // 2):
            c0 = (kh * (KV_GROUP // 2) + j) * LANES
            qn = _pair_norm(q[:, c0:c0 + LANES], gq_ref[...])
            heads += [qn[:, :HEAD_DIM], qn[:, HEAD_DIM:]]
        qs = jnp.concatenate(heads, axis=0).astype(BF16)
        sl = slice(kh * HEAD_DIM, (kh + 1) * HEAD_DIM)
        s = _dot_nt(qs, k[:, sl].astype(BF16)) * scale
        work.append(([s], [v[:, sl]], _sink_column(sink_ref, layer, kh, t)))
    outs = [o[g * t:(g + 1) * t] for o in _sink_softmax_av(work) for g in range(KV_GROUP)]
    att_ref[...] = jnp.concatenate(outs, axis=-1).astype(BF16)


def _attn_ctx(sink, p, gq, gk, layer):
    n = p.shape[0]
    t = SEQ
    kern = functools.partial(_attn_ctx_kernel, layer=layer)
    return pl.pallas_call(
        kern,
        grid=(n // t,),
        in_specs=[
            pl.BlockSpec(memory_space=pltpu.SMEM),
            pl.BlockSpec((t, Q_W), lambda b: (b, COL_AQ // Q_W)),
            pl.BlockSpec((t, 2 * KV_W), lambda b: (b, COL_AK // (2 * KV_W))),
            pl.BlockSpec((None, 1, LANES), lambda b: (layer, 0, 0)),
            pl.BlockSpec((None, 1, LANES), lambda b: (layer, 0, 0)),
        ],
        out_specs=[
            pl.BlockSpec((t, Q_W), lambda b: (b, 0)),
            pl.BlockSpec((t, KV_W), lambda b: (b, 0)),
            pl.BlockSpec((t, KV_W), lambda b: (b, 0)),
        ],
        out_shape=[
            jax.ShapeDtypeStruct((n, Q_W), BF16),
            jax.ShapeDtypeStruct((n, KV_W), F32),
            jax.ShapeDtypeStruct((n, KV_W), F32),
        ],
        compiler_params=_cparams(("parallel",)),
        name="attn_ctx",
    )(sink, p, p, gq, gk)


def _attn_lat_kernel(sink_ref, q_ref, kv_ref, kc_ref, vc_ref, gq_ref, gk_ref, cos_ref, sin_ref, att_ref, *, layer):
    t = ATT_BLOCK
    span = ATT_BLOCK + 2 * WINDOW
    start = pl.program_id(1) * ATT_BLOCK
    ks = pl.multiple_of(jnp.clip(start - WINDOW, 0, DEC_SEQ - span), ATT_BLOCK)
    q0 = pl.multiple_of(start, ATT_BLOCK)
    kvw = kv_ref[pl.ds(ks, span), :]
    k = _rope(_pair_norm(kvw[:, :KV_W], gk_ref[...]), cos_ref[pl.ds(ks, span), :], sin_ref[pl.ds(ks, span), :])
    v = kvw[:, KV_W:]
    cos_q = cos_ref[pl.ds(q0, t), :]
    sin_q = sin_ref[pl.ds(q0, t), :]
    qpos = start + (lax.broadcasted_iota(jnp.int32, (KV_GROUP * t, span), 0) & (t - 1))
    kpos = ks + lax.broadcasted_iota(jnp.int32, (KV_GROUP * t, span), 1)
    band = jnp.abs(qpos - kpos) <= WINDOW
    q = q_ref[...]
    kc = kc_ref[...]
    vc = vc_ref[...]
    scale = 1.0 / math.sqrt(HEAD_DIM)
    work = []
    for kh in range(N_KV_HEADS):
        heads = []
        for j in range(KV_GROUP // 2):
            c0 = (kh * (KV_GROUP // 2) + j) * LANES
            qn = _rope(_pair_norm(q[:, c0:c0 + LANES], gq_ref[...]), cos_q, sin_q)
            heads += [qn[:, :HEAD_DIM], qn[:, HEAD_DIM:]]
        qs = jnp.concatenate(heads, axis=0).astype(BF16)
        sl = slice(kh * HEAD_DIM, (kh + 1) * HEAD_DIM)
        s_w = jnp.where(band, _dot_nt(qs, k[:, sl].astype(BF16)) * scale, MASK_NEG)
        s_c = _dot_nt(qs, kc[:, sl].astype(BF16)) * scale
        work.append(([s_w, s_c], [v[:, sl], vc[:, sl]], _sink_column(sink_ref, layer, kh, t)))
    outs = [o[g * t:(g + 1) * t] for o in _sink_softmax_av(work) for g in range(KV_GROUP)]
    att_ref[...] = jnp.concatenate(outs, axis=-1).astype(BF16)


def _attn_lat(sink, p, cache_k, cache_v, gq, gk, cos_t, sin_t, layer):
    n = p.shape[0]
    t = ATT_BLOCK
    nb = DEC_SEQ // t
    kern = functools.partial(_attn_lat_kernel, layer=layer)
    return pl.pallas_call(
        kern,
        grid=(DEC_BATCH, nb),
        in_specs=[
            pl.BlockSpec(memory_space=pltpu.SMEM),
            pl.BlockSpec((t, Q_W), lambda b, j: (b * nb + j, COL_AQ // Q_W)),
            pl.BlockSpec((DEC_SEQ, 2 * KV_W), lambda b, j: (b, COL_AK // (2 * KV_W))),
            pl.BlockSpec((None, None, PAST_LEN, KV_W), lambda b, j: (b, layer, 0, 0)),
            pl.BlockSpec((None, None, PAST_LEN, KV_W), lambda b, j: (b, layer, 0, 0)),
            pl.BlockSpec((None, 1, LANES), lambda b, j: (layer, 0, 0)),
            pl.BlockSpec((None, 1, LANES), lambda b, j: (layer, 0, 0)),
            pl.BlockSpec((DEC_SEQ, LANES), lambda b, j: (0, 0)),
            pl.BlockSpec((DEC_SEQ, LANES), lambda b, j: (0, 0)),
        ],
        out_specs=pl.BlockSpec((t, Q_W), lambda b, j: (b * nb + j, 0)),
        out_shape=jax.ShapeDtypeStruct((n, Q_W), BF16),
        compiler_params=_cparams(("parallel", "parallel")),
        name="attn_lat",
    )(sink, p, p, cache_k, cache_v, gq, gk, cos_t, sin_t)


def _rope_tables():
    pos = jnp.arange(DEC_SEQ)
    rows = (pos // GRID_W).astype(F32)
    cols = (pos % GRID_W).astype(F32)
    half = HEAD_DIM // 2
    freqs = ROPE_BASE ** (-jnp.arange(0, half, 2, dtype=F32) / half)
    ang_r = rows[:, None] * freqs[None, :]
    ang_c = cols[:, None] * freqs[None, :]
    ang = jnp.concatenate([ang_r, ang_r, ang_c, ang_c], axis=-1)
    sign = jnp.tile(jnp.concatenate([-jnp.ones((16,), F32), jnp.ones((16,), F32)]), 2)
    cos_t = jnp.tile(jnp.cos(ang), (1, LANES // HEAD_DIM))
    sin_t = jnp.tile(jnp.sin(ang) * sign[None, :], (1, LANES // HEAD_DIM))
    return cos_t, sin_t


def _ssm_kernel(*refs, nblk, has_h0):
    if has_h0:
        x_ref, w_ref, bst_ref, cst_ref, lamt_ref, d_ref, h0_ref, z_ref, loc_scr, ent_scr = refs
    else:
        x_ref, w_ref, bst_ref, cst_ref, lamt_ref, d_ref, z_ref, fin_ref, loc_scr, ent_scr = refs
    ns = SSM_STATE
    t_blk = SSM_T
    ng = LANES // SSM_GROUP
    rows = nblk * SUBLANES
    per_half = LANES // SSM_GROUP
    x = x_ref[...].reshape(rows, t_blk * LANES)
    cols = [x[:, t * LANES:(t + 1) * LANES] for t in range(t_blk)]
    chunk = lax.broadcasted_iota(jnp.int32, (rows, LANES), 1) // SSM_GROUP
    is_fwd = lax.broadcasted_iota(jnp.int32, (SUBLANES, 2 * ns), 1) < ns

    def roll(a, chunks):
        shift = (chunks * SSM_GROUP) % LANES
        return a if shift == 0 else pltpu.roll(a, shift, 1)

    xg = []
    for g in range(ng):
        halves = []
        for h in range(t_blk // per_half):
            acc = None
            for j in range(per_half):
                r = roll(cols[h * per_half + j], j - g)
                acc = r if acc is None else jnp.where(chunk == j, r, acc)
            halves.append(acc)
        xg.append(jnp.concatenate(halves, axis=1))
        loc_scr[g] = _dot(xg[g].astype(BF16), bst_ref[g])
    l_re = [jnp.broadcast_to(lamt_ref[g, 0:1, :], (SUBLANES, 2 * ns)) for g in range(ng)]
    l_im = [jnp.broadcast_to(lamt_ref[g, 1:2, :], (SUBLANES, 2 * ns)) for g in range(ng)]
    if has_h0:
        s_re = [h0_ref[g, 0] for g in range(ng)]
        s_im = [h0_ref[g, 1] for g in range(ng)]
    else:
        s_re = [jnp.zeros((SUBLANES, 2 * ns), F32) for _ in range(ng)]
        s_im = [jnp.zeros((SUBLANES, 2 * ns), F32) for _ in range(ng)]
    for j in range(nblk):
        rf = slice(j * SUBLANES, (j + 1) * SUBLANES)
        rb = slice((nblk - 1 - j) * SUBLANES, (nblk - j) * SUBLANES)
        for g in range(ng):
            ent_scr[g, rf, 0:ns] = s_re[g][:, 0:ns]
            ent_scr[g, rb, ns:2 * ns] = s_re[g][:, ns:2 * ns]
            ent_scr[g, rf, 2 * ns:3 * ns] = s_im[g][:, 0:ns]
            ent_scr[g, rb, 3 * ns:4 * ns] = s_im[g][:, ns:2 * ns]
            loc_re = jnp.where(is_fwd, loc_scr[g, rf, 0:2 * ns], loc_scr[g, rb, 0:2 * ns])
            loc_im = jnp.where(is_fwd, loc_scr[g, rf, 2 * ns:4 * ns], loc_scr[g, rb, 2 * ns:4 * ns])
            n_re = l_re[g] * s_re[g] - l_im[g] * s_im[g] + loc_re
            n_im = l_re[g] * s_im[g] + l_im[g] * s_re[g] + loc_im
            s_re[g], s_im[g] = n_re, n_im
    if not has_h0:
        for g in range(ng):
            fin_ref[g, 0] = s_re[g]
            fin_ref[g, 1] = s_im[g]
    out_cols = [None] * t_blk
    for g in range(ng):
        y = (_dot(xg[g].astype(BF16), w_ref[g]) + _dot_nt(ent_scr[g].astype(BF16), cst_ref[g])
             + d_ref[g] * xg[g])
        zg = _gelu_tanh(y)
        for h in range(t_blk // per_half):
            zh = zg[:, h * LANES:(h + 1) * LANES]
            for j in range(per_half):
                t = h * per_half + j
                r = roll(zh, g - j)
                out_cols[t] = r if out_cols[t] is None else jnp.where(chunk == g, r, out_cols[t])
    z_ref[...] = jnp.concatenate(out_cols, axis=1).astype(BF16).reshape(nblk, SUBLANES, t_blk * LANES)


def _ssm(p, sw, layer, nseq, seq_len, h0=None):
    n = p.shape[0]
    t_blk = SSM_T
    nblk = seq_len // t_blk
    bw = t_blk * SSM_GROUP
    nsg = SSM_WIDTH // LANES
    ng = LANES // SSM_GROUP
    npad = -(-nseq // SUBLANES) * SUBLANES
    has_h0 = h0 is not None
    su = p[:, COL_SU:COL_SU + SSM_WIDTH].reshape(nseq, nblk, t_blk, nsg, LANES)
    x = su.transpose(1, 0, 3, 2, 4).reshape(nblk, nseq, nsg * t_blk * LANES)
    if npad != nseq:
        x = jnp.pad(x, ((0, 0), (0, npad - nseq), (0, 0)))
    kern = functools.partial(_ssm_kernel, nblk=nblk, has_h0=has_h0)
    wspec = lambda a, b: pl.BlockSpec((None, None, ng, a, b), lambda s, g: (layer, g, 0, 0, 0))
    grp = lambda a: a.reshape((DEPTH, nsg, ng) + a.shape[2:])
    xspec = pl.BlockSpec((nblk, SUBLANES, t_blk * LANES), lambda s, g: (0, s, g))
    in_specs = [xspec, wspec(bw, bw), wspec(bw, 4 * SSM_STATE), wspec(bw, 4 * SSM_STATE),
                wspec(2, 2 * SSM_STATE), wspec(1, bw)]
    args = [x, grp(sw['w']), grp(sw['bst']), grp(sw['cst']), grp(sw['lamt']), grp(sw['d'])]
    out_specs = [xspec]
    out_shape = [jax.ShapeDtypeStruct((nblk, npad, nsg * t_blk * LANES), BF16)]
    st_block = (ng, 2, SUBLANES, 2 * SSM_STATE)
    if has_h0:
        in_specs.append(pl.BlockSpec((None, None) + st_block, lambda s, g: (layer, g, 0, 0, 0, 0)))
        args.append(h0)
    else:
        out_specs.append(pl.BlockSpec((None, None) + st_block, lambda s, g: (s, g, 0, 0, 0, 0)))
        out_shape.append(jax.ShapeDtypeStruct((npad // SUBLANES, nsg) + st_block, F32))
    rows = nblk * SUBLANES
    outs = pl.pallas_call(
        kern,
        grid=(npad // SUBLANES, nsg),
        in_specs=in_specs,
        out_specs=out_specs,
        out_shape=out_shape,
        scratch_shapes=[pltpu.VMEM((ng, rows, 4 * SSM_STATE), F32), pltpu.VMEM((ng, rows, 4 * SSM_STATE), F32)],
        compiler_params=_cparams(("parallel", "parallel")),
        name="ssm",
    )(*args)
    z = outs[0][:, :nseq].reshape(nblk, nseq, nsg, t_blk, LANES).transpose(1, 0, 3, 2, 4).reshape(n, SSM_WIDTH)
    return (z,) + tuple(outs[1:])


def _hgrn_consts():
    c = HG_CHUNK
    i = np.arange(c)[:, None]
    r = np.arange(c)[None, :]
    m_all = np.zeros((2, HG_LEVELS, c, c), np.float32)
    lev_of = np.full((2, c, c), -1, np.int32)
    for lev in range(HG_LEVELS):
        n = 2 << lev
        half = n // 2
        mid = (i // n) * n + half
        upper = (i % n) >= half
        m_all[0, lev] = np.where(upper, (r >= mid) & (r <= i), (r > i) & (r < mid))
        m_all[1, lev] = np.where(upper, (r >= mid) & (r < i), (r >= i) & (r < mid))
        same = (i // n) == (r // n)
        r_upper = (r % n) >= half
        lev_of[0][same & upper & ~r_upper] = lev
        lev_of[1][same & ~upper & r_upper] = lev
    cum = np.stack([r <= i, r >= i]).astype(np.float32)
    m_small = np.concatenate([cum] + [m_all[:, lev] for lev in HG_MXU_LEVELS], axis=1)
    return (jnp.asarray(np.concatenate([m_small, m_small], axis=-1), BF16), jnp.asarray(lev_of))


def _hgrn_chunks(chains, states, m_ref, lev_ref):
    c = HG_CHUNK
    nb = c // SUBLANES
    w = []
    for d, key, r0, hq_ref, f_ref, hi_ref, lb, cum_ref in chains:
        q = _silu(hq_ref[pl.ds(r0, c), :])
        fg = lb + (1.0 - lb) * _sigmoid(f_ref[pl.ds(r0, c), :])
        k = 1.0 - fg
        log2f = jnp.log(fg) * (1.0 / math.log(2.0))
        v = hi_ref[pl.ds(r0, c), :]
        hi = log2f.astype(BF16)
        lo = (log2f - hi.astype(F32)).astype(BF16)
        sums = _dot(m_ref[d], jnp.concatenate([hi, lo], axis=0))
        w.append(dict(d=d, key=key, q=q, fg=fg, k=k, v=v, vb=v.astype(BF16), sums=sums, cum=sums[0:c],
                      cum_ref=cum_ref, att=[None] * nb))
    for x in w:
        x['cum_ref'][...] = x['cum']

    def select_into(x, rows, lev, a):
        att = x['att']
        for j in range(rows.stop // SUBLANES - rows.start // SUBLANES):
            i = rows.start // SUBLANES + j
            blk = a[j * SUBLANES:(j + 1) * SUBLANES]
            mask = lev_ref[x['d'], i * SUBLANES:(i + 1) * SUBLANES, :] == lev
            att[i] = jnp.where(mask, blk, 0.0 if att[i] is None else att[i])

    full = slice(0, c)
    prods = [_dot_nt((x['q'] * x['fg']).astype(BF16), x['k'].astype(BF16)) for x in w]
    for x, a in zip(w, prods):
        select_into(x, full, 0, a)
    for n_lev, lev in enumerate(HG_MXU_LEVELS):
        prods = []
        for x in w:
            g = jnp.exp2(x['sums'][(n_lev + 1) * c:(n_lev + 2) * c])
            prods.append(_dot_nt((x['q'] * g).astype(BF16), (x['k'] * g).astype(BF16)))
        for x, a in zip(w, prods):
            select_into(x, full, lev, a)
    for lev in range(HG_MXU_LEVELS[-1] + 1, HG_LEVELS):
        half = 1 << lev
        n = 2 * half
        prods = []
        for x in w:
            d, q, k, cum = x['d'], x['q'], x['k'], x['cum']
            q_parts, k_parts, q_rows = [], [], []
            for j in range(c // n):
                lower = slice(j * n, j * n + half)
                upper = slice(j * n + half, (j + 1) * n)
                qr, kr = (upper, lower) if d == 0 else (lower, upper)
                ref_row = j * n + half - 1 if d == 0 else j * n + half
                ref = jnp.broadcast_to(x['cum_ref'][ref_row:ref_row + 1, :], (half, HG_KDIM))
                q_parts.append(q[qr] * jnp.exp2(cum[qr] - ref))
                k_scaled = k[kr] * jnp.exp2(ref - cum[kr])
                k_parts += [k_scaled, k[qr]] if d == 0 else [k[qr], k_scaled]
                q_rows.append(qr)
            prods.append((q_rows, _dot_nt(jnp.concatenate(q_parts, axis=0).astype(BF16),
                                          jnp.concatenate(k_parts, axis=0).astype(BF16))))
        for x, (q_rows, a) in zip(w, prods):
            for j, qr in enumerate(q_rows):
                select_into(x, qr, lev, a[j * half:(j + 1) * half])
    for x in w:
        d, q, k, cum = x['d'], x['q'], x['k'], x['cum']
        last = c - 1 if d == 0 else 0
        tot_row = cum[last:last + 1, :]
        x['lhs'] = jnp.concatenate([jnp.concatenate(x['att'], axis=0).astype(BF16),
                                    (q * jnp.exp2(cum)).astype(BF16)], axis=1)
        x['kg'] = (k * jnp.exp2(tot_row - cum)).astype(BF16)
        x['dv'] = jnp.sum(q * k, axis=-1, keepdims=True) * x['v']
        x['tot_col'] = jnp.broadcast_to(jnp.exp2(tot_row), (c, HG_KDIM)).T
    states = list(states)
    outs = []
    for x in w:
        st = states[x['key']]
        outs.append(x['dv'] + _dot(x['lhs'], jnp.concatenate([x['vb'], st.astype(BF16)], axis=0)))
        states[x['key']] = st * x['tot_col'] + _dot_tn(x['kg'], x['vb'])
    return outs, states


def _hgrn_kernel(*refs, seq_len, nh, unroll, has_s0):
    if has_s0:
        (hq_ref, ff_ref, fb_ref, hi_ref, hgt_ref, lb_ref, ng_ref, m_ref, lev_ref, s0_ref,
         o_ref, of_scr, ob_scr, cum_scr) = refs
    else:
        (hq_ref, ff_ref, fb_ref, hi_ref, hgt_ref, lb_ref, ng_ref, m_ref, lev_ref,
         o_ref, sfin_ref, of_scr, ob_scr, cum_scr) = refs
    c = HG_CHUNK
    nch = seq_len // c
    head = lambda ref, h: ref.at[:, h * HG_KDIM:(h + 1) * HG_KDIM]

    def step(ci, carry):
        chains, rows = [], []
        for u in range(unroll):
            cf = ci * unroll + u
            rf = pl.multiple_of(cf * c, c)
            rb = pl.multiple_of((nch - 1 - cf) * c, c)
            for h in range(nh):
                hq, hi = head(hq_ref, h), head(hi_ref, h)
                slot = (u * nh + h) * 2
                chains.append((0, 2 * h, rf, hq, head(ff_ref, h), hi, lb_ref[h, 0:1, :], cum_scr.at[slot]))
                chains.append((1, 2 * h + 1, rb, hq, head(fb_ref, h), hi, lb_ref[h, 1:2, :], cum_scr.at[slot + 1]))
                rows += [(head(of_scr, h), rf), (head(ob_scr, h), rb)]
        outs, states = _hgrn_chunks(chains, carry, m_ref, lev_ref)
        for (scr, r0), o in zip(rows, outs):
            scr[pl.ds(r0, c), :] = o
        return tuple(states)

    if has_s0:
        init = tuple(s0_ref[d, h] for h in range(nh) for d in range(2))
    else:
        init = tuple(jnp.zeros((HG_KDIM, HG_VDIM), F32) for _ in range(2 * nh))
    if nch == unroll:
        fin = step(0, init)
    else:
        fin = lax.fori_loop(0, nch // unroll, step, init)
    if not has_s0:
        for h in range(nh):
            for d in range(2):
                sfin_ref[d, h] = fin[2 * h + d]

    for h in range(nh):
        o = head(of_scr, h)[...] + head(ob_scr, h)[...]
        ms = jnp.mean(o * o, axis=-1, keepdims=True)
        y = o * lax.rsqrt(ms + EPS) * ng_ref[...]
        head(o_ref, h)[...] = (y * _silu(head(hgt_ref, h)[...])).astype(BF16)


def _hgrn(p, lb, norm_g, m_all, lev_of, layer, seq_len, s0=None):
    n = p.shape[0]
    nseq = n // seq_len
    has_s0 = s0 is not None
    nch = seq_len // HG_CHUNK
    unroll = min(nch, HG_CHAINS // 2)
    nh = HG_CHAINS // (2 * unroll)
    kern = functools.partial(_hgrn_kernel, seq_len=seq_len, nh=nh, unroll=unroll, has_s0=has_s0)
    base = COL_HG // (nh * LANES)
    per = HG_HEADS // nh

    def col(k):
        return pl.BlockSpec((seq_len, nh * LANES), lambda b, h, k=k: (b, base + k * per + h))

    c = HG_CHUNK
    in_specs = [col(0), col(1), col(2), col(3), col(4),
                pl.BlockSpec((None, nh, 2, HG_KDIM), lambda b, h: (layer, h, 0, 0)),
                pl.BlockSpec((None, 1, HG_VDIM), lambda b, h: (layer, 0, 0)),
                pl.BlockSpec((2, (len(HG_MXU_LEVELS) + 1) * c, 2 * c), lambda b, h: (0, 0, 0)),
                pl.BlockSpec((2, c, c), lambda b, h: (0, 0, 0))]
    args = [p, p, p, p, p, lb, norm_g, m_all, lev_of]
    out_specs = [pl.BlockSpec((seq_len, nh * HG_VDIM), lambda b, h: (b, h))]
    out_shape = [jax.ShapeDtypeStruct((n, HG_VW), BF16)]
    if has_s0:
        in_specs.append(pl.BlockSpec((None, None, 2, nh, HG_KDIM, HG_VDIM), lambda b, h: (b, layer, 0, h, 0, 0)))
        args.append(s0)
    else:
        out_specs.append(pl.BlockSpec((None, 2, nh, HG_KDIM, HG_VDIM), lambda b, h: (b, 0, h, 0, 0)))
        out_shape.append(jax.ShapeDtypeStruct((nseq, 2, HG_HEADS, HG_KDIM, HG_VDIM), F32))
    return pl.pallas_call(
        kern,
        grid=(nseq, per),
        in_specs=in_specs,
        out_specs=out_specs,
        out_shape=out_shape,
        scratch_shapes=[pltpu.VMEM((seq_len, nh * HG_VDIM), F32), pltpu.VMEM((seq_len, nh * HG_VDIM), F32),
                        pltpu.VMEM((HG_CHAINS, HG_CHUNK, HG_KDIM), F32)],
        compiler_params=_cparams(("parallel", "parallel")),
        name="hgrn",
    )(*args)


def _router(logits):
    lane = lax.broadcasted_iota(jnp.int32, logits.shape, 1)
    big = jnp.int32(LANES)
    is_g = lane < N_GROUPS
    gl = jnp.where(is_g, logits, -jnp.inf)
    gmax = jnp.max(gl, axis=-1, keepdims=True)
    gsum = jnp.sum(jnp.exp(gl - gmax), axis=-1, keepdims=True)
    g_idx = jnp.min(jnp.where(is_g & (gl == gmax), lane, big), axis=-1, keepdims=True)
    g_prob = 1.0 / gsum
    e_lane = lane - ROUTER_OFF
    sel = (e_lane >= 0) & (e_lane < N_EXPERTS) & ((e_lane >> 2) == g_idx)
    el = jnp.where(sel, logits, -jnp.inf)
    emax = jnp.max(el, axis=-1, keepdims=True)
    eexp = jnp.exp(el - emax)
    ep = eexp / jnp.sum(eexp, axis=-1, keepdims=True)
    p1 = jnp.max(ep, axis=-1, keepdims=True)
    i1 = jnp.min(jnp.where(sel & (ep == p1), lane, big), axis=-1, keepdims=True)
    rest = sel & (lane != i1)
    ep2 = jnp.where(rest, ep, -1.0)
    p2 = jnp.max(ep2, axis=-1, keepdims=True)
    i2 = jnp.min(jnp.where(rest & (ep2 == p2), lane, big), axis=-1, keepdims=True)
    den = p1 + p2
    w1 = g_prob * (p1 / den)
    w2 = g_prob * (p2 / den)
    return jnp.where(lane == i1, w1, jnp.where(lane == i2, w2, 0.0))


def _merge_kernel(x_ref, att_ref, z_ref, hg_ref, mg_ref, mod_ref, n2_ref,
                  wao_ref, wga_ref, wgb_ref, who_ref, wout_ref, wr_ref, br_ref,
                  xm_ref, h2_ref, comb_ref):
    y_att = _dot(att_ref[...], wao_ref[...])
    z = z_ref[...]
    y_ssm = _dot(z, wga_ref[...]) * _sigmoid(_dot(z, wgb_ref[...]))
    y_hg = _dot(hg_ref[...], who_ref[...])
    merged = (_sigmoid(mg_ref[:, 0:D_MODEL]) * y_att
              + _sigmoid(mg_ref[:, D_MODEL:2 * D_MODEL]) * y_ssm
              + _sigmoid(mg_ref[:, 2 * D_MODEL:3 * D_MODEL]) * y_hg)
    xm = x_ref[...] + mod_ref[2:3, :] * _dot(merged.astype(BF16), wout_ref[...])
    xm_ref[...] = xm
    ms = jnp.mean(xm * xm, axis=-1, keepdims=True)
    h2 = xm * lax.rsqrt(ms + EPS) * n2_ref[...] * (1.0 + mod_ref[4:5, :]) + mod_ref[3:4, :]
    h2_hi = h2.astype(BF16)
    h2_ref[...] = h2_hi
    h2_lo = (h2 - h2_hi.astype(F32)).astype(BF16)
    logits = _dot(jnp.concatenate([h2_hi, h2_lo, h2_hi], axis=1), wr_ref[...]) + br_ref[...]
    comb_ref[...] = _router(logits)


def _merge(x, att, z, hg, p, mod, n2g, w, layer, seq_len, cond_row0):
    n = x.shape[0]
    tm = 256
    rows_per_cond = seq_len if cond_row0 else n
    mod_idx = (lambda i: (layer, cond_row0 + (i * tm) // rows_per_cond, 0, 0))
    row = lambda width: pl.BlockSpec((tm, width), lambda i: (i, 0))
    wspec = lambda a, b: pl.BlockSpec((None, a, b), lambda i: (layer, 0, 0))
    return pl.pallas_call(
        _merge_kernel,
        grid=(n // tm,),
        in_specs=[
            row(D_MODEL), row(Q_W), row(SSM_WIDTH), row(HG_VW),
            pl.BlockSpec((tm, MG_W), lambda i: (i, COL_MG // MG_W)),
            pl.BlockSpec((None, None, 6, D_MODEL), mod_idx),
            wspec(1, D_MODEL),
            wspec(Q_W, D_MODEL), wspec(SSM_WIDTH, D_MODEL), wspec(SSM_WIDTH, D_MODEL),
            wspec(HG_VW, D_MODEL), wspec(D_MODEL, D_MODEL), wspec(3 * D_MODEL, LANES), wspec(1, LANES),
        ],
        out_specs=[row(D_MODEL), row(D_MODEL), row(LANES)],
        out_shape=[
            jax.ShapeDtypeStruct((n, D_MODEL), F32),
            jax.ShapeDtypeStruct((n, D_MODEL), BF16),
            jax.ShapeDtypeStruct((n, LANES), F32),
        ],
        compiler_params=_cparams(("parallel",)),
        name="merge",
    )(x, att, z, hg, p, mod, n2g, w['w_attn_o'], w['w_glu_a'], w['w_glu_b'], w['w_hg_o'], w['w_out'],
      w['w_route'], w['b_route'])


def _moe_kernel(h_ref, comb_ref, xm_ref, mod_ref, wg_ref, wu_ref, wd_ref, o_ref, acc_scr):
    grp = pl.program_id(1)
    comb = comb_ref[...]
    lane = lax.broadcasted_iota(jnp.int32, comb.shape, 1)
    h = h_ref[...]
    hg = _dot(h, wg_ref[...])
    hu = _dot(h, wu_ref[...])
    parts = []
    for e in range(EXPERTS_PER_GROUP):
        cw = jnp.sum(jnp.where(lane == ROUTER_OFF + grp * EXPERTS_PER_GROUP + e, comb, 0.0),
                     axis=-1, keepdims=True)
        sl = slice(e * EXPERT_FF, (e + 1) * EXPERT_FF)
        parts.append((_silu(hg[:, sl]) * hu[:, sl] * cw).astype(BF16))
    y = _dot(jnp.concatenate(parts, axis=1), wd_ref[...])

    @pl.when(grp == 0)
    def _():
        acc_scr[...] = y

    @pl.when(grp > 0)
    def _():
        acc_scr[...] += y

    @pl.when(grp == N_GROUPS - 1)
    def _():
        o_ref[...] = xm_ref[...] + mod_ref[5:6, :] * acc_scr[...]


def _moe(h2, comb, xm, mod, w, layer, seq_len, cond_row0):
    n = h2.shape[0]
    tm = 512
    gw = EXPERTS_PER_GROUP * EXPERT_FF
    rows_per_cond = seq_len if cond_row0 else n
    mod_idx = (lambda i, g: (layer, cond_row0 + (i * tm) // rows_per_cond, 0, 0))
    return pl.pallas_call(
        _moe_kernel,
        grid=(n // tm, N_GROUPS),
        in_specs=[
            pl.BlockSpec((tm, D_MODEL), lambda i, g: (i, 0)),
            pl.BlockSpec((tm, LANES), lambda i, g: (i, 0)),
            pl.BlockSpec((tm, D_MODEL), lambda i, g: (i, 0)),
            pl.BlockSpec((None, None, 6, D_MODEL), mod_idx),
            pl.BlockSpec((None, D_MODEL, gw), lambda i, g: (layer, 0, g)),
            pl.BlockSpec((None, D_MODEL, gw), lambda i, g: (layer, 0, g)),
            pl.BlockSpec((None, gw, D_MODEL), lambda i, g: (layer, g, 0)),
        ],
        out_specs=pl.BlockSpec((tm, D_MODEL), lambda i, g: (i, 0)),
        out_shape=jax.ShapeDtypeStruct((n, D_MODEL), F32),
        scratch_shapes=[pltpu.VMEM((tm, D_MODEL), F32)],
        compiler_params=_cparams(("parallel", "arbitrary")),
        name="moe",
    )(h2, comb, xm, mod, w['w_e_gate'], w['w_e_up'], w['w_e_down'])


def _layer(x, layer, w, seq_len, cond_row0, cache):
    p = _in_proj(x, w['mod'], w['norm1_g'], w['w_in'], layer, seq_len, cond_row0)
    nseq = x.shape[0] // seq_len
    if cache is None:
        att, k_new, v_new = _attn_ctx(w['attn_sink'], p, w['q_norm_g'], w['k_norm_g'], layer)
        z, ssm_fin = _ssm(p, w['ssm'], layer, nseq, seq_len)
        hg, hg_fin = _hgrn(p, w['hg_lb'], w['hg_norm_g'], w['hg_m'], w['hg_lev'], layer, seq_len)
        ctx = (k_new, v_new, ssm_fin, hg_fin)
    else:
        cache_k, cache_v, h0, s0 = cache
        att = _attn_lat(w['attn_sink'], p, cache_k, cache_v, w['q_norm_g'], w['k_norm_g'],
                        w['rope_cos'], w['rope_sin'], layer)
        (z,) = _ssm(p, w['ssm'], layer, nseq, seq_len, h0=h0)
        (hg,) = _hgrn(p, w['hg_lb'], w['hg_norm_g'], w['hg_m'], w['hg_lev'], layer, seq_len, s0=s0)
        ctx = None
    xm, h2, comb = _merge(x, att, z, hg, p, w['mod'], w['norm2_g'], w, layer, seq_len, cond_row0)
    x = _moe(h2, comb, xm, w['mod'], w, layer, seq_len, cond_row0)
    return x, ctx


def kernel(x_prompt, x_sample, cache_k, cache_v, state_ssm_re, state_ssm_im, state_hgrn, c, c_ctx, w_mod, b_mod, norm1_g, norm2_g, w_in, q_norm_g, k_norm_g, attn_sink, w_attn_o, ssm_a_re, ssm_a_im, ssm_log_dt, ssm_b_re, ssm_b_im, ssm_c_re, ssm_c_im, ssm_d, w_glu_a, w_glu_b, hg_lb, hg_norm_g, w_hg_o, w_out, w_group, b_group, w_router, b_router, w_e_gate, w_e_up, w_e_down):
    w = {}
    w['mod'] = _modulation(c, c_ctx, w_mod, b_mod)
    w['norm1_g'] = norm1_g.reshape(DEPTH, 1, D_MODEL)
    w['norm2_g'] = norm2_g.reshape(DEPTH, 1, D_MODEL)
    w['w_in'] = jnp.concatenate(
        [w_in[:, :, 3840:], w_in[:, :, 1280:3840], w_in[:, :, 0:768], w_in[:, :, 768:1280]], axis=-1).astype(BF16)
    w['q_norm_g'] = jnp.tile(q_norm_g, (1, LANES // HEAD_DIM)).reshape(DEPTH, 1, LANES)
    w['k_norm_g'] = jnp.tile(k_norm_g, (1, LANES // HEAD_DIM)).reshape(DEPTH, 1, LANES)
    w['attn_sink'] = attn_sink
    w['rope_cos'], w['rope_sin'] = _rope_tables()
    w['ssm'] = _ssm_prep(ssm_a_re, ssm_a_im, ssm_log_dt, ssm_b_re, ssm_b_im, ssm_c_re, ssm_c_im, ssm_d)
    w['hg_lb'] = _lower_bounds(hg_lb)
    w['hg_norm_g'] = hg_norm_g.reshape(DEPTH, 1, HG_VDIM)
    w['hg_m'], w['hg_lev'] = _hgrn_consts()
    for name, val in (('w_attn_o', w_attn_o), ('w_glu_a', w_glu_a), ('w_glu_b', w_glu_b), ('w_hg_o', w_hg_o),
                      ('w_out', w_out)):
        w[name] = val.astype(BF16)
    ff_all = N_EXPERTS * EXPERT_FF
    w['w_e_gate'] = w_e_gate.astype(BF16).transpose(0, 2, 1, 3).reshape(DEPTH, D_MODEL, ff_all)
    w['w_e_up'] = w_e_up.astype(BF16).transpose(0, 2, 1, 3).reshape(DEPTH, D_MODEL, ff_all)
    w['w_e_down'] = w_e_down.astype(BF16).reshape(DEPTH, ff_all, D_MODEL)
    pad = LANES - N_GROUPS - N_EXPERTS
    w_route = jnp.concatenate([w_group, w_router, jnp.zeros((DEPTH, D_MODEL, pad), F32)], axis=-1)
    w_route_hi = w_route.astype(BF16)
    w_route_lo = (w_route - w_route_hi.astype(F32)).astype(BF16)
    w['w_route'] = jnp.concatenate([w_route_hi, w_route_hi, w_route_lo], axis=1)
    w['b_route'] = jnp.concatenate([b_group, b_router, jnp.zeros((DEPTH, pad), F32)], axis=-1).reshape(DEPTH, 1, LANES)

    ck = cache_k.reshape(DEC_BATCH, DEPTH, PAST_LEN, KV_W)
    cv = cache_v.reshape(DEC_BATCH, DEPTH, PAST_LEN, KV_W)

    ng = LANES // SSM_GROUP
    h0 = jnp.stack([state_ssm_re, state_ssm_im]).transpose(2, 4, 0, 1, 3, 5).reshape(
        DEPTH, SSM_GROUPS // ng, ng, 2, DEC_BATCH, 2 * SSM_STATE)
    h0 = jnp.pad(h0, ((0, 0),) * 4 + ((0, SUBLANES - DEC_BATCH), (0, 0)))

    xp = x_prompt.reshape(BATCH * SEQ, D_MODEL)
    xs = x_sample.reshape(DEC_BATCH * DEC_SEQ, D_MODEL)
    new_k, new_v, new_re, new_im, new_hg = [], [], [], [], []
    for l in range(DEPTH):
        xp, ctx = _layer(xp, l, w, SEQ, 0, None)
        new_k.append(ctx[0].reshape(BATCH, SEQ, N_KV_HEADS, HEAD_DIM))
        new_v.append(ctx[1].reshape(BATCH, SEQ, N_KV_HEADS, HEAD_DIM))
        fin = ctx[2].reshape(BATCH // SUBLANES, SSM_GROUPS // ng, ng, 2, SUBLANES, 2, SSM_STATE)
        fin = fin.transpose(3, 0, 4, 5, 1, 2, 6).reshape(2, BATCH, 2, SSM_GROUPS, SSM_STATE)
        new_re.append(fin[0])
        new_im.append(fin[1])
        new_hg.append(ctx[3])
        xs, _ = _layer(xs, l, w, DEC_SEQ, 1, (ck, cv, h0, state_hgrn))
    return (xp.reshape(BATCH, SEQ, D_MODEL), xs.reshape(DEC_BATCH, DEC_SEQ, D_MODEL),
            jnp.stack(new_k, axis=1), jnp.stack(new_v, axis=1), jnp.stack(new_re, axis=1),
            jnp.stack(new_im, axis=1), jnp.stack(new_hg, axis=1))
```

```python
import functools
import math

import numpy as np
import jax
import jax.numpy as jnp
from jax import lax
from jax.experimental import pallas as pl
from jax.experimental.pallas import tpu as pltpu

F32 = jnp.float32
BF16 = jnp.bfloat16

D_MODEL = 1024
BATCH = 32
SEQ = 256
DEPTH = 4
DEC_BATCH = 4
DEC_SEQ = 1024
PAST_LEN = 512
GRID_W = 64
N_HEADS = 8
N_KV_HEADS = 2
HEAD_DIM = 64
KV_GROUP = N_HEADS // N_KV_HEADS
WINDOW = 128
ATT_BLOCK = 128
ROPE_BASE = 10000.0
SSM_WIDTH = 512
SSM_GROUP = 16
SSM_GROUPS = SSM_WIDTH // SSM_GROUP
SSM_STATE = 64
HG_HEADS = 4
HG_KDIM = 128
HG_VDIM = 128
N_BRANCHES = 3
N_GROUPS = 4
EXPERTS_PER_GROUP = 4
N_EXPERTS = N_GROUPS * EXPERTS_PER_GROUP
EXPERT_FF = 256
EPS = 1e-6
Q_W = N_HEADS * HEAD_DIM
KV_W = N_KV_HEADS * HEAD_DIM
HG_KW = HG_HEADS * HG_KDIM
HG_VW = HG_HEADS * HG_VDIM

LANES = 128
SUBLANES = 8
VMEM_LIMIT = 56 * 1024 * 1024

MG_W = N_BRANCHES * D_MODEL
COL_MG = 0
COL_HG = COL_MG + MG_W
COL_AQ = COL_HG + 3 * HG_KW + 2 * HG_VW
COL_AK = COL_AQ + Q_W
COL_SU = COL_AK + 2 * KV_W
IN_WIDTH = COL_SU + SSM_WIDTH

SSM_T = 16
HG_CHUNK = 128
HG_LEVELS = 7
HG_CHAINS = 8
HG_MXU_LEVELS = (1, 2)
ROUTER_OFF = N_GROUPS
MASK_NEG = -1e30


def _cparams(sem):
    return pltpu.CompilerParams(dimension_semantics=sem, vmem_limit_bytes=VMEM_LIMIT)


def _sigmoid(x):
    return 1.0 / (1.0 + jnp.exp(-x))


def _silu(x):
    return x * _sigmoid(x)


def _gelu_tanh(x):
    return 0.5 * x * (1.0 + jnp.tanh(math.sqrt(2.0 / math.pi) * (x + 0.044715 * (x * x * x))))


def _dot(a, b):
    return jnp.dot(a, b, preferred_element_type=F32)


def _dot_nt(a, b):
    return lax.dot_general(a, b, (((1,), (1,)), ((), ())), preferred_element_type=F32)


def _dot_tn(a, b):
    return lax.dot_general(a, b, (((0,), (0,)), ((), ())), preferred_element_type=F32)


def _shift_lanes(a, s):
    if s == 0:
        return a
    lo, hi = a[:, :LANES], a[:, LANES:]
    zero = jnp.zeros_like(lo)
    lane = lax.broadcasted_iota(jnp.int32, lo.shape, 1)
    if s > 0:
        if s >= LANES:
            t = s - LANES
            out_hi = lo if t == 0 else jnp.where(lane >= t, pltpu.roll(lo, t, 1), 0.0)
            return jnp.concatenate([zero, out_hi], axis=1)
        r_lo, r_hi = pltpu.roll(lo, s, 1), pltpu.roll(hi, s, 1)
        return jnp.concatenate([jnp.where(lane >= s, r_lo, 0.0), jnp.where(lane >= s, r_hi, r_lo)], axis=1)
    s = -s
    if s >= LANES:
        t = s - LANES
        out_lo = hi if t == 0 else jnp.where(lane < LANES - t, pltpu.roll(hi, LANES - t, 1), 0.0)
        return jnp.concatenate([out_lo, zero], axis=1)
    r_lo, r_hi = pltpu.roll(lo, LANES - s, 1), pltpu.roll(hi, LANES - s, 1)
    keep = lane < LANES - s
    return jnp.concatenate([jnp.where(keep, r_lo, r_hi), jnp.where(keep, r_hi, 0.0)], axis=1)


def _ssm_prep_kernel(are_ref, aim_ref, ldt_ref, btre_ref, btim_ref, cre_ref, cim_ref,
                     w_ref, bst_ref, cst_ref, lamt_ref):
    t_blk = SSM_T
    ns = SSM_STATE
    hp = lax.Precision.HIGHEST
    jf = lax.broadcasted_iota(jnp.int32, (2 * t_blk, ns), 0).astype(F32)
    bt_re = btre_ref[...]
    bt_im = btim_ref[...]
    c_re = cre_ref[...]
    c_im = cim_ref[...]
    p_re, p_im, bb_re, bb_im, kt = [], [], [], [], []
    for d in range(2):
        a_re = are_ref[d:d + 1, :]
        a_im = aim_ref[d:d + 1, :]
        dt = jnp.exp(ldt_ref[d:d + 1, :])
        mag = jnp.exp(jf * (a_re * dt))
        ang = jf * (a_im * dt)
        p_re.append(mag * jnp.cos(ang))
        p_im.append(mag * jnp.sin(ang))
        nr = p_re[d][1:2] - 1.0
        ni = p_im[d][1:2]
        den = a_re * a_re + a_im * a_im
        f_re = (nr * a_re + ni * a_im) / den
        f_im = (ni * a_re - nr * a_im) / den
        bb_re.append(f_re * bt_re - f_im * bt_im)
        bb_im.append(f_re * bt_im + f_im * bt_re)
        lags = range(t_blk) if d == 0 else range(t_blk - 1, -1, -1)
        a_parts_re = [c_re * p_re[d][j:j + 1] - c_im * p_im[d][j:j + 1] for j in lags]
        a_parts_im = [c_re * p_im[d][j:j + 1] + c_im * p_re[d][j:j + 1] for j in lags]
        nt = (((1,), (1,)), ((), ()))
        kt.append(lax.dot_general(bb_re[d], jnp.concatenate(a_parts_re, axis=0), nt,
                                  precision=hp, preferred_element_type=F32)
                  - lax.dot_general(bb_im[d], jnp.concatenate(a_parts_im, axis=0), nt,
                                    precision=hp, preferred_element_type=F32))
        lamt_ref[0:1, d * ns:(d + 1) * ns] = p_re[d][t_blk:t_blk + 1]
        lamt_ref[1:2, d * ns:(d + 1) * ns] = p_im[d][t_blk:t_blk + 1]
    for t in range(t_blk):
        rows = slice(t * SSM_GROUP, (t + 1) * SSM_GROUP)
        w_ref[rows, :] = (_shift_lanes(kt[0], t * SSM_GROUP)
                          + _shift_lanes(kt[1], -(t_blk - 1 - t) * SSM_GROUP)).astype(BF16)
        ef, eb = t_blk - 1 - t, t
        bst = [bb_re[0] * p_re[0][ef:ef + 1] - bb_im[0] * p_im[0][ef:ef + 1],
               bb_re[1] * p_re[1][eb:eb + 1] - bb_im[1] * p_im[1][eb:eb + 1],
               bb_re[0] * p_im[0][ef:ef + 1] + bb_im[0] * p_re[0][ef:ef + 1],
               bb_re[1] * p_im[1][eb:eb + 1] + bb_im[1] * p_re[1][eb:eb + 1]]
        bst_ref[rows, :] = jnp.concatenate(bst, axis=1).astype(BF16)
        ef, eb = t + 1, t_blk - t
        cst = [c_re * p_re[0][ef:ef + 1] - c_im * p_im[0][ef:ef + 1],
               c_re * p_re[1][eb:eb + 1] - c_im * p_im[1][eb:eb + 1],
               -(c_re * p_im[0][ef:ef + 1] + c_im * p_re[0][ef:ef + 1]),
               -(c_re * p_im[1][eb:eb + 1] + c_im * p_re[1][eb:eb + 1])]
        cst_ref[rows, :] = jnp.concatenate(cst, axis=1).astype(BF16)


def _ssm_prep(ssm_a_re, ssm_a_im, ssm_log_dt, ssm_b_re, ssm_b_im, ssm_c_re, ssm_c_im, ssm_d):
    t_blk = SSM_T
    bw = t_blk * SSM_GROUP
    lgd = lambda a: a.transpose(0, 2, 1, 3)
    ldt = jnp.broadcast_to(ssm_log_dt[..., None], ssm_a_re.shape)
    vec = pl.BlockSpec((None, None, 2, SSM_STATE), lambda l, g: (l, g, 0, 0))
    mat = pl.BlockSpec((None, None, SSM_GROUP, SSM_STATE), lambda l, g: (l, g, 0, 0))
    op = lambda width: pl.BlockSpec((None, None, bw, width), lambda l, g: (l, g, 0, 0))
    w, bst, cst, lamt = pl.pallas_call(
        _ssm_prep_kernel,
        grid=(DEPTH, SSM_GROUPS),
        in_specs=[vec, vec, vec, mat, mat, mat, mat],
        out_specs=[op(bw), op(4 * SSM_STATE), op(4 * SSM_STATE),
                   pl.BlockSpec((None, None, 2, 2 * SSM_STATE), lambda l, g: (l, g, 0, 0))],
        out_shape=[
            jax.ShapeDtypeStruct((DEPTH, SSM_GROUPS, bw, bw), BF16),
            jax.ShapeDtypeStruct((DEPTH, SSM_GROUPS, bw, 4 * SSM_STATE), BF16),
            jax.ShapeDtypeStruct((DEPTH, SSM_GROUPS, bw, 4 * SSM_STATE), BF16),
            jax.ShapeDtypeStruct((DEPTH, SSM_GROUPS, 2, 2 * SSM_STATE), F32),
        ],
        compiler_params=_cparams(("parallel", "parallel")),
        name="ssm_prep",
    )(lgd(ssm_a_re), lgd(ssm_a_im), lgd(ldt), jnp.swapaxes(ssm_b_re, -1, -2), jnp.swapaxes(ssm_b_im, -1, -2),
      ssm_c_re, ssm_c_im)
    d_row = jnp.tile(ssm_d, (1, 1, t_blk)).reshape(DEPTH, SSM_GROUPS, 1, bw)
    return dict(w=w, bst=bst, cst=cst, lamt=lamt, d=d_row)


def _lb_kernel(x_ref, o_ref):
    x = x_ref[...]
    m = jnp.max(x, axis=0, keepdims=True)
    e = jnp.exp(x - m)
    s = e / jnp.sum(e, axis=0, keepdims=True)
    run = jnp.zeros_like(s[0:1])
    o_ref[0:1, :] = run
    for l in range(1, DEPTH):
        run = run + s[l:l + 1]
        o_ref[l:l + 1, :] = run


def _lower_bounds(hg_lb):
    w = 2 * HG_KW
    out = pl.pallas_call(
        _lb_kernel,
        out_shape=jax.ShapeDtypeStruct((DEPTH, w), F32),
        name="hgrn_lower_bounds",
    )(hg_lb.reshape(DEPTH, w))
    return out.reshape(DEPTH, 2, HG_HEADS, HG_KDIM).transpose(0, 2, 1, 3)


def _mod_kernel(c_ref, w_ref, b_ref, o_ref):
    c = c_ref[...]
    a = _silu(c).astype(BF16)
    o_ref[...] = _dot(a, w_ref[...].astype(BF16)) + b_ref[...]


def _modulation(c, c_ctx, w_mod, b_mod):
    rows = SUBLANES
    cond = jnp.concatenate([c_ctx[None, :], c, jnp.zeros((rows - 1 - DEC_BATCH, D_MODEL), F32)], axis=0)
    tn = D_MODEL
    out = pl.pallas_call(
        _mod_kernel,
        grid=(DEPTH, 6),
        in_specs=[
            pl.BlockSpec((rows, D_MODEL), lambda l, j: (0, 0)),
            pl.BlockSpec((None, D_MODEL, tn), lambda l, j: (l, 0, j)),
            pl.BlockSpec((None, 1, tn), lambda l, j: (l, 0, j)),
        ],
        out_specs=pl.BlockSpec((None, rows, tn), lambda l, j: (l, 0, j)),
        out_shape=jax.ShapeDtypeStruct((DEPTH, rows, 6 * D_MODEL), F32),
        compiler_params=_cparams(("parallel", "parallel")),
        name="modulation",
    )(cond, w_mod, b_mod.reshape(DEPTH, 1, 6 * D_MODEL))
    return out.reshape(DEPTH, rows, 6, D_MODEL)


def _inproj_kernel(x_ref, mod_ref, g_ref, w_ref, o_ref, h_scr):
    @pl.when(pl.program_id(1) == 0)
    def _():
        x = x_ref[...]
        ms = jnp.mean(x * x, axis=-1, keepdims=True)
        y = x * lax.rsqrt(ms + EPS) * g_ref[...]
        h = y * (1.0 + mod_ref[1:2, :]) + mod_ref[0:1, :]
        h_scr[...] = h.astype(BF16)

    o_ref[...] = _dot(h_scr[...], w_ref[...])


def _in_proj(x, mod, norm_g, w_in, layer, seq_len, cond_row0):
    n = x.shape[0]
    tm = 1024
    tn = 2304
    rows_per_cond = seq_len if cond_row0 else n
    mod_idx = (lambda i, j: (layer, cond_row0 + (i * tm) // rows_per_cond, 0, 0))
    return pl.pallas_call(
        _inproj_kernel,
        grid=(n // tm, IN_WIDTH // tn),
        in_specs=[
            pl.BlockSpec((tm, D_MODEL), lambda i, j: (i, 0)),
            pl.BlockSpec((None, None, 6, D_MODEL), mod_idx),
            pl.BlockSpec((None, 1, D_MODEL), lambda i, j: (layer, 0, 0)),
            pl.BlockSpec((None, D_MODEL, tn), lambda i, j: (layer, 0, j)),
        ],
        out_specs=pl.BlockSpec((tm, tn), lambda i, j: (i, j)),
        out_shape=jax.ShapeDtypeStruct((n, IN_WIDTH), F32),
        scratch_shapes=[pltpu.VMEM((tm, D_MODEL), BF16)],
        compiler_params=_cparams(("parallel", "arbitrary")),
        name="in_proj",
    )(x, mod, norm_g, w_in)


def _pair_norm(x, g):
    xx = x * x
    s_a = jnp.sum(xx[:, :HEAD_DIM], axis=-1, keepdims=True)
    s_b = jnp.sum(xx[:, HEAD_DIM:], axis=-1, keepdims=True)
    lane = lax.broadcasted_iota(jnp.int32, x.shape, 1)
    ms = jnp.where(lane < HEAD_DIM, s_a, s_b) * (1.0 / HEAD_DIM)
    return x * lax.rsqrt(ms + EPS) * g


def _rope(x, cos, sin_signed):
    lane = lax.broadcasted_iota(jnp.int32, x.shape, 1)
    first = (lane & 31) < 16
    partner = jnp.where(first, pltpu.roll(x, LANES - 16, 1), pltpu.roll(x, 16, 1))
    return x * cos + partner * sin_signed


def _sink_column(sink_ref, layer, kh, t):
    row = lax.broadcasted_iota(jnp.int32, (KV_GROUP * t, 1), 0)
    col = jnp.full((KV_GROUP * t, 1), sink_ref[layer, kh * KV_GROUP], F32)
    for g in range(1, KV_GROUP):
        col = jnp.where(row >= g * t, sink_ref[layer, kh * KV_GROUP + g], col)
    return col


def _sink_softmax_av(heads):
    ms = []
    for scores, _, sink_col in heads:
        m = sink_col
        for s in scores:
            m = jnp.maximum(m, jnp.max(s, axis=-1, keepdims=True))
        ms.append(m)
    ps = [[jnp.exp(s - m) for s in scores] for (scores, _, _), m in zip(heads, ms)]
    outs = []
    for (scores, values, sink_col), m, p_list in zip(heads, ms, ps):
        den = jnp.exp(sink_col - m)
        o = None
        for p, v in zip(p_list, values):
            den = den + jnp.sum(p, axis=-1, keepdims=True)
            pv = _dot(p.astype(BF16), v.astype(BF16))
            o = pv if o is None else o + pv
        outs.append(o / den)
    return outs


def _attn_ctx_kernel(sink_ref, q_ref, kv_ref, gq_ref, gk_ref, att_ref, k_ref, v_ref, *, layer):
    t = q_ref.shape[0]
    kv = kv_ref[...]
    k = _pair_norm(kv[:, :KV_W], gk_ref[...])
    v = kv[:, KV_W:]
    k_ref[...] = k
    v_ref[...] = v
    q = q_ref[...]
    scale = 1.0 / math.sqrt(HEAD_DIM)
    work = []
    for kh in range(N_KV_HEADS):
        heads = []
        for j in range(KV_GROUP // 2):
            c0 = (kh * (KV_GROUP // 2) + j) * LANES
            qn = _pair_norm(q[:, c0:c0 + LANES], gq_ref[...])
            heads += [qn[:, :HEAD_DIM], qn[:, HEAD_DIM:]]
        qs = jnp.concatenate(heads, axis=0).astype(BF16)
        sl = slice(kh * HEAD_DIM, (kh + 1) * HEAD_DIM)
        s = _dot_nt(qs, k[:, sl].astype(BF16)) * scale
        work.append(([s], [v[:, sl]], _sink_column(sink_ref, layer, kh, t)))
    outs = [o[g * t:(g + 1) * t] for o in _sink_softmax_av(work) for g in range(KV_GROUP)]
    att_ref[...] = jnp.concatenate(outs, axis=-1).astype(BF16)


def _attn_ctx(sink, p, gq, gk, layer):
    n = p.shape[0]
    t = SEQ
    kern = functools.partial(_attn_ctx_kernel, layer=layer)
    return pl.pallas_call(
        kern,
        grid=(n // t,),
        in_specs=[
            pl.BlockSpec(memory_space=pltpu.SMEM),
            pl.BlockSpec((t, Q_W), lambda b: (b, COL_AQ // Q_W)),
            pl.BlockSpec((t, 2 * KV_W), lambda b: (b, COL_AK // (2 * KV_W))),
            pl.BlockSpec((None, 1, LANES), lambda b: (layer, 0, 0)),
            pl.BlockSpec((None, 1, LANES), lambda b: (layer, 0, 0)),
        ],
        out_specs=[
            pl.BlockSpec((t, Q_W), lambda b: (b, 0)),
            pl.BlockSpec((t, KV_W), lambda b: (b, 0)),
            pl.BlockSpec((t, KV_W), lambda b: (b, 0)),
        ],
        out_shape=[
            jax.ShapeDtypeStruct((n, Q_W), BF16),
            jax.ShapeDtypeStruct((n, KV_W), F32),
            jax.ShapeDtypeStruct((n, KV_W), F32),
        ],
        compiler_params=_cparams(("parallel",)),
        name="attn_ctx",
    )(sink, p, p, gq, gk)


def _attn_lat_kernel(sink_ref, q_ref, kv_ref, kc_ref, vc_ref, gq_ref, gk_ref, cos_ref, sin_ref, att_ref, *, layer):
    t = ATT_BLOCK
    span = ATT_BLOCK + 2 * WINDOW
    start = pl.program_id(1) * ATT_BLOCK
    ks = pl.multiple_of(jnp.clip(start - WINDOW, 0, DEC_SEQ - span), ATT_BLOCK)
    q0 = pl.multiple_of(start, ATT_BLOCK)
    kvw = kv_ref[pl.ds(ks, span), :]
    k = _rope(_pair_norm(kvw[:, :KV_W], gk_ref[...]), cos_ref[pl.ds(ks, span), :], sin_ref[pl.ds(ks, span), :])
    v = kvw[:, KV_W:]
    cos_q = cos_ref[pl.ds(q0, t), :]
    sin_q = sin_ref[pl.ds(q0, t), :]
    qpos = start + (lax.broadcasted_iota(jnp.int32, (KV_GROUP * t, span), 0) & (t - 1))
    kpos = ks + lax.broadcasted_iota(jnp.int32, (KV_GROUP * t, span), 1)
    band = jnp.abs(qpos - kpos) <= WINDOW
    q = q_ref[...]
    kc = kc_ref[...]
    vc = vc_ref[...]
    scale = 1.0 / math.sqrt(HEAD_DIM)
    work = []
    for kh in range(N_KV_HEADS):
        heads = []
        for j in range(KV_GROUP // 2):
            c0 = (kh * (KV_GROUP // 2) + j) * LANES
            qn = _rope(_pair_norm(q[:, c0:c0 + LANES], gq_ref[...]), cos_q, sin_q)
            heads += [qn[:, :HEAD_DIM], qn[:, HEAD_DIM:]]
        qs = jnp.concatenate(heads, axis=0).astype(BF16)
        sl = slice(kh * HEAD_DIM, (kh + 1) * HEAD_DIM)
        s_w = jnp.where(band, _dot_nt(qs, k[:, sl].astype(BF16)) * scale, MASK_NEG)
        s_c = _dot_nt(qs, kc[:, sl].astype(BF16)) * scale
        work.append(([s_w, s_c], [v[:, sl], vc[:, sl]], _sink_column(sink_ref, layer, kh, t)))
    outs = [o[g * t:(g + 1) * t] for o in _sink_softmax_av(work) for g in range(KV_GROUP)]
    att_ref[...] = jnp.concatenate(outs, axis=-1).astype(BF16)


def _attn_lat(sink, p, cache_k, cache_v, gq, gk, cos_t, sin_t, layer):
    n = p.shape[0]
    t = ATT_BLOCK
    nb = DEC_SEQ // t
    kern = functools.partial(_attn_lat_kernel, layer=layer)
    return pl.pallas_call(
        kern,
        grid=(DEC_BATCH, nb),
        in_specs=[
            pl.BlockSpec(memory_space=pltpu.SMEM),
            pl.BlockSpec((t, Q_W), lambda b, j: (b * nb + j, COL_AQ // Q_W)),
            pl.BlockSpec((DEC_SEQ, 2 * KV_W), lambda b, j: (b, COL_AK // (2 * KV_W))),
            pl.BlockSpec((None, None, PAST_LEN, KV_W), lambda b, j: (b, layer, 0, 0)),
            pl.BlockSpec((None, None, PAST_LEN, KV_W), lambda b, j: (b, layer, 0, 0)),
            pl.BlockSpec((None, 1, LANES), lambda b, j: (layer, 0, 0)),
            pl.BlockSpec((None, 1, LANES), lambda b, j: (layer, 0, 0)),
            pl.BlockSpec((DEC_SEQ, LANES), lambda b, j: (0, 0)),
            pl.BlockSpec((DEC_SEQ, LANES), lambda b, j: (0, 0)),
        ],
        out_specs=pl.BlockSpec((t, Q_W), lambda b, j: (b * nb + j, 0)),
        out_shape=jax.ShapeDtypeStruct((n, Q_W), BF16),
        compiler_params=_cparams(("parallel", "parallel")),
        name="attn_lat",
    )(sink, p, p, cache_k, cache_v, gq, gk, cos_t, sin_t)


def _rope_tables():
    pos = jnp.arange(DEC_SEQ)
    rows = (pos // GRID_W).astype(F32)
    cols = (pos % GRID_W).astype(F32)
    half = HEAD_DIM // 2
    freqs = ROPE_BASE ** (-jnp.arange(0, half, 2, dtype=F32) / half)
    ang_r = rows[:, None] * freqs[None, :]
    ang_c = cols[:, None] * freqs[None, :]
    ang = jnp.concatenate([ang_r, ang_r, ang_c, ang_c], axis=-1)
    sign = jnp.tile(jnp.concatenate([-jnp.ones((16,), F32), jnp.ones((16,), F32)]), 2)
    cos_t = jnp.tile(jnp.cos(ang), (1, LANES // HEAD_DIM))
    sin_t = jnp.tile(jnp.sin(ang) * sign[None, :], (1, LANES // HEAD_DIM))
    return cos_t, sin_t


def _ssm_kernel(*refs, nseq, seq_len, has_h0):
    if has_h0:
        x_ref, w_ref, bst_ref, cst_ref, lamt_ref, d_ref, h0_ref, z_ref, loc_scr, ent_scr, cols_scr = refs
    else:
        x_ref, w_ref, bst_ref, cst_ref, lamt_ref, d_ref, z_ref, fin_ref, loc_scr, ent_scr, cols_scr = refs
    ns = SSM_STATE
    t_blk = SSM_T
    nblk = seq_len // t_blk
    ng = LANES // SSM_GROUP
    rows = nblk * SUBLANES
    per_half = LANES // SSM_GROUP
    if nseq < SUBLANES:
        cols_scr[...] = jnp.zeros_like(cols_scr)
    for t in range(t_blk):
        for b in range(nblk):
            cols_scr[t, b * SUBLANES:b * SUBLANES + nseq, :] = x_ref[pl.ds(b * t_blk + t, nseq, stride=seq_len), :]
    cols = [cols_scr[t] for t in range(t_blk)]
    chunk = lax.broadcasted_iota(jnp.int32, (rows, LANES), 1) // SSM_GROUP
    is_fwd = lax.broadcasted_iota(jnp.int32, (SUBLANES, 2 * ns), 1) < ns

    def roll(a, chunks):
        shift = (chunks * SSM_GROUP) % LANES
        return a if shift == 0 else pltpu.roll(a, shift, 1)

    xg = []
    for g in range(ng):
        halves = []
        for h in range(t_blk // per_half):
            acc = None
            for j in range(per_half):
                r = roll(cols[h * per_half + j], j - g)
                acc = r if acc is None else jnp.where(chunk == j, r, acc)
            halves.append(acc)
        xg.append(jnp.concatenate(halves, axis=1))
        loc_scr[g] = _dot(xg[g].astype(BF16), bst_ref[g])
    l_re = [jnp.broadcast_to(lamt_ref[g, 0:1, :], (SUBLANES, 2 * ns)) for g in range(ng)]
    l_im = [jnp.broadcast_to(lamt_ref[g, 1:2, :], (SUBLANES, 2 * ns)) for g in range(ng)]
    if has_h0:
        s_re = [h0_ref[g, 0] for g in range(ng)]
        s_im = [h0_ref[g, 1] for g in range(ng)]
    else:
        s_re = [jnp.zeros((SUBLANES, 2 * ns), F32) for _ in range(ng)]
        s_im = [jnp.zeros((SUBLANES, 2 * ns), F32) for _ in range(ng)]
    for j in range(nblk):
        rf = slice(j * SUBLANES, (j + 1) * SUBLANES)
        rb = slice((nblk - 1 - j) * SUBLANES, (nblk - j) * SUBLANES)
        for g in range(ng):
            ent_scr[g, rf, 0:ns] = s_re[g][:, 0:ns]
            ent_scr[g, rb, ns:2 * ns] = s_re[g][:, ns:2 * ns]
            ent_scr[g, rf, 2 * ns:3 * ns] = s_im[g][:, 0:ns]
            ent_scr[g, rb, 3 * ns:4 * ns] = s_im[g][:, ns:2 * ns]
            loc_re = jnp.where(is_fwd, loc_scr[g, rf, 0:2 * ns], loc_scr[g, rb, 0:2 * ns])
            loc_im = jnp.where(is_fwd, loc_scr[g, rf, 2 * ns:4 * ns], loc_scr[g, rb, 2 * ns:4 * ns])
            n_re = l_re[g] * s_re[g] - l_im[g] * s_im[g] + loc_re
            n_im = l_re[g] * s_im[g] + l_im[g] * s_re[g] + loc_im
            s_re[g], s_im[g] = n_re, n_im
    if not has_h0:
        for g in range(ng):
            fin_ref[g, 0] = s_re[g]
            fin_ref[g, 1] = s_im[g]
    out_cols = [None] * t_blk
    for g in range(ng):
        y = (_dot(xg[g].astype(BF16), w_ref[g]) + _dot_nt(ent_scr[g].astype(BF16), cst_ref[g])
             + d_ref[g] * xg[g])
        zg = _gelu_tanh(y)
        for h in range(t_blk // per_half):
            zh = zg[:, h * LANES:(h + 1) * LANES]
            for j in range(per_half):
                t = h * per_half + j
                r = roll(zh, g - j)
                out_cols[t] = r if out_cols[t] is None else jnp.where(chunk == g, r, out_cols[t])
    for t in range(t_blk):
        for b in range(nblk):
            z_ref[pl.ds(b * t_blk + t, nseq, stride=seq_len), :] = out_cols[t][b * SUBLANES:b * SUBLANES + nseq]


def _ssm(p, sw, layer, nseq, seq_len, h0=None):
    n = p.shape[0]
    t_blk = SSM_T
    nblk = seq_len // t_blk
    bw = t_blk * SSM_GROUP
    nsg = SSM_WIDTH // LANES
    ng = LANES // SSM_GROUP
    per_step = min(nseq, SUBLANES)
    has_h0 = h0 is not None
    kern = functools.partial(_ssm_kernel, nseq=per_step, seq_len=seq_len, has_h0=has_h0)
    wspec = lambda a, b: pl.BlockSpec((None, None, ng, a, b), lambda s, g: (layer, g, 0, 0, 0))
    grp = lambda a: a.reshape((DEPTH, nsg, ng) + a.shape[2:])
    tok = per_step * seq_len
    in_specs = [pl.BlockSpec((tok, LANES), lambda s, g: (s, COL_SU // LANES + g)),
                wspec(bw, bw), wspec(bw, 4 * SSM_STATE), wspec(bw, 4 * SSM_STATE),
                wspec(2, 2 * SSM_STATE), wspec(1, bw)]
    args = [p, grp(sw['w']), grp(sw['bst']), grp(sw['cst']), grp(sw['lamt']), grp(sw['d'])]
    out_specs = [pl.BlockSpec((tok, LANES), lambda s, g: (s, g))]
    out_shape = [jax.ShapeDtypeStruct((n, SSM_WIDTH), F32)]
    st_block = (ng, 2, SUBLANES, 2 * SSM_STATE)
    if has_h0:
        in_specs.append(pl.BlockSpec((None, None) + st_block, lambda s, g: (layer, g, 0, 0, 0, 0)))
        args.append(h0)
    else:
        out_specs.append(pl.BlockSpec((None, None) + st_block, lambda s, g: (s, g, 0, 0, 0, 0)))
        out_shape.append(jax.ShapeDtypeStruct((nseq // per_step, nsg) + st_block, F32))
    rows = nblk * SUBLANES
    return pl.pallas_call(
        kern,
        grid=(nseq // per_step, nsg),
        in_specs=in_specs,
        out_specs=out_specs,
        out_shape=out_shape,
        scratch_shapes=[pltpu.VMEM((ng, rows, 4 * SSM_STATE), F32), pltpu.VMEM((ng, rows, 4 * SSM_STATE), F32),
                        pltpu.VMEM((t_blk, rows, LANES), F32)],
        compiler_params=_cparams(("parallel", "parallel")),
        name="ssm",
    )(*args)


def _hgrn_consts():
    c = HG_CHUNK
    i = np.arange(c)[:, None]
    r = np.arange(c)[None, :]
    m_all = np.zeros((2, HG_LEVELS, c, c), np.float32)
    lev_of = np.full((2, c, c), -1, np.int32)
    for lev in range(HG_LEVELS):
        n = 2 << lev
        half = n // 2
        mid = (i // n) * n + half
        upper = (i % n) >= half
        m_all[0, lev] = np.where(upper, (r >= mid) & (r <= i), (r > i) & (r < mid))
        m_all[1, lev] = np.where(upper, (r >= mid) & (r < i), (r >= i) & (r < mid))
        same = (i // n) == (r // n)
        r_upper = (r % n) >= half
        lev_of[0][same & upper & ~r_upper] = lev
        lev_of[1][same & ~upper & r_upper] = lev
    cum = np.stack([r <= i, r >= i]).astype(np.float32)
    m_small = np.concatenate([cum] + [m_all[:, lev] for lev in HG_MXU_LEVELS], axis=1)
    return (jnp.asarray(np.concatenate([m_small, m_small], axis=-1), BF16), jnp.asarray(lev_of))


def _hgrn_chunks(chains, states, m_ref, lev_ref):
    c = HG_CHUNK
    nb = c // SUBLANES
    w = []
    for d, key, r0, hq_ref, f_ref, hi_ref, lb, cum_ref in chains:
        q = _silu(hq_ref[pl.ds(r0, c), :])
        fg = lb + (1.0 - lb) * _sigmoid(f_ref[pl.ds(r0, c), :])
        k = 1.0 - fg
        log2f = jnp.log(fg) * (1.0 / math.log(2.0))
        v = hi_ref[pl.ds(r0, c), :]
        hi = log2f.astype(BF16)
        lo = (log2f - hi.astype(F32)).astype(BF16)
        sums = _dot(m_ref[d], jnp.concatenate([hi, lo], axis=0))
        w.append(dict(d=d, key=key, q=q, fg=fg, k=k, v=v, vb=v.astype(BF16), sums=sums, cum=sums[0:c],
                      cum_ref=cum_ref, att=[None] * nb))
    for x in w:
        x['cum_ref'][...] = x['cum']

    def select_into(x, rows, lev, a):
        att = x['att']
        for j in range(rows.stop // SUBLANES - rows.start // SUBLANES):
            i = rows.start // SUBLANES + j
            blk = a[j * SUBLANES:(j + 1) * SUBLANES]
            mask = lev_ref[x['d'], i * SUBLANES:(i + 1) * SUBLANES, :] == lev
            att[i] = jnp.where(mask, blk, 0.0 if att[i] is None else att[i])

    full = slice(0, c)
    prods = [_dot_nt((x['q'] * x['fg']).astype(BF16), x['k'].astype(BF16)) for x in w]
    for x, a in zip(w, prods):
        select_into(x, full, 0, a)
    for n_lev, lev in enumerate(HG_MXU_LEVELS):
        prods = []
        for x in w:
            g = jnp.exp2(x['sums'][(n_lev + 1) * c:(n_lev + 2) * c])
            prods.append(_dot_nt((x['q'] * g).astype(BF16), (x['k'] * g).astype(BF16)))
        for x, a in zip(w, prods):
            select_into(x, full, lev, a)
    for lev in range(HG_MXU_LEVELS[-1] + 1, HG_LEVELS):
        half = 1 << lev
        n = 2 * half
        prods = []
        for x in w:
            d, q, k, cum = x['d'], x['q'], x['k'], x['cum']
            q_parts, k_parts, q_rows = [], [], []
            for j in range(c // n):
                lower = slice(j * n, j * n + half)
                upper = slice(j * n + half, (j + 1) * n)
                qr, kr = (upper, lower) if d == 0 else (lower, upper)
                ref_row = j * n + half - 1 if d == 0 else j * n + half
                ref = jnp.broadcast_to(x['cum_ref'][ref_row:ref_row + 1, :], (half, HG_KDIM))
                q_parts.append(q[qr] * jnp.exp2(cum[qr] - ref))
                k_scaled = k[kr] * jnp.exp2(ref - cum[kr])
                k_parts += [k_scaled, k[qr]] if d == 0 else [k[qr], k_scaled]
                q_rows.append(qr)
            prods.append((q_rows, _dot_nt(jnp.concatenate(q_parts, axis=0).astype(BF16),
                                          jnp.concatenate(k_parts, axis=0).astype(BF16))))
        for x, (q_rows, a) in zip(w, prods):
            for j, qr in enumerate(q_rows):
                select_into(x, qr, lev, a[j * half:(j + 1) * half])
    for x in w:
        d, q, k, cum = x['d'], x['q'], x['k'], x['cum']
        last = c - 1 if d == 0 else 0
        tot_row = cum[last:last + 1, :]
        x['lhs'] = jnp.concatenate([jnp.concatenate(x['att'], axis=0).astype(BF16),
                                    (q * jnp.exp2(cum)).astype(BF16)], axis=1)
        x['kg'] = (k * jnp.exp2(tot_row - cum)).astype(BF16)
        x['dv'] = jnp.sum(q * k, axis=-1, keepdims=True) * x['v']
        x['tot_col'] = jnp.broadcast_to(jnp.exp2(tot_row), (c, HG_KDIM)).T
    states = list(states)
    outs = []
    for x in w:
        st = states[x['key']]
        outs.append(x['dv'] + _dot(x['lhs'], jnp.concatenate([x['vb'], st.astype(BF16)], axis=0)))
        states[x['key']] = st * x['tot_col'] + _dot_tn(x['kg'], x['vb'])
    return outs, states


def _hgrn_kernel(*refs, seq_len, nh, unroll, has_s0):
    if has_s0:
        (hq_ref, ff_ref, fb_ref, hi_ref, hgt_ref, lb_ref, ng_ref, m_ref, lev_ref, s0_ref,
         o_ref, of_scr, ob_scr, cum_scr) = refs
    else:
        (hq_ref, ff_ref, fb_ref, hi_ref, hgt_ref, lb_ref, ng_ref, m_ref, lev_ref,
         o_ref, sfin_ref, of_scr, ob_scr, cum_scr) = refs
    c = HG_CHUNK
    nch = seq_len // c
    head = lambda ref, h: ref.at[:, h * HG_KDIM:(h + 1) * HG_KDIM]

    def step(ci, carry):
        chains, rows = [], []
        for u in range(unroll):
            cf = ci * unroll + u
            rf = pl.multiple_of(cf * c, c)
            rb = pl.multiple_of((nch - 1 - cf) * c, c)
            for h in range(nh):
                hq, hi = head(hq_ref, h), head(hi_ref, h)
                slot = (u * nh + h) * 2
                chains.append((0, 2 * h, rf, hq, head(ff_ref, h), hi, lb_ref[h, 0:1, :], cum_scr.at[slot]))
                chains.append((1, 2 * h + 1, rb, hq, head(fb_ref, h), hi, lb_ref[h, 1:2, :], cum_scr.at[slot + 1]))
                rows += [(head(of_scr, h), rf), (head(ob_scr, h), rb)]
        outs, states = _hgrn_chunks(chains, carry, m_ref, lev_ref)
        for (scr, r0), o in zip(rows, outs):
            scr[pl.ds(r0, c), :] = o
        return tuple(states)

    if has_s0:
        init = tuple(s0_ref[d, h] for h in range(nh) for d in range(2))
    else:
        init = tuple(jnp.zeros((HG_KDIM, HG_VDIM), F32) for _ in range(2 * nh))
    if nch == unroll:
        fin = step(0, init)
    else:
        fin = lax.fori_loop(0, nch // unroll, step, init)
    if not has_s0:
        for h in range(nh):
            for d in range(2):
                sfin_ref[d, h] = fin[2 * h + d]

    for h in range(nh):
        o = head(of_scr, h)[...] + head(ob_scr, h)[...]
        ms = jnp.mean(o * o, axis=-1, keepdims=True)
        y = o * lax.rsqrt(ms + EPS) * ng_ref[...]
        head(o_ref, h)[...] = (y * _silu(head(hgt_ref, h)[...])).astype(BF16)


def _hgrn(p, lb, norm_g, m_all, lev_of, layer, seq_len, s0=None):
    n = p.shape[0]
    nseq = n // seq_len
    has_s0 = s0 is not None
    nch = seq_len // HG_CHUNK
    unroll = min(nch, HG_CHAINS // 2)
    nh = HG_CHAINS // (2 * unroll)
    kern = functools.partial(_hgrn_kernel, seq_len=seq_len, nh=nh, unroll=unroll, has_s0=has_s0)
    base = COL_HG // (nh * LANES)
    per = HG_HEADS // nh

    def col(k):
        return pl.BlockSpec((seq_len, nh * LANES), lambda b, h, k=k: (b, base + k * per + h))

    c = HG_CHUNK
    in_specs = [col(0), col(1), col(2), col(3), col(4),
                pl.BlockSpec((None, nh, 2, HG_KDIM), lambda b, h: (layer, h, 0, 0)),
                pl.BlockSpec((None, 1, HG_VDIM), lambda b, h: (layer, 0, 0)),
                pl.BlockSpec((2, (len(HG_MXU_LEVELS) + 1) * c, 2 * c), lambda b, h: (0, 0, 0)),
                pl.BlockSpec((2, c, c), lambda b, h: (0, 0, 0))]
    args = [p, p, p, p, p, lb, norm_g, m_all, lev_of]
    out_specs = [pl.BlockSpec((seq_len, nh * HG_VDIM), lambda b, h: (b, h))]
    out_shape = [jax.ShapeDtypeStruct((n, HG_VW), BF16)]
    if has_s0:
        in_specs.append(pl.BlockSpec((None, None, 2, nh, HG_KDIM, HG_VDIM), lambda b, h: (b, layer, 0, h, 0, 0)))
        args.append(s0)
    else:
        out_specs.append(pl.BlockSpec((None, 2, nh, HG_KDIM, HG_VDIM), lambda b, h: (b, 0, h, 0, 0)))
        out_shape.append(jax.ShapeDtypeStruct((nseq, 2, HG_HEADS, HG_KDIM, HG_VDIM), F32))
    return pl.pallas_call(
        kern,
        grid=(nseq, per),
        in_specs=in_specs,
        out_specs=out_specs,
        out_shape=out_shape,
        scratch_shapes=[pltpu.VMEM((seq_len, nh * HG_VDIM), F32), pltpu.VMEM((seq_len, nh * HG_VDIM), F32),
                        pltpu.VMEM((HG_CHAINS, HG_CHUNK, HG_KDIM), F32)],
        compiler_params=_cparams(("parallel", "parallel")),
        name="hgrn",
    )(*args)


ROUTER_ROWS = 32


def _router(logits_t):
    row = lax.broadcasted_iota(jnp.int32, logits_t.shape, 0)
    big = jnp.int32(ROUTER_ROWS)
    red = dict(axis=0, keepdims=True)
    is_g = row < N_GROUPS
    gl = jnp.where(is_g, logits_t, -jnp.inf)
    gmax = jnp.max(gl, **red)
    gsum = jnp.sum(jnp.exp(gl - gmax), **red)
    g_idx = jnp.min(jnp.where(is_g & (gl == gmax), row, big), **red)
    g_prob = 1.0 / gsum
    e_row = row - ROUTER_OFF
    sel = (e_row >= 0) & (e_row < N_EXPERTS) & ((e_row >> 2) == g_idx)
    el = jnp.where(sel, logits_t, -jnp.inf)
    emax = jnp.max(el, **red)
    eexp = jnp.exp(el - emax)
    ep = eexp / jnp.sum(eexp, **red)
    p1 = jnp.max(ep, **red)
    i1 = jnp.min(jnp.where(sel & (ep == p1), row, big), **red)
    rest = sel & (row != i1)
    ep2 = jnp.where(rest, ep, -1.0)
    p2 = jnp.max(ep2, **red)
    i2 = jnp.min(jnp.where(rest & (ep2 == p2), row, big), **red)
    den = p1 + p2
    w1 = g_prob * (p1 / den)
    w2 = g_prob * (p2 / den)
    return jnp.where(row == i1, w1, jnp.where(row == i2, w2, 0.0))


def _merge_kernel(x_ref, att_ref, z_ref, hg_ref, mg_ref, mod_ref, n2_ref,
                  wao_ref, wga_ref, wgb_ref, who_ref, wout_ref, wr_ref, br_ref,
                  xm_ref, h2_ref, comb_ref):
    y_att = _dot(att_ref[...], wao_ref[...])
    z = z_ref[...].astype(BF16)
    y_ssm = _dot(z, wga_ref[...]) * _sigmoid(_dot(z, wgb_ref[...]))
    y_hg = _dot(hg_ref[...], who_ref[...])
    merged = (_sigmoid(mg_ref[:, 0:D_MODEL]) * y_att
              + _sigmoid(mg_ref[:, D_MODEL:2 * D_MODEL]) * y_ssm
              + _sigmoid(mg_ref[:, 2 * D_MODEL:3 * D_MODEL]) * y_hg)
    xm = x_ref[...] + mod_ref[2:3, :] * _dot(merged.astype(BF16), wout_ref[...])
    xm_ref[...] = xm
    ms = jnp.mean(xm * xm, axis=-1, keepdims=True)
    h2 = xm * lax.rsqrt(ms + EPS) * n2_ref[...] * (1.0 + mod_ref[4:5, :]) + mod_ref[3:4, :]
    h2_hi = h2.astype(BF16)
    h2_ref[...] = h2_hi
    h2_lo = (h2 - h2_hi.astype(F32)).astype(BF16)
    logits = _dot(jnp.concatenate([h2_hi, h2_lo, h2_hi], axis=1), wr_ref[...]) + br_ref[...]
    comb_t = _router(logits.T[0:ROUTER_ROWS])
    comb_t = jnp.concatenate([comb_t, jnp.zeros((LANES - ROUTER_ROWS, comb_t.shape[1]), F32)], axis=0)
    comb_ref[...] = comb_t.T


def _merge(x, att, z, hg, p, mod, n2g, w, layer, seq_len, cond_row0):
    n = x.shape[0]
    tm = 256
    rows_per_cond = seq_len if cond_row0 else n
    mod_idx = (lambda i: (layer, cond_row0 + (i * tm) // rows_per_cond, 0, 0))
    row = lambda width: pl.BlockSpec((tm, width), lambda i: (i, 0))
    wspec = lambda a, b: pl.BlockSpec((None, a, b), lambda i: (layer, 0, 0))
    return pl.pallas_call(
        _merge_kernel,
        grid=(n // tm,),
        in_specs=[
            row(D_MODEL), row(Q_W), row(SSM_WIDTH), row(HG_VW),
            pl.BlockSpec((tm, MG_W), lambda i: (i, COL_MG // MG_W)),
            pl.BlockSpec((None, None, 6, D_MODEL), mod_idx),
            wspec(1, D_MODEL),
            wspec(Q_W, D_MODEL), wspec(SSM_WIDTH, D_MODEL), wspec(SSM_WIDTH, D_MODEL),
            wspec(HG_VW, D_MODEL), wspec(D_MODEL, D_MODEL), wspec(3 * D_MODEL, LANES), wspec(1, LANES),
        ],
        out_specs=[row(D_MODEL), row(D_MODEL), row(LANES)],
        out_shape=[
            jax.ShapeDtypeStruct((n, D_MODEL), F32),
            jax.ShapeDtypeStruct((n, D_MODEL), BF16),
            jax.ShapeDtypeStruct((n, LANES), F32),
        ],
        compiler_params=_cparams(("parallel",)),
        name="merge",
    )(x, att, z, hg, p, mod, n2g, w['w_attn_o'], w['w_glu_a'], w['w_glu_b'], w['w_hg_o'], w['w_out'],
      w['w_route'], w['b_route'])


def _moe_kernel(h_ref, comb_ref, xm_ref, mod_ref, wg_ref, wu_ref, wd_ref, o_ref, acc_scr):
    grp = pl.program_id(1)
    comb = comb_ref[...]
    lane = lax.broadcasted_iota(jnp.int32, comb.shape, 1)
    h = h_ref[...]
    hg = _dot(h, wg_ref[...])
    hu = _dot(h, wu_ref[...])
    parts = []
    for e in range(EXPERTS_PER_GROUP):
        cw = jnp.sum(jnp.where(lane == ROUTER_OFF + grp * EXPERTS_PER_GROUP + e, comb, 0.0),
                     axis=-1, keepdims=True)
        sl = slice(e * EXPERT_FF, (e + 1) * EXPERT_FF)
        parts.append((_silu(hg[:, sl]) * hu[:, sl] * cw).astype(BF16))
    y = _dot(jnp.concatenate(parts, axis=1), wd_ref[...])

    @pl.when(grp == 0)
    def _():
        acc_scr[...] = y

    @pl.when(grp > 0)
    def _():
        acc_scr[...] += y

    @pl.when(grp == N_GROUPS - 1)
    def _():
        o_ref[...] = xm_ref[...] + mod_ref[5:6, :] * acc_scr[...]


def _moe(h2, comb, xm, mod, w, layer, seq_len, cond_row0):
    n = h2.shape[0]
    tm = 512
    gw = EXPERTS_PER_GROUP * EXPERT_FF
    rows_per_cond = seq_len if cond_row0 else n
    mod_idx = (lambda i, g: (layer, cond_row0 + (i * tm) // rows_per_cond, 0, 0))
    return pl.pallas_call(
        _moe_kernel,
        grid=(n // tm, N_GROUPS),
        in_specs=[
            pl.BlockSpec((tm, D_MODEL), lambda i, g: (i, 0)),
            pl.BlockSpec((tm, LANES), lambda i, g: (i, 0)),
            pl.BlockSpec((tm, D_MODEL), lambda i, g: (i, 0)),
            pl.BlockSpec((None, None, 6, D_MODEL), mod_idx),
            pl.BlockSpec((None, D_MODEL, gw), lambda i, g: (layer, 0, g)),
            pl.BlockSpec((None, D_MODEL, gw), lambda i, g: (layer, 0, g)),
            pl.BlockSpec((None, gw, D_MODEL), lambda i, g: (layer, g, 0)),
        ],
        out_specs=pl.BlockSpec((tm, D_MODEL), lambda i, g: (i, 0)),
        out_shape=jax.ShapeDtypeStruct((n, D_MODEL), F32),
        scratch_shapes=[pltpu.VMEM((tm, D_MODEL), F32)],
        compiler_params=_cparams(("parallel", "arbitrary")),
        name="moe",
    )(h2, comb, xm, mod, w['w_e_gate'], w['w_e_up'], w['w_e_down'])


def _layer(x, layer, w, seq_len, cond_row0, cache):
    p = _in_proj(x, w['mod'], w['norm1_g'], w['w_in'], layer, seq_len, cond_row0)
    nseq = x.shape[0] // seq_len
    if cache is None:
        att, k_new, v_new = _attn_ctx(w['attn_sink'], p, w['q_norm_g'], w['k_norm_g'], layer)
        z, ssm_fin = _ssm(p, w['ssm'], layer, nseq, seq_len)
        hg, hg_fin = _hgrn(p, w['hg_lb'], w['hg_norm_g'], w['hg_m'], w['hg_lev'], layer, seq_len)
        ctx = (k_new, v_new, ssm_fin, hg_fin)
    else:
        cache_k, cache_v, h0, s0 = cache
        att = _attn_lat(w['attn_sink'], p, cache_k, cache_v, w['q_norm_g'], w['k_norm_g'],
                        w['rope_cos'], w['rope_sin'], layer)
        (z,) = _ssm(p, w['ssm'], layer, nseq, seq_len, h0=h0)
        (hg,) = _hgrn(p, w['hg_lb'], w['hg_norm_g'], w['hg_m'], w['hg_lev'], layer, seq_len, s0=s0)
        ctx = None
    xm, h2, comb = _merge(x, att, z, hg, p, w['mod'], w['norm2_g'], w, layer, seq_len, cond_row0)
    x = _moe(h2, comb, xm, w['mod'], w, layer, seq_len, cond_row0)
    return x, ctx


def kernel(x_prompt, x_sample, cache_k, cache_v, state_ssm_re, state_ssm_im, state_hgrn, c, c_ctx, w_mod, b_mod, norm1_g, norm2_g, w_in, q_norm_g, k_norm_g, attn_sink, w_attn_o, ssm_a_re, ssm_a_im, ssm_log_dt, ssm_b_re, ssm_b_im, ssm_c_re, ssm_c_im, ssm_d, w_glu_a, w_glu_b, hg_lb, hg_norm_g, w_hg_o, w_out, w_group, b_group, w_router, b_router, w_e_gate, w_e_up, w_e_down):
    w = {}
    w['mod'] = _modulation(c, c_ctx, w_mod, b_mod)
    w['norm1_g'] = norm1_g.reshape(DEPTH, 1, D_MODEL)
    w['norm2_g'] = norm2_g.reshape(DEPTH, 1, D_MODEL)
    w['w_in'] = jnp.concatenate(
        [w_in[:, :, 3840:], w_in[:, :, 1280:3840], w_in[:, :, 0:768], w_in[:, :, 768:1280]], axis=-1).astype(BF16)
    w['q_norm_g'] = jnp.tile(q_norm_g, (1, LANES // HEAD_DIM)).reshape(DEPTH, 1, LANES)
    w['k_norm_g'] = jnp.tile(k_norm_g, (1, LANES // HEAD_DIM)).reshape(DEPTH, 1, LANES)
    w['attn_sink'] = attn_sink
    w['rope_cos'], w['rope_sin'] = _rope_tables()
    w['ssm'] = _ssm_prep(ssm_a_re, ssm_a_im, ssm_log_dt, ssm_b_re, ssm_b_im, ssm_c_re, ssm_c_im, ssm_d)
    w['hg_lb'] = _lower_bounds(hg_lb)
    w['hg_norm_g'] = hg_norm_g.reshape(DEPTH, 1, HG_VDIM)
    w['hg_m'], w['hg_lev'] = _hgrn_consts()
    for name, val in (('w_attn_o', w_attn_o), ('w_glu_a', w_glu_a), ('w_glu_b', w_glu_b), ('w_hg_o', w_hg_o),
                      ('w_out', w_out)):
        w[name] = val.astype(BF16)
    ff_all = N_EXPERTS * EXPERT_FF
    w['w_e_gate'] = w_e_gate.astype(BF16).transpose(0, 2, 1, 3).reshape(DEPTH, D_MODEL, ff_all)
    w['w_e_up'] = w_e_up.astype(BF16).transpose(0, 2, 1, 3).reshape(DEPTH, D_MODEL, ff_all)
    w['w_e_down'] = w_e_down.astype(BF16).reshape(DEPTH, ff_all, D_MODEL)
    pad = LANES - N_GROUPS - N_EXPERTS
    w_route = jnp.concatenate([w_group, w_router, jnp.zeros((DEPTH, D_MODEL, pad), F32)], axis=-1)
    w_route_hi = w_route.astype(BF16)
    w_route_lo = (w_route - w_route_hi.astype(F32)).astype(BF16)
    w['w_route'] = jnp.concatenate([w_route_hi, w_route_hi, w_route_lo], axis=1)
    w['b_route'] = jnp.concatenate([b_group, b_router, jnp.zeros((DEPTH, pad), F32)], axis=-1).reshape(DEPTH, 1, LANES)

    ck = cache_k.reshape(DEC_BATCH, DEPTH, PAST_LEN, KV_W)
    cv = cache_v.reshape(DEC_BATCH, DEPTH, PAST_LEN, KV_W)

    ng = LANES // SSM_GROUP
    h0 = jnp.stack([state_ssm_re, state_ssm_im]).transpose(2, 4, 0, 1, 3, 5).reshape(
        DEPTH, SSM_GROUPS // ng, ng, 2, DEC_BATCH, 2 * SSM_STATE)
    h0 = jnp.pad(h0, ((0, 0),) * 4 + ((0, SUBLANES - DEC_BATCH), (0, 0)))

    xp = x_prompt.reshape(BATCH * SEQ, D_MODEL)
    xs = x_sample.reshape(DEC_BATCH * DEC_SEQ, D_MODEL)
    new_k, new_v, new_re, new_im, new_hg = [], [], [], [], []
    for l in range(DEPTH):
        xp, ctx = _layer(xp, l, w, SEQ, 0, None)
        new_k.append(ctx[0].reshape(BATCH, SEQ, N_KV_HEADS, HEAD_DIM))
        new_v.append(ctx[1].reshape(BATCH, SEQ, N_KV_HEADS, HEAD_DIM))
        fin = ctx[2].reshape(BATCH // SUBLANES, SSM_GROUPS // ng, ng, 2, SUBLANES, 2, SSM_STATE)
        fin = fin.transpose(3, 0, 4, 5, 1, 2, 6).reshape(2, BATCH, 2, SSM_GROUPS, SSM_STATE)
        new_re.append(fin[0])
        new_im.append(fin[1])
        new_hg.append(ctx[3])
        xs, _ = _layer(xs, l, w, DEC_SEQ, 1, (ck, cv, h0, state_hgrn))
    return (xp.reshape(BATCH, SEQ, D_MODEL), xs.reshape(DEC_BATCH, DEC_SEQ, D_MODEL),
            jnp.stack(new_k, axis=1), jnp.stack(new_v, axis=1), jnp.stack(new_re, axis=1),
            jnp.stack(new_im, axis=1), jnp.stack(new_hg, axis=1))
```

```python
import functools
import math

import numpy as np
import jax
import jax.numpy as jnp
from jax import lax
from jax.experimental import pallas as pl
from jax.experimental.pallas import tpu as pltpu

F32 = jnp.float32
BF16 = jnp.bfloat16

D_MODEL = 1024
BATCH = 32
SEQ = 256
DEPTH = 4
DEC_BATCH = 4
DEC_SEQ = 1024
PAST_LEN = 512
GRID_W = 64
N_HEADS = 8
N_KV_HEADS = 2
HEAD_DIM = 64
KV_GROUP = N_HEADS // N_KV_HEADS
WINDOW = 128
LAT_QBLOCK = 128
ROPE_BASE = 10000.0
SSM_WIDTH = 512
SSM_GROUP = 16
SSM_GROUPS = SSM_WIDTH // SSM_GROUP
SSM_STATE = 64
HG_HEADS = 4
HG_KDIM = 128
HG_VDIM = 128
N_BRANCHES = 3
N_GROUPS = 4
EXPERTS_PER_GROUP = 4
N_EXPERTS = N_GROUPS * EXPERTS_PER_GROUP
EXPERT_FF = 256
EPS = 1e-6
Q_W = N_HEADS * HEAD_DIM
KV_W = N_KV_HEADS * HEAD_DIM
HG_KW = HG_HEADS * HG_KDIM
HG_VW = HG_HEADS * HG_VDIM

LANES = 128
SUBLANES = 8
VMEM_LIMIT = 56 * 1024 * 1024

MG_W = N_BRANCHES * D_MODEL
COL_HG = 0
COL_AQ = COL_HG + 3 * HG_KW + 2 * HG_VW
COL_AK = COL_AQ + Q_W
COL_SU = COL_AK + 2 * KV_W
P_WIDTH = COL_SU + SSM_WIDTH
IN_WIDTH = P_WIDTH
INPROJ_STEPS = 3

SSM_T = 16
SSM_PREP_GROUPS = 4
HG_CHUNK = 128
HG_LEVELS = 7
HG_CHAINS = 8
HG_MXU_LEVELS = (1, 2)
ROUTER_OFF = N_GROUPS
MASK_NEG = -1e30


def _cparams(sem):
    return pltpu.CompilerParams(dimension_semantics=sem, vmem_limit_bytes=VMEM_LIMIT)


def _sigmoid(x):
    return 1.0 / (1.0 + jnp.exp(-x))


def _silu(x):
    return x * _sigmoid(x)


def _gelu_tanh(x):
    return 0.5 * x * (1.0 + jnp.tanh(math.sqrt(2.0 / math.pi) * (x + 0.044715 * (x * x * x))))


def _dot(a, b):
    return jnp.dot(a, b, preferred_element_type=F32)


def _dot_nt(a, b):
    return lax.dot_general(a, b, (((1,), (1,)), ((), ())), preferred_element_type=F32)


def _dot_tn(a, b):
    return lax.dot_general(a, b, (((0,), (0,)), ((), ())), preferred_element_type=F32)


def _shift_lanes(a, s):
    if s == 0:
        return a
    lo, hi = a[:, :LANES], a[:, LANES:]
    zero = jnp.zeros_like(lo)
    lane = lax.broadcasted_iota(jnp.int32, lo.shape, 1)
    if s > 0:
        if s >= LANES:
            t = s - LANES
            out_hi = lo if t == 0 else jnp.where(lane >= t, pltpu.roll(lo, t, 1), 0.0)
            return jnp.concatenate([zero, out_hi], axis=1)
        r_lo, r_hi = pltpu.roll(lo, s, 1), pltpu.roll(hi, s, 1)
        return jnp.concatenate([jnp.where(lane >= s, r_lo, 0.0), jnp.where(lane >= s, r_hi, r_lo)], axis=1)
    s = -s
    if s >= LANES:
        t = s - LANES
        out_lo = hi if t == 0 else jnp.where(lane < LANES - t, pltpu.roll(hi, LANES - t, 1), 0.0)
        return jnp.concatenate([out_lo, zero], axis=1)
    r_lo, r_hi = pltpu.roll(lo, LANES - s, 1), pltpu.roll(hi, LANES - s, 1)
    keep = lane < LANES - s
    return jnp.concatenate([jnp.where(keep, r_lo, r_hi), jnp.where(keep, r_hi, 0.0)], axis=1)


def _ssm_prep_kernel(*refs):
    for g in range(SSM_PREP_GROUPS):
        _ssm_prep_group(*[r.at[g] for r in refs])


def _ssm_prep_group(are_ref, aim_ref, ldt_ref, btre_ref, btim_ref, cre_ref, cim_ref,
                    w_ref, bst_ref, cst_ref, lamt_ref):
    t_blk = SSM_T
    ns = SSM_STATE
    hp = lax.Precision.HIGHEST
    jf = lax.broadcasted_iota(jnp.int32, (2 * t_blk, ns), 0).astype(F32)
    bt_re = btre_ref[...]
    bt_im = btim_ref[...]
    c_re = cre_ref[...]
    c_im = cim_ref[...]
    p_re, p_im, bb_re, bb_im, kt = [], [], [], [], []
    for d in range(2):
        a_re = are_ref[d:d + 1, :]
        a_im = aim_ref[d:d + 1, :]
        dt = jnp.exp(ldt_ref[d:d + 1, :])
        mag = jnp.exp(jf * (a_re * dt))
        ang = jf * (a_im * dt)
        p_re.append(mag * jnp.cos(ang))
        p_im.append(mag * jnp.sin(ang))
        nr = p_re[d][1:2] - 1.0
        ni = p_im[d][1:2]
        den = a_re * a_re + a_im * a_im
        f_re = (nr * a_re + ni * a_im) / den
        f_im = (ni * a_re - nr * a_im) / den
        bb_re.append(f_re * bt_re - f_im * bt_im)
        bb_im.append(f_re * bt_im + f_im * bt_re)
        lags = range(t_blk) if d == 0 else range(t_blk - 1, -1, -1)
        a_parts_re = [c_re * p_re[d][j:j + 1] - c_im * p_im[d][j:j + 1] for j in lags]
        a_parts_im = [c_re * p_im[d][j:j + 1] + c_im * p_re[d][j:j + 1] for j in lags]
        nt = (((1,), (1,)), ((), ()))
        kt.append(lax.dot_general(bb_re[d], jnp.concatenate(a_parts_re, axis=0), nt,
                                  precision=hp, preferred_element_type=F32)
                  - lax.dot_general(bb_im[d], jnp.concatenate(a_parts_im, axis=0), nt,
                                    precision=hp, preferred_element_type=F32))
        lamt_ref[0:1, d * ns:(d + 1) * ns] = p_re[d][t_blk:t_blk + 1]
        lamt_ref[1:2, d * ns:(d + 1) * ns] = p_im[d][t_blk:t_blk + 1]
    for t in range(t_blk):
        rows = slice(t * SSM_GROUP, (t + 1) * SSM_GROUP)
        w_ref[rows, :] = (_shift_lanes(kt[0], t * SSM_GROUP)
                          + _shift_lanes(kt[1], -(t_blk - 1 - t) * SSM_GROUP)).astype(BF16)
        ef, eb = t_blk - 1 - t, t
        bst = [bb_re[0] * p_re[0][ef:ef + 1] - bb_im[0] * p_im[0][ef:ef + 1],
               bb_re[1] * p_re[1][eb:eb + 1] - bb_im[1] * p_im[1][eb:eb + 1],
               bb_re[0] * p_im[0][ef:ef + 1] + bb_im[0] * p_re[0][ef:ef + 1],
               bb_re[1] * p_im[1][eb:eb + 1] + bb_im[1] * p_re[1][eb:eb + 1]]
        bst_ref[rows, :] = jnp.concatenate(bst, axis=1).astype(BF16)
        ef, eb = t + 1, t_blk - t
        cst = [c_re * p_re[0][ef:ef + 1] - c_im * p_im[0][ef:ef + 1],
               c_re * p_re[1][eb:eb + 1] - c_im * p_im[1][eb:eb + 1],
               -(c_re * p_im[0][ef:ef + 1] + c_im * p_re[0][ef:ef + 1]),
               -(c_re * p_im[1][eb:eb + 1] + c_im * p_re[1][eb:eb + 1])]
        cst_ref[rows, :] = jnp.concatenate(cst, axis=1).astype(BF16)


def _ssm_prep(ssm_a_re, ssm_a_im, ssm_log_dt, ssm_b_re, ssm_b_im, ssm_c_re, ssm_c_im, ssm_d):
    t_blk = SSM_T
    bw = t_blk * SSM_GROUP
    lgd = lambda a: a.transpose(0, 2, 1, 3)
    ldt = jnp.broadcast_to(ssm_log_dt[..., None], ssm_a_re.shape)
    per = SSM_PREP_GROUPS
    vec = pl.BlockSpec((None, per, 2, SSM_STATE), lambda l, g: (l, g, 0, 0))
    mat = pl.BlockSpec((None, per, SSM_GROUP, SSM_STATE), lambda l, g: (l, g, 0, 0))
    op = lambda width: pl.BlockSpec((None, per, bw, width), lambda l, g: (l, g, 0, 0))
    w, bst, cst, lamt = pl.pallas_call(
        _ssm_prep_kernel,
        grid=(DEPTH, SSM_GROUPS // per),
        in_specs=[vec, vec, vec, mat, mat, mat, mat],
        out_specs=[op(bw), op(4 * SSM_STATE), op(4 * SSM_STATE),
                   pl.BlockSpec((None, per, 2, 2 * SSM_STATE), lambda l, g: (l, g, 0, 0))],
        out_shape=[
            jax.ShapeDtypeStruct((DEPTH, SSM_GROUPS, bw, bw), BF16),
            jax.ShapeDtypeStruct((DEPTH, SSM_GROUPS, bw, 4 * SSM_STATE), BF16),
            jax.ShapeDtypeStruct((DEPTH, SSM_GROUPS, bw, 4 * SSM_STATE), BF16),
            jax.ShapeDtypeStruct((DEPTH, SSM_GROUPS, 2, 2 * SSM_STATE), F32),
        ],
        compiler_params=_cparams(("parallel", "parallel")),
        name="ssm_prep",
    )(lgd(ssm_a_re), lgd(ssm_a_im), lgd(ldt), jnp.swapaxes(ssm_b_re, -1, -2), jnp.swapaxes(ssm_b_im, -1, -2),
      ssm_c_re, ssm_c_im)
    d_row = jnp.tile(ssm_d, (1, 1, t_blk)).reshape(DEPTH, SSM_GROUPS, 1, bw)
    return dict(w=w, bst=bst, cst=cst, lamt=lamt, d=d_row)


def _lb_kernel(x_ref, o_ref):
    x = x_ref[...]
    m = jnp.max(x, axis=0, keepdims=True)
    e = jnp.exp(x - m)
    s = e / jnp.sum(e, axis=0, keepdims=True)
    run = jnp.zeros_like(s[0:1])
    o_ref[0:1, :] = run
    for l in range(1, DEPTH):
        run = run + s[l:l + 1]
        o_ref[l:l + 1, :] = run


def _lower_bounds(hg_lb):
    w = 2 * HG_KW
    out = pl.pallas_call(
        _lb_kernel,
        out_shape=jax.ShapeDtypeStruct((DEPTH, w), F32),
        name="hgrn_lower_bounds",
    )(hg_lb.reshape(DEPTH, w))
    return out.reshape(DEPTH, 2, HG_HEADS, HG_KDIM).transpose(0, 2, 1, 3)


def _mod_kernel(c_ref, w_ref, b_ref, o_ref):
    c = c_ref[...]
    a = _silu(c).astype(BF16)
    o_ref[...] = _dot(a, w_ref[...].astype(BF16)) + b_ref[...]


def _modulation(c, c_ctx, w_mod, b_mod):
    rows = SUBLANES
    cond = jnp.concatenate([c_ctx[None, :], c, jnp.zeros((rows - 1 - DEC_BATCH, D_MODEL), F32)], axis=0)
    tn = D_MODEL
    out = pl.pallas_call(
        _mod_kernel,
        grid=(DEPTH, 6),
        in_specs=[
            pl.BlockSpec((rows, D_MODEL), lambda l, j: (0, 0)),
            pl.BlockSpec((None, D_MODEL, tn), lambda l, j: (l, 0, j)),
            pl.BlockSpec((None, 1, tn), lambda l, j: (l, 0, j)),
        ],
        out_specs=pl.BlockSpec((None, rows, tn), lambda l, j: (l, 0, j)),
        out_shape=jax.ShapeDtypeStruct((DEPTH, rows, 6 * D_MODEL), F32),
        compiler_params=_cparams(("parallel", "parallel")),
        name="modulation",
    )(cond, w_mod, b_mod.reshape(DEPTH, 1, 6 * D_MODEL))
    return out.reshape(DEPTH, rows, 6, D_MODEL)


def _inproj_kernel(x_ref, mod_ref, g_ref, wmg_ref, wp_ref, mg_ref, p_ref, h_scr):
    @pl.when(pl.program_id(1) == 0)
    def _():
        x = x_ref[...]
        ms = jnp.mean(x * x, axis=-1, keepdims=True)
        y = x * lax.rsqrt(ms + EPS) * g_ref[...]
        h = y * (1.0 + mod_ref[1:2, :]) + mod_ref[0:1, :]
        h_scr[...] = h.astype(BF16)

    h = h_scr[...]
    mg_ref[...] = _dot(h, wmg_ref[...]).astype(BF16)
    p_ref[...] = _dot(h, wp_ref[...])


def _in_proj(x, mod, norm_g, w_mg, w_p, layer, seq_len, cond_row0):
    n = x.shape[0]
    tm = 1024
    tn_mg = MG_W // INPROJ_STEPS
    tn_p = P_WIDTH // INPROJ_STEPS
    rows_per_cond = seq_len if cond_row0 else n
    mod_idx = (lambda i, j: (layer, cond_row0 + (i * tm) // rows_per_cond, 0, 0))
    return pl.pallas_call(
        _inproj_kernel,
        grid=(n // tm, INPROJ_STEPS),
        in_specs=[
            pl.BlockSpec((tm, D_MODEL), lambda i, j: (i, 0)),
            pl.BlockSpec((None, None, 6, D_MODEL), mod_idx),
            pl.BlockSpec((None, 1, D_MODEL), lambda i, j: (layer, 0, 0)),
            pl.BlockSpec((None, D_MODEL, tn_mg), lambda i, j: (layer, 0, j)),
            pl.BlockSpec((None, D_MODEL, tn_p), lambda i, j: (layer, 0, j)),
        ],
        out_specs=[pl.BlockSpec((tm, tn_mg), lambda i, j: (i, j)), pl.BlockSpec((tm, tn_p), lambda i, j: (i, j))],
        out_shape=[jax.ShapeDtypeStruct((n, MG_W), BF16), jax.ShapeDtypeStruct((n, P_WIDTH), F32)],
        scratch_shapes=[pltpu.VMEM((tm, D_MODEL), BF16)],
        compiler_params=_cparams(("parallel", "arbitrary")),
        name="in_proj",
    )(x, mod, norm_g, w_mg, w_p)


def _pair_norm(x, g):
    xx = x * x
    s_a = jnp.sum(xx[:, :HEAD_DIM], axis=-1, keepdims=True)
    s_b = jnp.sum(xx[:, HEAD_DIM:], axis=-1, keepdims=True)
    lane = lax.broadcasted_iota(jnp.int32, x.shape, 1)
    ms = jnp.where(lane < HEAD_DIM, s_a, s_b) * (1.0 / HEAD_DIM)
    return x * lax.rsqrt(ms + EPS) * g


def _rope(x, cos, sin_signed):
    lane = lax.broadcasted_iota(jnp.int32, x.shape, 1)
    first = (lane & 31) < 16
    partner = jnp.where(first, pltpu.roll(x, LANES - 16, 1), pltpu.roll(x, 16, 1))
    return x * cos + partner * sin_signed


def _sink_column(sink_ref, layer, kh, t):
    row = lax.broadcasted_iota(jnp.int32, (KV_GROUP * t, 1), 0)
    col = jnp.full((KV_GROUP * t, 1), sink_ref[layer, kh * KV_GROUP], F32)
    for g in range(1, KV_GROUP):
        col = jnp.where(row >= g * t, sink_ref[layer, kh * KV_GROUP + g], col)
    return col


def _sink_softmax_av(heads):
    ms = []
    for scores, _, sink_col in heads:
        m = sink_col
        for s in scores:
            m = jnp.maximum(m, jnp.max(s, axis=-1, keepdims=True))
        ms.append(m)
    ps = [[jnp.exp(s - m) for s in scores] for (scores, _, _), m in zip(heads, ms)]
    outs = []
    for (scores, values, sink_col), m, p_list in zip(heads, ms, ps):
        den = jnp.exp(sink_col - m)
        o = None
        for p, v in zip(p_list, values):
            den = den + jnp.sum(p, axis=-1, keepdims=True)
            pv = _dot(p.astype(BF16), v.astype(BF16))
            o = pv if o is None else o + pv
        outs.append(o / den)
    return outs


def _attn_ctx_kernel(sink_ref, q_ref, kv_ref, gq_ref, gk_ref, att_ref, k_ref, v_ref, *, layer):
    t = q_ref.shape[0]
    kv = kv_ref[...]
    k = _pair_norm(kv[:, :KV_W], gk_ref[...])
    v = kv[:, KV_W:]
    k_ref[...] = k
    v_ref[...] = v
    q = q_ref[...]
    scale = 1.0 / math.sqrt(HEAD_DIM)
    work = []
    for kh in range(N_KV_HEADS):
        heads = []
        for j in range(KV_GROUP // 2):
            c0 = (kh * (KV_GROUP // 2) + j) * LANES
            qn = _pair_norm(q[:, c0:c0 + LANES], gq_ref[...])
            heads += [qn[:, :HEAD_DIM], qn[:, HEAD_DIM:]]
        qs = jnp.concatenate(heads, axis=0).astype(BF16)
        sl = slice(kh * HEAD_DIM, (kh + 1) * HEAD_DIM)
        s = _dot_nt(qs, k[:, sl].astype(BF16)) * scale
        work.append(([s], [v[:, sl]], _sink_column(sink_ref, layer, kh, t)))
    outs = [o[g * t:(g + 1) * t] for o in _sink_softmax_av(work) for g in range(KV_GROUP)]
    att_ref[...] = jnp.concatenate(outs, axis=-1).astype(BF16)


def _attn_ctx(sink, p, gq, gk, layer):
    n = p.shape[0]
    t = SEQ
    kern = functools.partial(_attn_ctx_kernel, layer=layer)
    return pl.pallas_call(
        kern,
        grid=(n // t,),
        in_specs=[
            pl.BlockSpec(memory_space=pltpu.SMEM),
            pl.BlockSpec((t, Q_W), lambda b: (b, COL_AQ // Q_W)),
            pl.BlockSpec((t, 2 * KV_W), lambda b: (b, COL_AK // (2 * KV_W))),
            pl.BlockSpec((None, 1, LANES), lambda b: (layer, 0, 0)),
            pl.BlockSpec((None, 1, LANES), lambda b: (layer, 0, 0)),
        ],
        out_specs=[
            pl.BlockSpec((t, Q_W), lambda b: (b, 0)),
            pl.BlockSpec((t, KV_W), lambda b: (b, 0)),
            pl.BlockSpec((t, KV_W), lambda b: (b, 0)),
        ],
        out_shape=[
            jax.ShapeDtypeStruct((n, Q_W), BF16),
            jax.ShapeDtypeStruct((n, KV_W), F32),
            jax.ShapeDtypeStruct((n, KV_W), F32),
        ],
        compiler_params=_cparams(("parallel",)),
        name="attn_ctx",
    )(sink, p, p, gq, gk)


def _attn_lat_kernel(sink_ref, q_ref, kv_ref, kc_ref, vc_ref, gq_ref, gk_ref, cos_ref, sin_ref, att_ref, *, layer):
    t = LAT_QBLOCK
    span = t + 2 * WINDOW
    start = pl.program_id(1) * t
    ks = pl.multiple_of(jnp.clip(start - WINDOW, 0, DEC_SEQ - span), WINDOW)
    q0 = pl.multiple_of(start, t)
    kvw = kv_ref[pl.ds(ks, span), :]
    k = _rope(_pair_norm(kvw[:, :KV_W], gk_ref[...]), cos_ref[pl.ds(ks, span), :], sin_ref[pl.ds(ks, span), :])
    v = kvw[:, KV_W:]
    cos_q = cos_ref[pl.ds(q0, t), :]
    sin_q = sin_ref[pl.ds(q0, t), :]
    qpos = start + (lax.broadcasted_iota(jnp.int32, (KV_GROUP * t, span), 0) & (t - 1))
    kpos = ks + lax.broadcasted_iota(jnp.int32, (KV_GROUP * t, span), 1)
    band = jnp.abs(qpos - kpos) <= WINDOW
    q = q_ref[...]
    kc = kc_ref[...]
    vc = vc_ref[...]
    scale = 1.0 / math.sqrt(HEAD_DIM)
    work = []
    for kh in range(N_KV_HEADS):
        heads = []
        for j in range(KV_GROUP // 2):
            c0 = (kh * (KV_GROUP // 2) + j) * LANES
            qn = _rope(_pair_norm(q[:, c0:c0 + LANES], gq_ref[...]), cos_q, sin_q)
            heads += [qn[:, :HEAD_DIM], qn[:, HEAD_DIM:]]
        qs = jnp.concatenate(heads, axis=0).astype(BF16)
        sl = slice(kh * HEAD_DIM, (kh + 1) * HEAD_DIM)
        s_w = jnp.where(band, _dot_nt(qs, k[:, sl].astype(BF16)) * scale, MASK_NEG)
        s_c = _dot_nt(qs, kc[:, sl].astype(BF16)) * scale
        work.append(([s_w, s_c], [v[:, sl], vc[:, sl]], _sink_column(sink_ref, layer, kh, t)))
    outs = [o[g * t:(g + 1) * t] for o in _sink_softmax_av(work) for g in range(KV_GROUP)]
    att_ref[...] = jnp.concatenate(outs, axis=-1).astype(BF16)


def _attn_lat(sink, p, cache_k, cache_v, gq, gk, cos_t, sin_t, layer):
    n = p.shape[0]
    t = LAT_QBLOCK
    nb = DEC_SEQ // t
    kern = functools.partial(_attn_lat_kernel, layer=layer)
    return pl.pallas_call(
        kern,
        grid=(DEC_BATCH, nb),
        in_specs=[
            pl.BlockSpec(memory_space=pltpu.SMEM),
            pl.BlockSpec((t, Q_W), lambda b, j: (b * nb + j, COL_AQ // Q_W)),
            pl.BlockSpec((DEC_SEQ, 2 * KV_W), lambda b, j: (b, COL_AK // (2 * KV_W))),
            pl.BlockSpec((None, None, PAST_LEN, KV_W), lambda b, j: (b, layer, 0, 0)),
            pl.BlockSpec((None, None, PAST_LEN, KV_W), lambda b, j: (b, layer, 0, 0)),
            pl.BlockSpec((None, 1, LANES), lambda b, j: (layer, 0, 0)),
            pl.BlockSpec((None, 1, LANES), lambda b, j: (layer, 0, 0)),
            pl.BlockSpec((DEC_SEQ, LANES), lambda b, j: (0, 0)),
            pl.BlockSpec((DEC_SEQ, LANES), lambda b, j: (0, 0)),
        ],
        out_specs=pl.BlockSpec((t, Q_W), lambda b, j: (b * nb + j, 0)),
        out_shape=jax.ShapeDtypeStruct((n, Q_W), BF16),
        compiler_params=_cparams(("parallel", "parallel")),
        name="attn_lat",
    )(sink, p, p, cache_k, cache_v, gq, gk, cos_t, sin_t)


def _rope_tables():
    pos = jnp.arange(DEC_SEQ)
    rows = (pos // GRID_W).astype(F32)
    cols = (pos % GRID_W).astype(F32)
    half = HEAD_DIM // 2
    freqs = ROPE_BASE ** (-jnp.arange(0, half, 2, dtype=F32) / half)
    ang_r = rows[:, None] * freqs[None, :]
    ang_c = cols[:, None] * freqs[None, :]
    ang = jnp.concatenate([ang_r, ang_r, ang_c, ang_c], axis=-1)
    sign = jnp.tile(jnp.concatenate([-jnp.ones((16,), F32), jnp.ones((16,), F32)]), 2)
    cos_t = jnp.tile(jnp.cos(ang), (1, LANES // HEAD_DIM))
    sin_t = jnp.tile(jnp.sin(ang) * sign[None, :], (1, LANES // HEAD_DIM))
    return cos_t, sin_t


def _ssm_kernel(*refs, nseq, seq_len, has_h0):
    if has_h0:
        x_ref, w_ref, bst_ref, cst_ref, lamt_ref, d_ref, h0_ref, z_ref, loc_scr, ent_scr, cols_scr = refs
    else:
        x_ref, w_ref, bst_ref, cst_ref, lamt_ref, d_ref, z_ref, fin_ref, loc_scr, ent_scr, cols_scr = refs
    ns = SSM_STATE
    t_blk = SSM_T
    nblk = seq_len // t_blk
    ng = LANES // SSM_GROUP
    rows = nblk * SUBLANES
    per_half = LANES // SSM_GROUP
    if nseq < SUBLANES:
        cols_scr[...] = jnp.zeros_like(cols_scr)
    for t in range(t_blk):
        for b in range(nblk):
            cols_scr[t, b * SUBLANES:b * SUBLANES + nseq, :] = x_ref[pl.ds(b * t_blk + t, nseq, stride=seq_len), :]
    cols = [cols_scr[t] for t in range(t_blk)]
    chunk = lax.broadcasted_iota(jnp.int32, (rows, LANES), 1) // SSM_GROUP
    is_fwd = lax.broadcasted_iota(jnp.int32, (SUBLANES, 2 * ns), 1) < ns

    def roll(a, chunks):
        shift = (chunks * SSM_GROUP) % LANES
        return a if shift == 0 else pltpu.roll(a, shift, 1)

    xg = []
    for g in range(ng):
        halves = []
        for h in range(t_blk // per_half):
            acc = None
            for j in range(per_half):
                r = roll(cols[h * per_half + j], j - g)
                acc = r if acc is None else jnp.where(chunk == j, r, acc)
            halves.append(acc)
        xg.append(jnp.concatenate(halves, axis=1))
        loc_scr[g] = _dot(xg[g].astype(BF16), bst_ref[g])
    l_re = [jnp.broadcast_to(lamt_ref[g, 0:1, :], (SUBLANES, 2 * ns)) for g in range(ng)]
    l_im = [jnp.broadcast_to(lamt_ref[g, 1:2, :], (SUBLANES, 2 * ns)) for g in range(ng)]
    if has_h0:
        s_re = [h0_ref[g, 0] for g in range(ng)]
        s_im = [h0_ref[g, 1] for g in range(ng)]
    else:
        s_re = [jnp.zeros((SUBLANES, 2 * ns), F32) for _ in range(ng)]
        s_im = [jnp.zeros((SUBLANES, 2 * ns), F32) for _ in range(ng)]
    for j in range(nblk):
        rf = slice(j * SUBLANES, (j + 1) * SUBLANES)
        rb = slice((nblk - 1 - j) * SUBLANES, (nblk - j) * SUBLANES)
        for g in range(ng):
            ent_scr[g, rf, 0:ns] = s_re[g][:, 0:ns]
            ent_scr[g, rb, ns:2 * ns] = s_re[g][:, ns:2 * ns]
            ent_scr[g, rf, 2 * ns:3 * ns] = s_im[g][:, 0:ns]
            ent_scr[g, rb, 3 * ns:4 * ns] = s_im[g][:, ns:2 * ns]
            loc_re = jnp.where(is_fwd, loc_scr[g, rf, 0:2 * ns], loc_scr[g, rb, 0:2 * ns])
            loc_im = jnp.where(is_fwd, loc_scr[g, rf, 2 * ns:4 * ns], loc_scr[g, rb, 2 * ns:4 * ns])
            n_re = l_re[g] * s_re[g] - l_im[g] * s_im[g] + loc_re
            n_im = l_re[g] * s_im[g] + l_im[g] * s_re[g] + loc_im
            s_re[g], s_im[g] = n_re, n_im
    if not has_h0:
        for g in range(ng):
            fin_ref[g, 0] = s_re[g]
            fin_ref[g, 1] = s_im[g]
    out_cols = [None] * t_blk
    for g in range(ng):
        y = (_dot(xg[g].astype(BF16), w_ref[g]) + _dot_nt(ent_scr[g].astype(BF16), cst_ref[g])
             + d_ref[g] * xg[g])
        zg = _gelu_tanh(y)
        for h in range(t_blk // per_half):
            zh = zg[:, h * LANES:(h + 1) * LANES]
            for j in range(per_half):
                t = h * per_half + j
                r = roll(zh, g - j)
                out_cols[t] = r if out_cols[t] is None else jnp.where(chunk == g, r, out_cols[t])
    for t in range(t_blk):
        for b in range(nblk):
            z_ref[pl.ds(b * t_blk + t, nseq, stride=seq_len), :] = out_cols[t][b * SUBLANES:b * SUBLANES + nseq]


def _ssm(p, sw, layer, nseq, seq_len, h0=None):
    n = p.shape[0]
    t_blk = SSM_T
    nblk = seq_len // t_blk
    bw = t_blk * SSM_GROUP
    nsg = SSM_WIDTH // LANES
    ng = LANES // SSM_GROUP
    per_step = min(nseq, SUBLANES)
    has_h0 = h0 is not None
    kern = functools.partial(_ssm_kernel, nseq=per_step, seq_len=seq_len, has_h0=has_h0)
    wspec = lambda a, b: pl.BlockSpec((None, None, ng, a, b), lambda s, g: (layer, g, 0, 0, 0))
    grp = lambda a: a.reshape((DEPTH, nsg, ng) + a.shape[2:])
    tok = per_step * seq_len
    in_specs = [pl.BlockSpec((tok, LANES), lambda s, g: (s, COL_SU // LANES + g)),
                wspec(bw, bw), wspec(bw, 4 * SSM_STATE), wspec(bw, 4 * SSM_STATE),
                wspec(2, 2 * SSM_STATE), wspec(1, bw)]
    args = [p, grp(sw['w']), grp(sw['bst']), grp(sw['cst']), grp(sw['lamt']), grp(sw['d'])]
    out_specs = [pl.BlockSpec((tok, LANES), lambda s, g: (s, g))]
    out_shape = [jax.ShapeDtypeStruct((n, SSM_WIDTH), F32)]
    st_block = (ng, 2, SUBLANES, 2 * SSM_STATE)
    if has_h0:
        in_specs.append(pl.BlockSpec((None, None) + st_block, lambda s, g: (layer, g, 0, 0, 0, 0)))
        args.append(h0)
    else:
        out_specs.append(pl.BlockSpec((None, None) + st_block, lambda s, g: (s, g, 0, 0, 0, 0)))
        out_shape.append(jax.ShapeDtypeStruct((nseq // per_step, nsg) + st_block, F32))
    rows = nblk * SUBLANES
    return pl.pallas_call(
        kern,
        grid=(nseq // per_step, nsg),
        in_specs=in_specs,
        out_specs=out_specs,
        out_shape=out_shape,
        scratch_shapes=[pltpu.VMEM((ng, rows, 4 * SSM_STATE), F32), pltpu.VMEM((ng, rows, 4 * SSM_STATE), F32),
                        pltpu.VMEM((t_blk, rows, LANES), F32)],
        compiler_params=_cparams(("parallel", "parallel")),
        name="ssm",
    )(*args)


def _hgrn_consts():
    c = HG_CHUNK
    i = np.arange(c)[:, None]
    r = np.arange(c)[None, :]
    m_all = np.zeros((2, HG_LEVELS, c, c), np.float32)
    lev_of = np.full((2, c, c), -1, np.int32)
    for lev in range(HG_LEVELS):
        n = 2 << lev
        half = n // 2
        mid = (i // n) * n + half
        upper = (i % n) >= half
        m_all[0, lev] = np.where(upper, (r >= mid) & (r <= i), (r > i) & (r < mid))
        m_all[1, lev] = np.where(upper, (r >= mid) & (r < i), (r >= i) & (r < mid))
        same = (i // n) == (r // n)
        r_upper = (r % n) >= half
        lev_of[0][same & upper & ~r_upper] = lev
        lev_of[1][same & ~upper & r_upper] = lev
    cum = np.stack([r <= i, r >= i]).astype(np.float32)
    m_small = np.concatenate([cum] + [m_all[:, lev] for lev in HG_MXU_LEVELS], axis=1)
    return (jnp.asarray(np.concatenate([m_small, m_small], axis=-1), BF16), jnp.asarray(lev_of))


def _hgrn_chunks(chains, states, m_ref, lev_ref):
    c = HG_CHUNK
    nb = c // SUBLANES
    w = []
    for d, key, r0, hq_ref, f_ref, hi_ref, lb, cum_ref in chains:
        q = _silu(hq_ref[pl.ds(r0, c), :])
        fg = lb + (1.0 - lb) * _sigmoid(f_ref[pl.ds(r0, c), :])
        k = 1.0 - fg
        log2f = jnp.log(fg) * (1.0 / math.log(2.0))
        v = hi_ref[pl.ds(r0, c), :]
        hi = log2f.astype(BF16)
        lo = (log2f - hi.astype(F32)).astype(BF16)
        sums = _dot(m_ref[d], jnp.concatenate([hi, lo], axis=0))
        w.append(dict(d=d, key=key, q=q, fg=fg, k=k, v=v, vb=v.astype(BF16), sums=sums, cum=sums[0:c],
                      cum_ref=cum_ref, att=[None] * nb))
    for x in w:
        x['cum_ref'][...] = x['cum']

    def select_into(x, rows, lev, a):
        att = x['att']
        for j in range(rows.stop // SUBLANES - rows.start // SUBLANES):
            i = rows.start // SUBLANES + j
            blk = a[j * SUBLANES:(j + 1) * SUBLANES]
            mask = lev_ref[x['d'], i * SUBLANES:(i + 1) * SUBLANES, :] == lev
            att[i] = jnp.where(mask, blk, 0.0 if att[i] is None else att[i])

    full = slice(0, c)
    prods = [_dot_nt((x['q'] * x['fg']).astype(BF16), x['k'].astype(BF16)) for x in w]
    for x, a in zip(w, prods):
        select_into(x, full, 0, a)
    for n_lev, lev in enumerate(HG_MXU_LEVELS):
        prods = []
        for x in w:
            g = jnp.exp2(x['sums'][(n_lev + 1) * c:(n_lev + 2) * c])
            prods.append(_dot_nt((x['q'] * g).astype(BF16), (x['k'] * g).astype(BF16)))
        for x, a in zip(w, prods):
            select_into(x, full, lev, a)
    for lev in range(HG_MXU_LEVELS[-1] + 1, HG_LEVELS):
        half = 1 << lev
        n = 2 * half
        prods = []
        for x in w:
            d, q, k, cum = x['d'], x['q'], x['k'], x['cum']
            q_parts, k_parts, q_rows = [], [], []
            for j in range(c // n):
                lower = slice(j * n, j * n + half)
                upper = slice(j * n + half, (j + 1) * n)
                qr, kr = (upper, lower) if d == 0 else (lower, upper)
                ref_row = j * n + half - 1 if d == 0 else j * n + half
                ref = jnp.broadcast_to(x['cum_ref'][ref_row:ref_row + 1, :], (half, HG_KDIM))
                q_parts.append(q[qr] * jnp.exp2(cum[qr] - ref))
                k_scaled = k[kr] * jnp.exp2(ref - cum[kr])
                k_parts += [k_scaled, k[qr]] if d == 0 else [k[qr], k_scaled]
                q_rows.append(qr)
            prods.append((q_rows, _dot_nt(jnp.concatenate(q_parts, axis=0).astype(BF16),
                                          jnp.concatenate(k_parts, axis=0).astype(BF16))))
        for x, (q_rows, a) in zip(w, prods):
            for j, qr in enumerate(q_rows):
                select_into(x, qr, lev, a[j * half:(j + 1) * half])
    for x in w:
        d, q, k, cum = x['d'], x['q'], x['k'], x['cum']
        last = c - 1 if d == 0 else 0
        tot_row = cum[last:last + 1, :]
        x['lhs'] = jnp.concatenate([jnp.concatenate(x['att'], axis=0).astype(BF16),
                                    (q * jnp.exp2(cum)).astype(BF16)], axis=1)
        x['kg'] = (k * jnp.exp2(tot_row - cum)).astype(BF16)
        x['dv'] = jnp.sum(q * k, axis=-1, keepdims=True) * x['v']
        x['tot_col'] = jnp.broadcast_to(jnp.exp2(tot_row), (c, HG_KDIM)).T
    states = list(states)
    outs = []
    for x in w:
        st = states[x['key']]
        outs.append(x['dv'] + _dot(x['lhs'], jnp.concatenate([x['vb'], st.astype(BF16)], axis=0)))
        states[x['key']] = st * x['tot_col'] + _dot_tn(x['kg'], x['vb'])
    return outs, states


def _hgrn_kernel(*refs, seq_len, nh, unroll, has_s0):
    if has_s0:
        (hq_ref, ff_ref, fb_ref, hi_ref, hgt_ref, lb_ref, ng_ref, m_ref, lev_ref, s0_ref,
         o_ref, of_scr, ob_scr, cum_scr) = refs
    else:
        (hq_ref, ff_ref, fb_ref, hi_ref, hgt_ref, lb_ref, ng_ref, m_ref, lev_ref,
         o_ref, sfin_ref, of_scr, ob_scr, cum_scr) = refs
    c = HG_CHUNK
    nch = seq_len // c
    head = lambda ref, h: ref.at[:, h * HG_KDIM:(h + 1) * HG_KDIM]

    def step(ci, carry):
        chains, rows = [], []
        for u in range(unroll):
            cf = ci * unroll + u
            rf = pl.multiple_of(cf * c, c)
            rb = pl.multiple_of((nch - 1 - cf) * c, c)
            for h in range(nh):
                hq, hi = head(hq_ref, h), head(hi_ref, h)
                slot = (u * nh + h) * 2
                chains.append((0, 2 * h, rf, hq, head(ff_ref, h), hi, lb_ref[h, 0:1, :], cum_scr.at[slot]))
                chains.append((1, 2 * h + 1, rb, hq, head(fb_ref, h), hi, lb_ref[h, 1:2, :], cum_scr.at[slot + 1]))
                rows += [(head(of_scr, h), rf), (head(ob_scr, h), rb)]
        outs, states = _hgrn_chunks(chains, carry, m_ref, lev_ref)
        for (scr, r0), o in zip(rows, outs):
            scr[pl.ds(r0, c), :] = o
        return tuple(states)

    if has_s0:
        init = tuple(s0_ref[d, h] for h in range(nh) for d in range(2))
    else:
        init = tuple(jnp.zeros((HG_KDIM, HG_VDIM), F32) for _ in range(2 * nh))
    if nch == unroll:
        fin = step(0, init)
    else:
        fin = lax.fori_loop(0, nch // unroll, step, init)
    if not has_s0:
        for h in range(nh):
            for d in range(2):
                sfin_ref[d, h] = fin[2 * h + d]

    for h in range(nh):
        o = head(of_scr, h)[...] + head(ob_scr, h)[...]
        ms = jnp.mean(o * o, axis=-1, keepdims=True)
        y = o * lax.rsqrt(ms + EPS) * ng_ref[...]
        head(o_ref, h)[...] = (y * _silu(head(hgt_ref, h)[...])).astype(BF16)


def _hgrn(p, lb, norm_g, m_all, lev_of, layer, seq_len, s0=None):
    n = p.shape[0]
    nseq = n // seq_len
    has_s0 = s0 is not None
    nch = seq_len // HG_CHUNK
    unroll = min(nch, HG_CHAINS // 2)
    nh = HG_CHAINS // (2 * unroll)
    kern = functools.partial(_hgrn_kernel, seq_len=seq_len, nh=nh, unroll=unroll, has_s0=has_s0)
    base = COL_HG // (nh * LANES)
    per = HG_HEADS // nh

    def col(k):
        return pl.BlockSpec((seq_len, nh * LANES), lambda b, h, k=k: (b, base + k * per + h))

    c = HG_CHUNK
    in_specs = [col(0), col(1), col(2), col(3), col(4),
                pl.BlockSpec((None, nh, 2, HG_KDIM), lambda b, h: (layer, h, 0, 0)),
                pl.BlockSpec((None, 1, HG_VDIM), lambda b, h: (layer, 0, 0)),
                pl.BlockSpec((2, (len(HG_MXU_LEVELS) + 1) * c, 2 * c), lambda b, h: (0, 0, 0)),
                pl.BlockSpec((2, c, c), lambda b, h: (0, 0, 0))]
    args = [p, p, p, p, p, lb, norm_g, m_all, lev_of]
    out_specs = [pl.BlockSpec((seq_len, nh * HG_VDIM), lambda b, h: (b, h))]
    out_shape = [jax.ShapeDtypeStruct((n, HG_VW), BF16)]
    if has_s0:
        in_specs.append(pl.BlockSpec((None, None, 2, nh, HG_KDIM, HG_VDIM), lambda b, h: (b, layer, 0, h, 0, 0)))
        args.append(s0)
    else:
        out_specs.append(pl.BlockSpec((None, 2, nh, HG_KDIM, HG_VDIM), lambda b, h: (b, 0, h, 0, 0)))
        out_shape.append(jax.ShapeDtypeStruct((nseq, 2, HG_HEADS, HG_KDIM, HG_VDIM), F32))
    return pl.pallas_call(
        kern,
        grid=(nseq, per),
        in_specs=in_specs,
        out_specs=out_specs,
        out_shape=out_shape,
        scratch_shapes=[pltpu.VMEM((seq_len, nh * HG_VDIM), F32), pltpu.VMEM((seq_len, nh * HG_VDIM), F32),
                        pltpu.VMEM((HG_CHAINS, HG_CHUNK, HG_KDIM), F32)],
        compiler_params=_cparams(("parallel", "parallel")),
        name="hgrn",
    )(*args)


ROUTER_ROWS = 32


def _router(logits_t):
    row = lax.broadcasted_iota(jnp.int32, logits_t.shape, 0)
    big = jnp.int32(ROUTER_ROWS)
    red = dict(axis=0, keepdims=True)
    is_g = row < N_GROUPS
    gl = jnp.where(is_g, logits_t, -jnp.inf)
    gmax = jnp.max(gl, **red)
    gsum = jnp.sum(jnp.exp(gl - gmax), **red)
    g_idx = jnp.min(jnp.where(is_g & (gl == gmax), row, big), **red)
    g_prob = 1.0 / gsum
    e_row = row - ROUTER_OFF
    sel = (e_row >= 0) & (e_row < N_EXPERTS) & ((e_row >> 2) == g_idx)
    el = jnp.where(sel, logits_t, -jnp.inf)
    emax = jnp.max(el, **red)
    eexp = jnp.exp(el - emax)
    ep = eexp / jnp.sum(eexp, **red)
    p1 = jnp.max(ep, **red)
    i1 = jnp.min(jnp.where(sel & (ep == p1), row, big), **red)
    rest = sel & (row != i1)
    ep2 = jnp.where(rest, ep, -1.0)
    p2 = jnp.max(ep2, **red)
    i2 = jnp.min(jnp.where(rest & (ep2 == p2), row, big), **red)
    den = p1 + p2
    w1 = g_prob * (p1 / den)
    w2 = g_prob * (p2 / den)
    return jnp.where(row == i1, w1, jnp.where(row == i2, w2, 0.0))


def _merge_kernel(x_ref, att_ref, z_ref, hg_ref, mg_ref, mod_ref, n2_ref,
                  wao_ref, wga_ref, wgb_ref, who_ref, wout_ref, wr_ref, br_ref,
                  xm_ref, h2_ref, comb_ref):
    y_att = _dot(att_ref[...], wao_ref[...])
    z = z_ref[...].astype(BF16)
    y_ssm = _dot(z, wga_ref[...]) * _sigmoid(_dot(z, wgb_ref[...]))
    y_hg = _dot(hg_ref[...], who_ref[...])
    gate = lambda k: _sigmoid(mg_ref[:, k * D_MODEL:(k + 1) * D_MODEL].astype(F32))
    merged = gate(0) * y_att + gate(1) * y_ssm + gate(2) * y_hg
    xm = x_ref[...] + mod_ref[2:3, :] * _dot(merged.astype(BF16), wout_ref[...])
    xm_ref[...] = xm
    ms = jnp.mean(xm * xm, axis=-1, keepdims=True)
    h2 = xm * lax.rsqrt(ms + EPS) * n2_ref[...] * (1.0 + mod_ref[4:5, :]) + mod_ref[3:4, :]
    h2_hi = h2.astype(BF16)
    h2_ref[...] = h2_hi
    h2_lo = (h2 - h2_hi.astype(F32)).astype(BF16)
    logits = _dot(jnp.concatenate([h2_hi, h2_lo, h2_hi], axis=1), wr_ref[...]) + br_ref[...]
    comb_t = _router(logits.T[0:ROUTER_ROWS])
    comb_t = jnp.concatenate([comb_t, jnp.zeros((LANES - ROUTER_ROWS, comb_t.shape[1]), F32)], axis=0)
    comb_ref[...] = comb_t.T


def _merge(x, att, z, hg, p, mod, n2g, w, layer, seq_len, cond_row0):
    n = x.shape[0]
    tm = 256
    rows_per_cond = seq_len if cond_row0 else n
    mod_idx = (lambda i: (layer, cond_row0 + (i * tm) // rows_per_cond, 0, 0))
    row = lambda width: pl.BlockSpec((tm, width), lambda i: (i, 0))
    wspec = lambda a, b: pl.BlockSpec((None, a, b), lambda i: (layer, 0, 0))
    return pl.pallas_call(
        _merge_kernel,
        grid=(n // tm,),
        in_specs=[
            row(D_MODEL), row(Q_W), row(SSM_WIDTH), row(HG_VW),
            row(MG_W),
            pl.BlockSpec((None, None, 6, D_MODEL), mod_idx),
            wspec(1, D_MODEL),
            wspec(Q_W, D_MODEL), wspec(SSM_WIDTH, D_MODEL), wspec(SSM_WIDTH, D_MODEL),
            wspec(HG_VW, D_MODEL), wspec(D_MODEL, D_MODEL), wspec(3 * D_MODEL, LANES), wspec(1, LANES),
        ],
        out_specs=[row(D_MODEL), row(D_MODEL), row(LANES)],
        out_shape=[
            jax.ShapeDtypeStruct((n, D_MODEL), F32),
            jax.ShapeDtypeStruct((n, D_MODEL), BF16),
            jax.ShapeDtypeStruct((n, LANES), F32),
        ],
        compiler_params=_cparams(("parallel",)),
        name="merge",
    )(x, att, z, hg, p, mod, n2g, w['w_attn_o'], w['w_glu_a'], w['w_glu_b'], w['w_hg_o'], w['w_out'],
      w['w_route'], w['b_route'])


def _moe_kernel(h_ref, comb_ref, xm_ref, mod_ref, wg_ref, wu_ref, wd_ref, o_ref, acc_scr):
    grp = pl.program_id(1)
    comb = comb_ref[...]
    lane = lax.broadcasted_iota(jnp.int32, comb.shape, 1)
    h = h_ref[...]
    hg = _dot(h, wg_ref[...])
    hu = _dot(h, wu_ref[...])
    parts = []
    for e in range(EXPERTS_PER_GROUP):
        cw = jnp.sum(jnp.where(lane == ROUTER_OFF + grp * EXPERTS_PER_GROUP + e, comb, 0.0),
                     axis=-1, keepdims=True)
        sl = slice(e * EXPERT_FF, (e + 1) * EXPERT_FF)
        parts.append((_silu(hg[:, sl]) * hu[:, sl] * cw).astype(BF16))
    y = _dot(jnp.concatenate(parts, axis=1), wd_ref[...])

    @pl.when(grp == 0)
    def _():
        acc_scr[...] = y

    @pl.when(grp > 0)
    def _():
        acc_scr[...] += y

    @pl.when(grp == N_GROUPS - 1)
    def _():
        o_ref[...] = xm_ref[...] + mod_ref[5:6, :] * acc_scr[...]


def _moe(h2, comb, xm, mod, w, layer, seq_len, cond_row0):
    n = h2.shape[0]
    tm = 512
    gw = EXPERTS_PER_GROUP * EXPERT_FF
    rows_per_cond = seq_len if cond_row0 else n
    mod_idx = (lambda i, g: (layer, cond_row0 + (i * tm) // rows_per_cond, 0, 0))
    return pl.pallas_call(
        _moe_kernel,
        grid=(n // tm, N_GROUPS),
        in_specs=[
            pl.BlockSpec((tm, D_MODEL), lambda i, g: (i, 0)),
            pl.BlockSpec((tm, LANES), lambda i, g: (i, 0)),
            pl.BlockSpec((tm, D_MODEL), lambda i, g: (i, 0)),
            pl.BlockSpec((None, None, 6, D_MODEL), mod_idx),
            pl.BlockSpec((None, D_MODEL, gw), lambda i, g: (layer, 0, g)),
            pl.BlockSpec((None, D_MODEL, gw), lambda i, g: (layer, 0, g)),
            pl.BlockSpec((None, gw, D_MODEL), lambda i, g: (layer, g, 0)),
        ],
        out_specs=pl.BlockSpec((tm, D_MODEL), lambda i, g: (i, 0)),
        out_shape=jax.ShapeDtypeStruct((n, D_MODEL), F32),
        scratch_shapes=[pltpu.VMEM((tm, D_MODEL), F32)],
        compiler_params=_cparams(("parallel", "arbitrary")),
        name="moe",
    )(h2, comb, xm, mod, w['w_e_gate'], w['w_e_up'], w['w_e_down'])


def _layer(x, layer, w, seq_len, cond_row0, cache):
    mg, p = _in_proj(x, w['mod'], w['norm1_g'], w['w_in_mg'], w['w_in_p'], layer, seq_len, cond_row0)
    nseq = x.shape[0] // seq_len
    if cache is None:
        att, k_new, v_new = _attn_ctx(w['attn_sink'], p, w['q_norm_g'], w['k_norm_g'], layer)
        z, ssm_fin = _ssm(p, w['ssm'], layer, nseq, seq_len)
        hg, hg_fin = _hgrn(p, w['hg_lb'], w['hg_norm_g'], w['hg_m'], w['hg_lev'], layer, seq_len)
        ctx = (k_new, v_new, ssm_fin, hg_fin)
    else:
        cache_k, cache_v, h0, s0 = cache
        att = _attn_lat(w['attn_sink'], p, cache_k, cache_v, w['q_norm_g'], w['k_norm_g'],
                        w['rope_cos'], w['rope_sin'], layer)
        (z,) = _ssm(p, w['ssm'], layer, nseq, seq_len, h0=h0)
        (hg,) = _hgrn(p, w['hg_lb'], w['hg_norm_g'], w['hg_m'], w['hg_lev'], layer, seq_len, s0=s0)
        ctx = None
    xm, h2, comb = _merge(x, att, z, hg, mg, w['mod'], w['norm2_g'], w, layer, seq_len, cond_row0)
    x = _moe(h2, comb, xm, w['mod'], w, layer, seq_len, cond_row0)
    return x, ctx


def kernel(x_prompt, x_sample, cache_k, cache_v, state_ssm_re, state_ssm_im, state_hgrn, c, c_ctx, w_mod, b_mod, norm1_g, norm2_g, w_in, q_norm_g, k_norm_g, attn_sink, w_attn_o, ssm_a_re, ssm_a_im, ssm_log_dt, ssm_b_re, ssm_b_im, ssm_c_re, ssm_c_im, ssm_d, w_glu_a, w_glu_b, hg_lb, hg_norm_g, w_hg_o, w_out, w_group, b_group, w_router, b_router, w_e_gate, w_e_up, w_e_down):
    w = {}
    w['mod'] = _modulation(c, c_ctx, w_mod, b_mod)
    w['norm1_g'] = norm1_g.reshape(DEPTH, 1, D_MODEL)
    w['norm2_g'] = norm2_g.reshape(DEPTH, 1, D_MODEL)
    w['w_in_mg'] = w_in[:, :, 3840:].astype(BF16)
    w['w_in_p'] = jnp.concatenate([w_in[:, :, 1280:3840], w_in[:, :, 0:768], w_in[:, :, 768:1280]],
                                  axis=-1).astype(BF16)
    w['q_norm_g'] = jnp.tile(q_norm_g, (1, LANES // HEAD_DIM)).reshape(DEPTH, 1, LANES)
    w['k_norm_g'] = jnp.tile(k_norm_g, (1, LANES // HEAD_DIM)).reshape(DEPTH, 1, LANES)
    w['attn_sink'] = attn_sink
    w['rope_cos'], w['rope_sin'] = _rope_tables()
    w['ssm'] = _ssm_prep(ssm_a_re, ssm_a_im, ssm_log_dt, ssm_b_re, ssm_b_im, ssm_c_re, ssm_c_im, ssm_d)
    w['hg_lb'] = _lower_bounds(hg_lb)
    w['hg_norm_g'] = hg_norm_g.reshape(DEPTH, 1, HG_VDIM)
    w['hg_m'], w['hg_lev'] = _hgrn_consts()
    for name, val in (('w_attn_o', w_attn_o), ('w_glu_a', w_glu_a), ('w_glu_b', w_glu_b), ('w_hg_o', w_hg_o),
                      ('w_out', w_out)):
        w[name] = val.astype(BF16)
    ff_all = N_EXPERTS * EXPERT_FF
    w['w_e_gate'] = w_e_gate.astype(BF16).transpose(0, 2, 1, 3).reshape(DEPTH, D_MODEL, ff_all)
    w['w_e_up'] = w_e_up.astype(BF16).transpose(0, 2, 1, 3).reshape(DEPTH, D_MODEL, ff_all)
    w['w_e_down'] = w_e_down.astype(BF16).reshape(DEPTH, ff_all, D_MODEL)
    pad = LANES - N_GROUPS - N_EXPERTS
    w_route = jnp.concatenate([w_group, w_router, jnp.zeros((DEPTH, D_MODEL, pad), F32)], axis=-1)
    w_route_hi = w_route.astype(BF16)
    w_route_lo = (w_route - w_route_hi.astype(F32)).astype(BF16)
    w['w_route'] = jnp.concatenate([w_route_hi, w_route_hi, w_route_lo], axis=1)
    w['b_route'] = jnp.concatenate([b_group, b_router, jnp.zeros((DEPTH, pad), F32)], axis=-1).reshape(DEPTH, 1, LANES)

    ck = cache_k.reshape(DEC_BATCH, DEPTH, PAST_LEN, KV_W)
    cv = cache_v.reshape(DEC_BATCH, DEPTH, PAST_LEN, KV_W)

    ng = LANES // SSM_GROUP
    h0 = jnp.stack([state_ssm_re, state_ssm_im]).transpose(2, 4, 0, 1, 3, 5).reshape(
        DEPTH, SSM_GROUPS // ng, ng, 2, DEC_BATCH, 2 * SSM_STATE)
    h0 = jnp.pad(h0, ((0, 0),) * 4 + ((0, SUBLANES - DEC_BATCH), (0, 0)))

    xp = x_prompt.reshape(BATCH * SEQ, D_MODEL)
    xs = x_sample.reshape(DEC_BATCH * DEC_SEQ, D_MODEL)
    new_k, new_v, new_re, new_im, new_hg = [], [], [], [], []
    for l in range(DEPTH):
        xp, ctx = _layer(xp, l, w, SEQ, 0, None)
        new_k.append(ctx[0].reshape(BATCH, SEQ, N_KV_HEADS, HEAD_DIM))
        new_v.append(ctx[1].reshape(BATCH, SEQ, N_KV_HEADS, HEAD_DIM))
        fin = ctx[2].reshape(BATCH // SUBLANES, SSM_GROUPS // ng, ng, 2, SUBLANES, 2, SSM_STATE)
        fin = fin.transpose(3, 0, 4, 5, 1, 2, 6).reshape(2, BATCH, 2, SSM_GROUPS, SSM_STATE)
        new_re.append(fin[0])
        new_im.append(fin[1])
        new_hg.append(ctx[3])
        xs, _ = _layer(xs, l, w, DEC_SEQ, 1, (ck, cv, h0, state_hgrn))
    return (xp.reshape(BATCH, SEQ, D_MODEL), xs.reshape(DEC_BATCH, DEC_SEQ, D_MODEL),
            jnp.stack(new_k, axis=1), jnp.stack(new_v, axis=1), jnp.stack(new_re, axis=1),
            jnp.stack(new_im, axis=1), jnp.stack(new_hg, axis=1))
```

```python
import functools
import math

import numpy as np
import jax
import jax.numpy as jnp
from jax import lax
from jax.experimental import pallas as pl
from jax.experimental.pallas import tpu as pltpu

F32 = jnp.float32
BF16 = jnp.bfloat16

D_MODEL = 1024
BATCH = 32
SEQ = 256
DEPTH = 4
DEC_BATCH = 4
DEC_SEQ = 1024
PAST_LEN = 512
GRID_W = 64
N_HEADS = 8
N_KV_HEADS = 2
HEAD_DIM = 64
KV_GROUP = N_HEADS // N_KV_HEADS
WINDOW = 128
LAT_QBLOCK = 128
ROPE_BASE = 10000.0
SSM_WIDTH = 512
SSM_GROUP = 16
SSM_GROUPS = SSM_WIDTH // SSM_GROUP
SSM_STATE = 64
HG_HEADS = 4
HG_KDIM = 128
HG_VDIM = 128
N_BRANCHES = 3
N_GROUPS = 4
EXPERTS_PER_GROUP = 4
N_EXPERTS = N_GROUPS * EXPERTS_PER_GROUP
EXPERT_FF = 256
EPS = 1e-6
Q_W = N_HEADS * HEAD_DIM
KV_W = N_KV_HEADS * HEAD_DIM
HG_KW = HG_HEADS * HG_KDIM
HG_VW = HG_HEADS * HG_VDIM

LANES = 128
SUBLANES = 8
VMEM_LIMIT = 56 * 1024 * 1024

MG_W = N_BRANCHES * D_MODEL
COL_HG = 0
COL_AQ = COL_HG + 3 * HG_KW + 2 * HG_VW
COL_AK = COL_AQ + Q_W
COL_SU = COL_AK + 2 * KV_W
P_WIDTH = COL_SU + SSM_WIDTH
IN_WIDTH = P_WIDTH
INPROJ_STEPS = 3

SSM_T = 16
SSM_PREP_GROUPS = 4
HG_CHUNK = 128
HG_LEVELS = 7
HG_CHAINS = 8
HG_MXU_LEVELS = (1, 2)
ROUTER_OFF = N_GROUPS
MASK_NEG = -1e30


def _cparams(sem):
    return pltpu.CompilerParams(dimension_semantics=sem, vmem_limit_bytes=VMEM_LIMIT)


def _sigmoid(x):
    return 1.0 / (1.0 + jnp.exp(-x))


def _silu(x):
    return x * _sigmoid(x)


def _gelu_tanh(x):
    return 0.5 * x * (1.0 + jnp.tanh(math.sqrt(2.0 / math.pi) * (x + 0.044715 * (x * x * x))))


def _dot(a, b):
    return jnp.dot(a, b, preferred_element_type=F32)


def _dot_nt(a, b):
    return lax.dot_general(a, b, (((1,), (1,)), ((), ())), preferred_element_type=F32)


def _dot_tn(a, b):
    return lax.dot_general(a, b, (((0,), (0,)), ((), ())), preferred_element_type=F32)


def _shift_lanes(a, s):
    if s == 0:
        return a
    lo, hi = a[:, :LANES], a[:, LANES:]
    zero = jnp.zeros_like(lo)
    lane = lax.broadcasted_iota(jnp.int32, lo.shape, 1)
    if s > 0:
        if s >= LANES:
            t = s - LANES
            out_hi = lo if t == 0 else jnp.where(lane >= t, pltpu.roll(lo, t, 1), 0.0)
            return jnp.concatenate([zero, out_hi], axis=1)
        r_lo, r_hi = pltpu.roll(lo, s, 1), pltpu.roll(hi, s, 1)
        return jnp.concatenate([jnp.where(lane >= s, r_lo, 0.0), jnp.where(lane >= s, r_hi, r_lo)], axis=1)
    s = -s
    if s >= LANES:
        t = s - LANES
        out_lo = hi if t == 0 else jnp.where(lane < LANES - t, pltpu.roll(hi, LANES - t, 1), 0.0)
        return jnp.concatenate([out_lo, zero], axis=1)
    r_lo, r_hi = pltpu.roll(lo, LANES - s, 1), pltpu.roll(hi, LANES - s, 1)
    keep = lane < LANES - s
    return jnp.concatenate([jnp.where(keep, r_lo, r_hi), jnp.where(keep, r_hi, 0.0)], axis=1)


def _ssm_prep_kernel(*refs):
    for g in range(SSM_PREP_GROUPS):
        _ssm_prep_group(*[r.at[g] for r in refs])


def _ssm_prep_group(are_ref, aim_ref, ldt_ref, btre_ref, btim_ref, cre_ref, cim_ref,
                    w_ref, bst_ref, cst_ref, lamt_ref):
    t_blk = SSM_T
    ns = SSM_STATE
    hp = lax.Precision.HIGHEST
    jf = lax.broadcasted_iota(jnp.int32, (2 * t_blk, ns), 0).astype(F32)
    bt_re = btre_ref[...]
    bt_im = btim_ref[...]
    c_re = cre_ref[...]
    c_im = cim_ref[...]
    p_re, p_im, bb_re, bb_im, kt = [], [], [], [], []
    for d in range(2):
        a_re = are_ref[d:d + 1, :]
        a_im = aim_ref[d:d + 1, :]
        dt = jnp.exp(ldt_ref[d:d + 1, :])
        mag = jnp.exp(jf * (a_re * dt))
        ang = jf * (a_im * dt)
        p_re.append(mag * jnp.cos(ang))
        p_im.append(mag * jnp.sin(ang))
        nr = p_re[d][1:2] - 1.0
        ni = p_im[d][1:2]
        den = a_re * a_re + a_im * a_im
        f_re = (nr * a_re + ni * a_im) / den
        f_im = (ni * a_re - nr * a_im) / den
        bb_re.append(f_re * bt_re - f_im * bt_im)
        bb_im.append(f_re * bt_im + f_im * bt_re)
        lags = range(t_blk) if d == 0 else range(t_blk - 1, -1, -1)
        a_parts_re = [c_re * p_re[d][j:j + 1] - c_im * p_im[d][j:j + 1] for j in lags]
        a_parts_im = [c_re * p_im[d][j:j + 1] + c_im * p_re[d][j:j + 1] for j in lags]
        nt = (((1,), (1,)), ((), ()))
        kt.append(lax.dot_general(bb_re[d], jnp.concatenate(a_parts_re, axis=0), nt,
                                  precision=hp, preferred_element_type=F32)
                  - lax.dot_general(bb_im[d], jnp.concatenate(a_parts_im, axis=0), nt,
                                    precision=hp, preferred_element_type=F32))
        lamt_ref[0:1, d * ns:(d + 1) * ns] = p_re[d][t_blk:t_blk + 1]
        lamt_ref[1:2, d * ns:(d + 1) * ns] = p_im[d][t_blk:t_blk + 1]
    for t in range(t_blk):
        rows = slice(t * SSM_GROUP, (t + 1) * SSM_GROUP)
        w_ref[rows, :] = (_shift_lanes(kt[0], t * SSM_GROUP)
                          + _shift_lanes(kt[1], -(t_blk - 1 - t) * SSM_GROUP)).astype(BF16)
        ef, eb = t_blk - 1 - t, t
        bst = [bb_re[0] * p_re[0][ef:ef + 1] - bb_im[0] * p_im[0][ef:ef + 1],
               bb_re[1] * p_re[1][eb:eb + 1] - bb_im[1] * p_im[1][eb:eb + 1],
               bb_re[0] * p_im[0][ef:ef + 1] + bb_im[0] * p_re[0][ef:ef + 1],
               bb_re[1] * p_im[1][eb:eb + 1] + bb_im[1] * p_re[1][eb:eb + 1]]
        bst_ref[rows, :] = jnp.concatenate(bst, axis=1).astype(BF16)
        ef, eb = t + 1, t_blk - t
        cst = [c_re * p_re[0][ef:ef + 1] - c_im * p_im[0][ef:ef + 1],
               c_re * p_re[1][eb:eb + 1] - c_im * p_im[1][eb:eb + 1],
               -(c_re * p_im[0][ef:ef + 1] + c_im * p_re[0][ef:ef + 1]),
               -(c_re * p_im[1][eb:eb + 1] + c_im * p_re[1][eb:eb + 1])]
        cst_ref[rows, :] = jnp.concatenate(cst, axis=1).astype(BF16)


def _ssm_prep(ssm_a_re, ssm_a_im, ssm_log_dt, ssm_b_re, ssm_b_im, ssm_c_re, ssm_c_im, ssm_d):
    t_blk = SSM_T
    bw = t_blk * SSM_GROUP
    lgd = lambda a: a.transpose(0, 2, 1, 3)
    ldt = jnp.broadcast_to(ssm_log_dt[..., None], ssm_a_re.shape)
    per = SSM_PREP_GROUPS
    vec = pl.BlockSpec((None, per, 2, SSM_STATE), lambda l, g: (l, g, 0, 0))
    mat = pl.BlockSpec((None, per, SSM_GROUP, SSM_STATE), lambda l, g: (l, g, 0, 0))
    op = lambda width: pl.BlockSpec((None, per, bw, width), lambda l, g: (l, g, 0, 0))
    w, bst, cst, lamt = pl.pallas_call(
        _ssm_prep_kernel,
        grid=(DEPTH, SSM_GROUPS // per),
        in_specs=[vec, vec, vec, mat, mat, mat, mat],
        out_specs=[op(bw), op(4 * SSM_STATE), op(4 * SSM_STATE),
                   pl.BlockSpec((None, per, 2, 2 * SSM_STATE), lambda l, g: (l, g, 0, 0))],
        out_shape=[
            jax.ShapeDtypeStruct((DEPTH, SSM_GROUPS, bw, bw), BF16),
            jax.ShapeDtypeStruct((DEPTH, SSM_GROUPS, bw, 4 * SSM_STATE), BF16),
            jax.ShapeDtypeStruct((DEPTH, SSM_GROUPS, bw, 4 * SSM_STATE), BF16),
            jax.ShapeDtypeStruct((DEPTH, SSM_GROUPS, 2, 2 * SSM_STATE), F32),
        ],
        compiler_params=_cparams(("parallel", "parallel")),
        name="ssm_prep",
    )(lgd(ssm_a_re), lgd(ssm_a_im), lgd(ldt), jnp.swapaxes(ssm_b_re, -1, -2), jnp.swapaxes(ssm_b_im, -1, -2),
      ssm_c_re, ssm_c_im)
    d_row = jnp.tile(ssm_d, (1, 1, t_blk)).reshape(DEPTH, SSM_GROUPS, 1, bw)
    return dict(w=w, bst=bst, cst=cst, lamt=lamt, d=d_row)


def _lb_kernel(x_ref, o_ref):
    x = x_ref[...]
    m = jnp.max(x, axis=0, keepdims=True)
    e = jnp.exp(x - m)
    s = e / jnp.sum(e, axis=0, keepdims=True)
    run = jnp.zeros_like(s[0:1])
    o_ref[0:1, :] = run
    for l in range(1, DEPTH):
        run = run + s[l:l + 1]
        o_ref[l:l + 1, :] = run


def _lower_bounds(hg_lb):
    w = 2 * HG_KW
    out = pl.pallas_call(
        _lb_kernel,
        out_shape=jax.ShapeDtypeStruct((DEPTH, w), F32),
        name="hgrn_lower_bounds",
    )(hg_lb.reshape(DEPTH, w))
    return out.reshape(DEPTH, 2, HG_HEADS, HG_KDIM).transpose(0, 2, 1, 3)


def _mod_kernel(c_ref, w_ref, b_ref, o_ref):
    c = c_ref[...]
    a = _silu(c).astype(BF16)
    o_ref[...] = _dot(a, w_ref[...].astype(BF16)) + b_ref[...]


def _modulation(c, c_ctx, w_mod, b_mod):
    rows = SUBLANES
    cond = jnp.concatenate([c_ctx[None, :], c, jnp.zeros((rows - 1 - DEC_BATCH, D_MODEL), F32)], axis=0)
    tn = D_MODEL
    out = pl.pallas_call(
        _mod_kernel,
        grid=(DEPTH, 6),
        in_specs=[
            pl.BlockSpec((rows, D_MODEL), lambda l, j: (0, 0)),
            pl.BlockSpec((None, D_MODEL, tn), lambda l, j: (l, 0, j)),
            pl.BlockSpec((None, 1, tn), lambda l, j: (l, 0, j)),
        ],
        out_specs=pl.BlockSpec((None, rows, tn), lambda l, j: (l, 0, j)),
        out_shape=jax.ShapeDtypeStruct((DEPTH, rows, 6 * D_MODEL), F32),
        compiler_params=_cparams(("parallel", "parallel")),
        name="modulation",
    )(cond, w_mod, b_mod.reshape(DEPTH, 1, 6 * D_MODEL))
    return out.reshape(DEPTH, rows, 6, D_MODEL)


def _inproj_kernel(x_ref, mod_ref, g_ref, wmg_ref, wp_ref, mg_ref, p_ref, h_scr):
    @pl.when(pl.program_id(1) == 0)
    def _():
        x = x_ref[...]
        ms = jnp.mean(x * x, axis=-1, keepdims=True)
        y = x * lax.rsqrt(ms + EPS) * g_ref[...]
        h = y * (1.0 + mod_ref[1:2, :]) + mod_ref[0:1, :]
        h_scr[...] = h.astype(BF16)

    h = h_scr[...]
    mg_ref[...] = _dot(h, wmg_ref[...]).astype(BF16)
    p_ref[...] = _dot(h, wp_ref[...])


def _in_proj(x, mod, norm_g, w_mg, w_p, layer, seq_len, cond_row0):
    n = x.shape[0]
    tm = 1024
    tn_mg = MG_W // INPROJ_STEPS
    tn_p = P_WIDTH // INPROJ_STEPS
    rows_per_cond = seq_len if cond_row0 else n
    mod_idx = (lambda i, j: (layer, cond_row0 + (i * tm) // rows_per_cond, 0, 0))
    return pl.pallas_call(
        _inproj_kernel,
        grid=(n // tm, INPROJ_STEPS),
        in_specs=[
            pl.BlockSpec((tm, D_MODEL), lambda i, j: (i, 0)),
            pl.BlockSpec((None, None, 6, D_MODEL), mod_idx),
            pl.BlockSpec((None, 1, D_MODEL), lambda i, j: (layer, 0, 0)),
            pl.BlockSpec((None, D_MODEL, tn_mg), lambda i, j: (layer, 0, j)),
            pl.BlockSpec((None, D_MODEL, tn_p), lambda i, j: (layer, 0, j)),
        ],
        out_specs=[pl.BlockSpec((tm, tn_mg), lambda i, j: (i, j)), pl.BlockSpec((tm, tn_p), lambda i, j: (i, j))],
        out_shape=[jax.ShapeDtypeStruct((n, MG_W), BF16), jax.ShapeDtypeStruct((n, P_WIDTH), F32)],
        scratch_shapes=[pltpu.VMEM((tm, D_MODEL), BF16)],
        compiler_params=_cparams(("parallel", "arbitrary")),
        name="in_proj",
    )(x, mod, norm_g, w_mg, w_p)


def _pair_norm(x, g):
    xx = x * x
    s_a = jnp.sum(xx[:, :HEAD_DIM], axis=-1, keepdims=True)
    s_b = jnp.sum(xx[:, HEAD_DIM:], axis=-1, keepdims=True)
    lane = lax.broadcasted_iota(jnp.int32, x.shape, 1)
    ms = jnp.where(lane < HEAD_DIM, s_a, s_b) * (1.0 / HEAD_DIM)
    return x * lax.rsqrt(ms + EPS) * g


def _rope(x, cos, sin_signed):
    lane = lax.broadcasted_iota(jnp.int32, x.shape, 1)
    first = (lane & 31) < 16
    partner = jnp.where(first, pltpu.roll(x, LANES - 16, 1), pltpu.roll(x, 16, 1))
    return x * cos + partner * sin_signed


def _sink_column(sink_ref, layer, kh, t):
    row = lax.broadcasted_iota(jnp.int32, (KV_GROUP * t, 1), 0)
    col = jnp.full((KV_GROUP * t, 1), sink_ref[layer, kh * KV_GROUP], F32)
    for g in range(1, KV_GROUP):
        col = jnp.where(row >= g * t, sink_ref[layer, kh * KV_GROUP + g], col)
    return col


def _sink_softmax_av(heads):
    ms = []
    for scores, _, sink_col in heads:
        m = sink_col
        for s in scores:
            m = jnp.maximum(m, jnp.max(s, axis=-1, keepdims=True))
        ms.append(m)
    ps = [[jnp.exp(s - m) for s in scores] for (scores, _, _), m in zip(heads, ms)]
    outs = []
    for (scores, values, sink_col), m, p_list in zip(heads, ms, ps):
        den = jnp.exp(sink_col - m)
        o = None
        for p, v in zip(p_list, values):
            den = den + jnp.sum(p, axis=-1, keepdims=True)
            pv = _dot(p.astype(BF16), v.astype(BF16))
            o = pv if o is None else o + pv
        outs.append(o / den)
    return outs


def _attn_ctx_kernel(sink_ref, q_ref, kv_ref, gq_ref, gk_ref, att_ref, k_ref, v_ref, *, layer):
    t = q_ref.shape[0]
    kv = kv_ref[...]
    k = _pair_norm(kv[:, :KV_W], gk_ref[...])
    v = kv[:, KV_W:]
    k_ref[...] = k
    v_ref[...] = v
    q = q_ref[...]
    scale = 1.0 / math.sqrt(HEAD_DIM)
    work = []
    for kh in range(N_KV_HEADS):
        heads = []
        for j in range(KV_GROUP // 2):
            c0 = (kh * (KV_GROUP // 2) + j) * LANES
            qn = _pair_norm(q[:, c0:c0 + LANES], gq_ref[...])
            heads += [qn[:, :HEAD_DIM], qn[:, HEAD_DIM:]]
        qs = jnp.concatenate(heads, axis=0).astype(BF16)
        sl = slice(kh * HEAD_DIM, (kh + 1) * HEAD_DIM)
        s = _dot_nt(qs, k[:, sl].astype(BF16)) * scale
        work.append(([s], [v[:, sl]], _sink_column(sink_ref, layer, kh, t)))
    outs = [o[g * t:(g + 1) * t] for o in _sink_softmax_av(work) for g in range(KV_GROUP)]
    att_ref[...] = jnp.concatenate(outs, axis=-1).astype(BF16)


def _attn_ctx(sink, p, gq, gk, layer):
    n = p.shape[0]
    t = SEQ
    kern = functools.partial(_attn_ctx_kernel, layer=layer)
    return pl.pallas_call(
        kern,
        grid=(n // t,),
        in_specs=[
            pl.BlockSpec(memory_space=pltpu.SMEM),
            pl.BlockSpec((t, Q_W), lambda b: (b, COL_AQ // Q_W)),
            pl.BlockSpec((t, 2 * KV_W), lambda b: (b, COL_AK // (2 * KV_W))),
            pl.BlockSpec((None, 1, LANES), lambda b: (layer, 0, 0)),
            pl.BlockSpec((None, 1, LANES), lambda b: (layer, 0, 0)),
        ],
        out_specs=[
            pl.BlockSpec((t, Q_W), lambda b: (b, 0)),
            pl.BlockSpec((t, KV_W), lambda b: (b, 0)),
            pl.BlockSpec((t, KV_W), lambda b: (b, 0)),
        ],
        out_shape=[
            jax.ShapeDtypeStruct((n, Q_W), BF16),
            jax.ShapeDtypeStruct((n, KV_W), F32),
            jax.ShapeDtypeStruct((n, KV_W), F32),
        ],
        compiler_params=_cparams(("parallel",)),
        name="attn_ctx",
    )(sink, p, p, gq, gk)


def _attn_lat_kernel(sink_ref, q_ref, kv_ref, kc_ref, vc_ref, gq_ref, gk_ref, cos_ref, sin_ref, att_ref, *, layer):
    t = LAT_QBLOCK
    span = t + 2 * WINDOW
    start = pl.program_id(1) * t
    ks = pl.multiple_of(jnp.clip(start - WINDOW, 0, DEC_SEQ - span), WINDOW)
    q0 = pl.multiple_of(start, t)
    kvw = kv_ref[pl.ds(ks, span), :]
    k = _rope(_pair_norm(kvw[:, :KV_W], gk_ref[...]), cos_ref[pl.ds(ks, span), :], sin_ref[pl.ds(ks, span), :])
    v = kvw[:, KV_W:]
    cos_q = cos_ref[pl.ds(q0, t), :]
    sin_q = sin_ref[pl.ds(q0, t), :]
    qpos = start + (lax.broadcasted_iota(jnp.int32, (KV_GROUP * t, span), 0) & (t - 1))
    kpos = ks + lax.broadcasted_iota(jnp.int32, (KV_GROUP * t, span), 1)
    band = jnp.abs(qpos - kpos) <= WINDOW
    q = q_ref[...]
    kc = kc_ref[...]
    vc = vc_ref[...]
    scale = 1.0 / math.sqrt(HEAD_DIM)
    work = []
    for kh in range(N_KV_HEADS):
        heads = []
        for j in range(KV_GROUP // 2):
            c0 = (kh * (KV_GROUP // 2) + j) * LANES
            qn = _rope(_pair_norm(q[:, c0:c0 + LANES], gq_ref[...]), cos_q, sin_q)
            heads += [qn[:, :HEAD_DIM], qn[:, HEAD_DIM:]]
        qs = jnp.concatenate(heads, axis=0).astype(BF16)
        sl = slice(kh * HEAD_DIM, (kh + 1) * HEAD_DIM)
        s_w = jnp.where(band, _dot_nt(qs, k[:, sl].astype(BF16)) * scale, MASK_NEG)
        s_c = _dot_nt(qs, kc[:, sl].astype(BF16)) * scale
        work.append(([s_w, s_c], [v[:, sl], vc[:, sl]], _sink_column(sink_ref, layer, kh, t)))
    outs = [o[g * t:(g + 1) * t] for o in _sink_softmax_av(work) for g in range(KV_GROUP)]
    att_ref[...] = jnp.concatenate(outs, axis=-1).astype(BF16)


def _attn_lat(sink, p, cache_k, cache_v, gq, gk, cos_t, sin_t, layer):
    n = p.shape[0]
    t = LAT_QBLOCK
    nb = DEC_SEQ // t
    kern = functools.partial(_attn_lat_kernel, layer=layer)
    return pl.pallas_call(
        kern,
        grid=(DEC_BATCH, nb),
        in_specs=[
            pl.BlockSpec(memory_space=pltpu.SMEM),
            pl.BlockSpec((t, Q_W), lambda b, j: (b * nb + j, COL_AQ // Q_W)),
            pl.BlockSpec((DEC_SEQ, 2 * KV_W), lambda b, j: (b, COL_AK // (2 * KV_W))),
            pl.BlockSpec((None, None, PAST_LEN, KV_W), lambda b, j: (b, layer, 0, 0)),
            pl.BlockSpec((None, None, PAST_LEN, KV_W), lambda b, j: (b, layer, 0, 0)),
            pl.BlockSpec((None, 1, LANES), lambda b, j: (layer, 0, 0)),
            pl.BlockSpec((None, 1, LANES), lambda b, j: (layer, 0, 0)),
            pl.BlockSpec((DEC_SEQ, LANES), lambda b, j: (0, 0)),
            pl.BlockSpec((DEC_SEQ, LANES), lambda b, j: (0, 0)),
        ],
        out_specs=pl.BlockSpec((t, Q_W), lambda b, j: (b * nb + j, 0)),
        out_shape=jax.ShapeDtypeStruct((n, Q_W), BF16),
        compiler_params=_cparams(("parallel", "parallel")),
        name="attn_lat",
    )(sink, p, p, cache_k, cache_v, gq, gk, cos_t, sin_t)


def _rope_tables():
    pos = jnp.arange(DEC_SEQ)
    rows = (pos // GRID_W).astype(F32)
    cols = (pos % GRID_W).astype(F32)
    half = HEAD_DIM // 2
    freqs = ROPE_BASE ** (-jnp.arange(0, half, 2, dtype=F32) / half)
    ang_r = rows[:, None] * freqs[None, :]
    ang_c = cols[:, None] * freqs[None, :]
    ang = jnp.concatenate([ang_r, ang_r, ang_c, ang_c], axis=-1)
    sign = jnp.tile(jnp.concatenate([-jnp.ones((16,), F32), jnp.ones((16,), F32)]), 2)
    cos_t = jnp.tile(jnp.cos(ang), (1, LANES // HEAD_DIM))
    sin_t = jnp.tile(jnp.sin(ang) * sign[None, :], (1, LANES // HEAD_DIM))
    return cos_t, sin_t


def _ssm_kernel(*refs, nseq, seq_len, has_h0):
    if has_h0:
        x_ref, w_ref, bst_ref, cst_ref, lamt_ref, d_ref, h0_ref, z_ref, loc_scr, ent_scr, cols_scr = refs
    else:
        x_ref, w_ref, bst_ref, cst_ref, lamt_ref, d_ref, z_ref, fin_ref, loc_scr, ent_scr, cols_scr = refs
    ns = SSM_STATE
    t_blk = SSM_T
    nblk = seq_len // t_blk
    ng = LANES // SSM_GROUP
    rows = nblk * SUBLANES
    per_half = LANES // SSM_GROUP
    if nseq < SUBLANES:
        cols_scr[...] = jnp.zeros_like(cols_scr)
    for t in range(t_blk):
        for b in range(nblk):
            cols_scr[t, b * SUBLANES:b * SUBLANES + nseq, :] = x_ref[pl.ds(b * t_blk + t, nseq, stride=seq_len), :]
    cols = [cols_scr[t] for t in range(t_blk)]
    chunk = lax.broadcasted_iota(jnp.int32, (rows, LANES), 1) // SSM_GROUP
    is_fwd = lax.broadcasted_iota(jnp.int32, (SUBLANES, 2 * ns), 1) < ns

    def roll(a, chunks):
        shift = (chunks * SSM_GROUP) % LANES
        return a if shift == 0 else pltpu.roll(a, shift, 1)

    xg = []
    for g in range(ng):
        halves = []
        for h in range(t_blk // per_half):
            acc = None
            for j in range(per_half):
                r = roll(cols[h * per_half + j], j - g)
                acc = r if acc is None else jnp.where(chunk == j, r, acc)
            halves.append(acc)
        xg.append(jnp.concatenate(halves, axis=1))
        loc_scr[g] = _dot(xg[g].astype(BF16), bst_ref[g])
    l_re = [jnp.broadcast_to(lamt_ref[g, 0:1, :], (SUBLANES, 2 * ns)) for g in range(ng)]
    l_im = [jnp.broadcast_to(lamt_ref[g, 1:2, :], (SUBLANES, 2 * ns)) for g in range(ng)]
    if has_h0:
        s_re = [h0_ref[g, 0] for g in range(ng)]
        s_im = [h0_ref[g, 1] for g in range(ng)]
    else:
        s_re = [jnp.zeros((SUBLANES, 2 * ns), F32) for _ in range(ng)]
        s_im = [jnp.zeros((SUBLANES, 2 * ns), F32) for _ in range(ng)]
    for j in range(nblk):
        rf = slice(j * SUBLANES, (j + 1) * SUBLANES)
        rb = slice((nblk - 1 - j) * SUBLANES, (nblk - j) * SUBLANES)
        for g in range(ng):
            ent_scr[g, rf, 0:ns] = s_re[g][:, 0:ns]
            ent_scr[g, rb, ns:2 * ns] = s_re[g][:, ns:2 * ns]
            ent_scr[g, rf, 2 * ns:3 * ns] = s_im[g][:, 0:ns]
            ent_scr[g, rb, 3 * ns:4 * ns] = s_im[g][:, ns:2 * ns]
            loc_re = jnp.where(is_fwd, loc_scr[g, rf, 0:2 * ns], loc_scr[g, rb, 0:2 * ns])
            loc_im = jnp.where(is_fwd, loc_scr[g, rf, 2 * ns:4 * ns], loc_scr[g, rb, 2 * ns:4 * ns])
            n_re = l_re[g] * s_re[g] - l_im[g] * s_im[g] + loc_re
            n_im = l_re[g] * s_im[g] + l_im[g] * s_re[g] + loc_im
            s_re[g], s_im[g] = n_re, n_im
    if not has_h0:
        for g in range(ng):
            fin_ref[g, 0] = s_re[g]
            fin_ref[g, 1] = s_im[g]
    out_cols = [None] * t_blk
    for g in range(ng):
        y = (_dot(xg[g].astype(BF16), w_ref[g]) + _dot_nt(ent_scr[g].astype(BF16), cst_ref[g])
             + d_ref[g] * xg[g])
        zg = _gelu_tanh(y)
        for h in range(t_blk // per_half):
            zh = zg[:, h * LANES:(h + 1) * LANES]
            for j in range(per_half):
                t = h * per_half + j
                r = roll(zh, g - j)
                out_cols[t] = r if out_cols[t] is None else jnp.where(chunk == g, r, out_cols[t])
    for t in range(t_blk):
        for b in range(nblk):
            z_ref[pl.ds(b * t_blk + t, nseq, stride=seq_len), :] = out_cols[t][b * SUBLANES:b * SUBLANES + nseq]


def _ssm(p, sw, layer, nseq, seq_len, h0=None):
    n = p.shape[0]
    t_blk = SSM_T
    nblk = seq_len // t_blk
    bw = t_blk * SSM_GROUP
    nsg = SSM_WIDTH // LANES
    ng = LANES // SSM_GROUP
    per_step = min(nseq, SUBLANES)
    has_h0 = h0 is not None
    kern = functools.partial(_ssm_kernel, nseq=per_step, seq_len=seq_len, has_h0=has_h0)
    wspec = lambda a, b: pl.BlockSpec((None, None, ng, a, b), lambda s, g: (layer, g, 0, 0, 0))
    grp = lambda a: a.reshape((DEPTH, nsg, ng) + a.shape[2:])
    tok = per_step * seq_len
    in_specs = [pl.BlockSpec((tok, LANES), lambda s, g: (s, COL_SU // LANES + g)),
                wspec(bw, bw), wspec(bw, 4 * SSM_STATE), wspec(bw, 4 * SSM_STATE),
                wspec(2, 2 * SSM_STATE), wspec(1, bw)]
    args = [p, grp(sw['w']), grp(sw['bst']), grp(sw['cst']), grp(sw['lamt']), grp(sw['d'])]
    out_specs = [pl.BlockSpec((tok, LANES), lambda s, g: (s, g))]
    out_shape = [jax.ShapeDtypeStruct((n, SSM_WIDTH), F32)]
    st_block = (ng, 2, SUBLANES, 2 * SSM_STATE)
    if has_h0:
        in_specs.append(pl.BlockSpec((None, None) + st_block, lambda s, g: (layer, g, 0, 0, 0, 0)))
        args.append(h0)
    else:
        out_specs.append(pl.BlockSpec((None, None) + st_block, lambda s, g: (s, g, 0, 0, 0, 0)))
        out_shape.append(jax.ShapeDtypeStruct((nseq // per_step, nsg) + st_block, F32))
    rows = nblk * SUBLANES
    return pl.pallas_call(
        kern,
        grid=(nseq // per_step, nsg),
        in_specs=in_specs,
        out_specs=out_specs,
        out_shape=out_shape,
        scratch_shapes=[pltpu.VMEM((ng, rows, 4 * SSM_STATE), F32), pltpu.VMEM((ng, rows, 4 * SSM_STATE), F32),
                        pltpu.VMEM((t_blk, rows, LANES), F32)],
        compiler_params=_cparams(("parallel", "parallel")),
        name="ssm",
    )(*args)


def _hgrn_consts():
    c = HG_CHUNK
    i = np.arange(c)[:, None]
    r = np.arange(c)[None, :]
    m_all = np.zeros((2, HG_LEVELS, c, c), np.float32)
    lev_of = np.full((2, c, c), -1, np.int32)
    for lev in range(HG_LEVELS):
        n = 2 << lev
        half = n // 2
        mid = (i // n) * n + half
        upper = (i % n) >= half
        m_all[0, lev] = np.where(upper, (r >= mid) & (r <= i), (r > i) & (r < mid))
        m_all[1, lev] = np.where(upper, (r >= mid) & (r < i), (r >= i) & (r < mid))
        same = (i // n) == (r // n)
        r_upper = (r % n) >= half
        lev_of[0][same & upper & ~r_upper] = lev
        lev_of[1][same & ~upper & r_upper] = lev
    cum = np.stack([r <= i, r >= i]).astype(np.float32)
    m_small = np.concatenate([cum] + [m_all[:, lev] for lev in HG_MXU_LEVELS], axis=1)
    return (jnp.asarray(np.concatenate([m_small, m_small], axis=-1), BF16), jnp.asarray(lev_of))


def _hgrn_chunks(chains, states, m_ref, lev_ref):
    c = HG_CHUNK
    nb = c // SUBLANES
    w = []
    for d, key, r0, hq_ref, f_ref, hi_ref, lb, cum_ref in chains:
        q = _silu(hq_ref[pl.ds(r0, c), :])
        fg = lb + (1.0 - lb) * _sigmoid(f_ref[pl.ds(r0, c), :])
        k = 1.0 - fg
        log2f = jnp.log(fg) * (1.0 / math.log(2.0))
        v = hi_ref[pl.ds(r0, c), :]
        hi = log2f.astype(BF16)
        lo = (log2f - hi.astype(F32)).astype(BF16)
        sums = _dot(m_ref[d], jnp.concatenate([hi, lo], axis=0))
        w.append(dict(d=d, key=key, q=q, fg=fg, k=k, v=v, vb=v.astype(BF16), sums=sums, cum=sums[0:c],
                      cum_ref=cum_ref, att=[None] * nb))
    for x in w:
        x['cum_ref'][...] = x['cum']

    def select_into(x, rows, lev, a):
        att = x['att']
        for j in range(rows.stop // SUBLANES - rows.start // SUBLANES):
            i = rows.start // SUBLANES + j
            blk = a[j * SUBLANES:(j + 1) * SUBLANES]
            mask = lev_ref[x['d'], i * SUBLANES:(i + 1) * SUBLANES, :] == lev
            att[i] = jnp.where(mask, blk, 0.0 if att[i] is None else att[i])

    full = slice(0, c)
    prods = [_dot_nt((x['q'] * x['fg']).astype(BF16), x['k'].astype(BF16)) for x in w]
    for x, a in zip(w, prods):
        select_into(x, full, 0, a)
    for n_lev, lev in enumerate(HG_MXU_LEVELS):
        prods = []
        for x in w:
            g = jnp.exp2(x['sums'][(n_lev + 1) * c:(n_lev + 2) * c])
            prods.append(_dot_nt((x['q'] * g).astype(BF16), (x['k'] * g).astype(BF16)))
        for x, a in zip(w, prods):
            select_into(x, full, lev, a)
    for lev in range(HG_MXU_LEVELS[-1] + 1, HG_LEVELS):
        half = 1 << lev
        n = 2 * half
        prods = []
        for x in w:
            d, q, k, cum = x['d'], x['q'], x['k'], x['cum']
            q_parts, k_parts, q_rows = [], [], []
            for j in range(c // n):
                lower = slice(j * n, j * n + half)
                upper = slice(j * n + half, (j + 1) * n)
                qr, kr = (upper, lower) if d == 0 else (lower, upper)
                ref_row = j * n + half - 1 if d == 0 else j * n + half
                ref = jnp.broadcast_to(x['cum_ref'][ref_row:ref_row + 1, :], (half, HG_KDIM))
                q_parts.append(q[qr] * jnp.exp2(cum[qr] - ref))
                k_scaled = k[kr] * jnp.exp2(ref - cum[kr])
                k_parts += [k_scaled, k[qr]] if d == 0 else [k[qr], k_scaled]
                q_rows.append(qr)
            prods.append((q_rows, _dot_nt(jnp.concatenate(q_parts, axis=0).astype(BF16),
                                          jnp.concatenate(k_parts, axis=0).astype(BF16))))
        for x, (q_rows, a) in zip(w, prods):
            for j, qr in enumerate(q_rows):
                select_into(x, qr, lev, a[j * half:(j + 1) * half])
    for x in w:
        d, q, k, cum = x['d'], x['q'], x['k'], x['cum']
        last = c - 1 if d == 0 else 0
        tot_row = cum[last:last + 1, :]
        x['lhs'] = jnp.concatenate([jnp.concatenate(x['att'], axis=0).astype(BF16),
                                    (q * jnp.exp2(cum)).astype(BF16)], axis=1)
        x['kg'] = (k * jnp.exp2(tot_row - cum)).astype(BF16)
        x['dv'] = jnp.sum(q * k, axis=-1, keepdims=True) * x['v']
        x['tot_col'] = jnp.broadcast_to(jnp.exp2(tot_row), (c, HG_KDIM)).T
    states = list(states)
    outs = []
    for x in w:
        st = states[x['key']]
        outs.append(x['dv'] + _dot(x['lhs'], jnp.concatenate([x['vb'], st.astype(BF16)], axis=0)))
        states[x['key']] = st * x['tot_col'] + _dot_tn(x['kg'], x['vb'])
    return outs, states


def _hgrn_kernel(*refs, seq_len, nh, unroll, has_s0):
    if has_s0:
        (hq_ref, ff_ref, fb_ref, hi_ref, hgt_ref, lb_ref, ng_ref, m_ref, lev_ref, s0_ref,
         o_ref, of_scr, ob_scr, cum_scr) = refs
    else:
        (hq_ref, ff_ref, fb_ref, hi_ref, hgt_ref, lb_ref, ng_ref, m_ref, lev_ref,
         o_ref, sfin_ref, of_scr, ob_scr, cum_scr) = refs
    c = HG_CHUNK
    nch = seq_len // c
    head = lambda ref, h: ref.at[:, h * HG_KDIM:(h + 1) * HG_KDIM]

    def step(ci, carry):
        chains, rows = [], []
        for u in range(unroll):
            cf = ci * unroll + u
            rf = pl.multiple_of(cf * c, c)
            rb = pl.multiple_of((nch - 1 - cf) * c, c)
            for h in range(nh):
                hq, hi = head(hq_ref, h), head(hi_ref, h)
                slot = (u * nh + h) * 2
                chains.append((0, 2 * h, rf, hq, head(ff_ref, h), hi, lb_ref[h, 0:1, :], cum_scr.at[slot]))
                chains.append((1, 2 * h + 1, rb, hq, head(fb_ref, h), hi, lb_ref[h, 1:2, :], cum_scr.at[slot + 1]))
                rows += [(head(of_scr, h), rf), (head(ob_scr, h), rb)]
        outs, states = _hgrn_chunks(chains, carry, m_ref, lev_ref)
        for (scr, r0), o in zip(rows, outs):
            scr[pl.ds(r0, c), :] = o
        return tuple(states)

    if has_s0:
        init = tuple(s0_ref[d, h] for h in range(nh) for d in range(2))
    else:
        init = tuple(jnp.zeros((HG_KDIM, HG_VDIM), F32) for _ in range(2 * nh))
    if nch == unroll:
        fin = step(0, init)
    else:
        fin = lax.fori_loop(0, nch // unroll, step, init)
    if not has_s0:
        for h in range(nh):
            for d in range(2):
                sfin_ref[d, h] = fin[2 * h + d]

    for h in range(nh):
        o = head(of_scr, h)[...] + head(ob_scr, h)[...]
        ms = jnp.mean(o * o, axis=-1, keepdims=True)
        y = o * lax.rsqrt(ms + EPS) * ng_ref[...]
        head(o_ref, h)[...] = (y * _silu(head(hgt_ref, h)[...])).astype(BF16)


def _hgrn(p, lb, norm_g, m_all, lev_of, layer, seq_len, s0=None):
    n = p.shape[0]
    nseq = n // seq_len
    has_s0 = s0 is not None
    nch = seq_len // HG_CHUNK
    unroll = min(nch, HG_CHAINS // 2)
    nh = HG_CHAINS // (2 * unroll)
    kern = functools.partial(_hgrn_kernel, seq_len=seq_len, nh=nh, unroll=unroll, has_s0=has_s0)
    base = COL_HG // (nh * LANES)
    per = HG_HEADS // nh

    def col(k):
        return pl.BlockSpec((seq_len, nh * LANES), lambda b, h, k=k: (b, base + k * per + h))

    c = HG_CHUNK
    in_specs = [col(0), col(1), col(2), col(3), col(4),
                pl.BlockSpec((None, nh, 2, HG_KDIM), lambda b, h: (layer, h, 0, 0)),
                pl.BlockSpec((None, 1, HG_VDIM), lambda b, h: (layer, 0, 0)),
                pl.BlockSpec((2, (len(HG_MXU_LEVELS) + 1) * c, 2 * c), lambda b, h: (0, 0, 0)),
                pl.BlockSpec((2, c, c), lambda b, h: (0, 0, 0))]
    args = [p, p, p, p, p, lb, norm_g, m_all, lev_of]
    out_specs = [pl.BlockSpec((seq_len, nh * HG_VDIM), lambda b, h: (b, h))]
    out_shape = [jax.ShapeDtypeStruct((n, HG_VW), BF16)]
    if has_s0:
        in_specs.append(pl.BlockSpec((None, None, 2, nh, HG_KDIM, HG_VDIM), lambda b, h: (b, layer, 0, h, 0, 0)))
        args.append(s0)
    else:
        out_specs.append(pl.BlockSpec((None, 2, nh, HG_KDIM, HG_VDIM), lambda b, h: (b, 0, h, 0, 0)))
        out_shape.append(jax.ShapeDtypeStruct((nseq, 2, HG_HEADS, HG_KDIM, HG_VDIM), F32))
    return pl.pallas_call(
        kern,
        grid=(nseq, per),
        in_specs=in_specs,
        out_specs=out_specs,
        out_shape=out_shape,
        scratch_shapes=[pltpu.VMEM((seq_len, nh * HG_VDIM), F32), pltpu.VMEM((seq_len, nh * HG_VDIM), F32),
                        pltpu.VMEM((HG_CHAINS, HG_CHUNK, HG_KDIM), F32)],
        compiler_params=_cparams(("parallel", "parallel")),
        name="hgrn",
    )(*args)


ROUTER_ROWS = 32


def _router(logits_t):
    row = lax.broadcasted_iota(jnp.int32, logits_t.shape, 0)
    big = jnp.int32(ROUTER_ROWS)
    red = dict(axis=0, keepdims=True)
    is_g = row < N_GROUPS
    gl = jnp.where(is_g, logits_t, -jnp.inf)
    gmax = jnp.max(gl, **red)
    gsum = jnp.sum(jnp.exp(gl - gmax), **red)
    g_idx = jnp.min(jnp.where(is_g & (gl == gmax), row, big), **red)
    g_prob = 1.0 / gsum
    e_row = row - ROUTER_OFF
    sel = (e_row >= 0) & (e_row < N_EXPERTS) & ((e_row >> 2) == g_idx)
    el = jnp.where(sel, logits_t, -jnp.inf)
    emax = jnp.max(el, **red)
    eexp = jnp.exp(el - emax)
    ep = eexp / jnp.sum(eexp, **red)
    p1 = jnp.max(ep, **red)
    i1 = jnp.min(jnp.where(sel & (ep == p1), row, big), **red)
    rest = sel & (row != i1)
    ep2 = jnp.where(rest, ep, -1.0)
    p2 = jnp.max(ep2, **red)
    i2 = jnp.min(jnp.where(rest & (ep2 == p2), row, big), **red)
    den = p1 + p2
    w1 = g_prob * (p1 / den)
    w2 = g_prob * (p2 / den)
    return jnp.where(row == i1, w1, jnp.where(row == i2, w2, 0.0))


def _merge_kernel(x_ref, att_ref, z_ref, hg_ref, mg_ref, mod_ref, n2_ref,
                  wao_ref, wga_ref, wgb_ref, who_ref, wout_ref, wr_ref, br_ref,
                  xm_ref, h2_ref, comb_ref):
    y_att = _dot(att_ref[...], wao_ref[...])
    z = z_ref[...].astype(BF16)
    y_ssm = _dot(z, wga_ref[...]) * _sigmoid(_dot(z, wgb_ref[...]))
    y_hg = _dot(hg_ref[...], who_ref[...])
    gate = lambda k: _sigmoid(mg_ref[:, k * D_MODEL:(k + 1) * D_MODEL].astype(F32))
    merged = gate(0) * y_att + gate(1) * y_ssm + gate(2) * y_hg
    xm = x_ref[...] + mod_ref[2:3, :] * _dot(merged.astype(BF16), wout_ref[...])
    xm_ref[...] = xm
    ms = jnp.mean(xm * xm, axis=-1, keepdims=True)
    h2 = xm * lax.rsqrt(ms + EPS) * n2_ref[...] * (1.0 + mod_ref[4:5, :]) + mod_ref[3:4, :]
    h2_hi = h2.astype(BF16)
    h2_ref[...] = h2_hi
    h2_lo = (h2 - h2_hi.astype(F32)).astype(BF16)
    logits = _dot(jnp.concatenate([h2_hi, h2_lo, h2_hi], axis=1), wr_ref[...]) + br_ref[...]
    comb_t = _router(logits.T[0:ROUTER_ROWS])
    comb_t = jnp.concatenate([comb_t, jnp.zeros((LANES - ROUTER_ROWS, comb_t.shape[1]), F32)], axis=0)
    comb_ref[...] = comb_t.T


def _merge(x, att, z, hg, p, mod, n2g, w, layer, seq_len, cond_row0):
    n = x.shape[0]
    tm = 256
    rows_per_cond = seq_len if cond_row0 else n
    mod_idx = (lambda i: (layer, cond_row0 + (i * tm) // rows_per_cond, 0, 0))
    row = lambda width: pl.BlockSpec((tm, width), lambda i: (i, 0))
    wspec = lambda a, b: pl.BlockSpec((None, a, b), lambda i: (layer, 0, 0))
    return pl.pallas_call(
        _merge_kernel,
        grid=(n // tm,),
        in_specs=[
            row(D_MODEL), row(Q_W), row(SSM_WIDTH), row(HG_VW),
            row(MG_W),
            pl.BlockSpec((None, None, 6, D_MODEL), mod_idx),
            wspec(1, D_MODEL),
            wspec(Q_W, D_MODEL), wspec(SSM_WIDTH, D_MODEL), wspec(SSM_WIDTH, D_MODEL),
            wspec(HG_VW, D_MODEL), wspec(D_MODEL, D_MODEL), wspec(3 * D_MODEL, LANES), wspec(1, LANES),
        ],
        out_specs=[row(D_MODEL), row(D_MODEL), row(LANES)],
        out_shape=[
            jax.ShapeDtypeStruct((n, D_MODEL), F32),
            jax.ShapeDtypeStruct((n, D_MODEL), BF16),
            jax.ShapeDtypeStruct((n, LANES), F32),
        ],
        compiler_params=_cparams(("parallel",)),
        name="merge",
    )(x, att, z, hg, p, mod, n2g, w['w_attn_o'], w['w_glu_a'], w['w_glu_b'], w['w_hg_o'], w['w_out'],
      w['w_route'], w['b_route'])


def _moe_kernel(h_ref, comb_ref, xm_ref, mod_ref, wg_ref, wu_ref, wd_ref, o_ref, act_scr):
    comb = comb_ref[...]
    lane = lax.broadcasted_iota(jnp.int32, comb.shape, 1)
    h = h_ref[...]
    gw = EXPERTS_PER_GROUP * EXPERT_FF
    for grp in range(N_GROUPS):
        cols = slice(grp * gw, (grp + 1) * gw)
        hg = _dot(h, wg_ref[:, cols])
        hu = _dot(h, wu_ref[:, cols])
        for e in range(EXPERTS_PER_GROUP):
            cw = jnp.sum(jnp.where(lane == ROUTER_OFF + grp * EXPERTS_PER_GROUP + e, comb, 0.0),
                         axis=-1, keepdims=True)
            sl = slice(e * EXPERT_FF, (e + 1) * EXPERT_FF)
            act_scr[:, grp * gw + e * EXPERT_FF:grp * gw + (e + 1) * EXPERT_FF] = (
                _silu(hg[:, sl]) * hu[:, sl] * cw).astype(BF16)
    o_ref[...] = xm_ref[...] + mod_ref[5:6, :] * _dot(act_scr[...], wd_ref[...])


def _moe(h2, comb, xm, mod, w, layer, seq_len, cond_row0):
    n = h2.shape[0]
    tm = 512
    ff_all = N_EXPERTS * EXPERT_FF
    rows_per_cond = seq_len if cond_row0 else n
    mod_idx = (lambda i: (layer, cond_row0 + (i * tm) // rows_per_cond, 0, 0))
    once = pl.Buffered(1)
    return pl.pallas_call(
        _moe_kernel,
        grid=(n // tm,),
        in_specs=[
            pl.BlockSpec((tm, D_MODEL), lambda i: (i, 0)),
            pl.BlockSpec((tm, LANES), lambda i: (i, 0)),
            pl.BlockSpec((tm, D_MODEL), lambda i: (i, 0)),
            pl.BlockSpec((None, None, 6, D_MODEL), mod_idx),
            pl.BlockSpec((None, D_MODEL, ff_all), lambda i: (layer, 0, 0), pipeline_mode=once),
            pl.BlockSpec((None, D_MODEL, ff_all), lambda i: (layer, 0, 0), pipeline_mode=once),
            pl.BlockSpec((None, ff_all, D_MODEL), lambda i: (layer, 0, 0), pipeline_mode=once),
        ],
        out_specs=pl.BlockSpec((tm, D_MODEL), lambda i: (i, 0)),
        out_shape=jax.ShapeDtypeStruct((n, D_MODEL), F32),
        scratch_shapes=[pltpu.VMEM((tm, ff_all), BF16)],
        compiler_params=_cparams(("arbitrary",)),
        name="moe",
    )(h2, comb, xm, mod, w['w_e_gate'], w['w_e_up'], w['w_e_down'])


def _layer(x, layer, w, seq_len, cond_row0, cache):
    mg, p = _in_proj(x, w['mod'], w['norm1_g'], w['w_in_mg'], w['w_in_p'], layer, seq_len, cond_row0)
    nseq = x.shape[0] // seq_len
    if cache is None:
        att, k_new, v_new = _attn_ctx(w['attn_sink'], p, w['q_norm_g'], w['k_norm_g'], layer)
        z, ssm_fin = _ssm(p, w['ssm'], layer, nseq, seq_len)
        hg, hg_fin = _hgrn(p, w['hg_lb'], w['hg_norm_g'], w['hg_m'], w['hg_lev'], layer, seq_len)
        ctx = (k_new, v_new, ssm_fin, hg_fin)
    else:
        cache_k, cache_v, h0, s0 = cache
        att = _attn_lat(w['attn_sink'], p, cache_k, cache_v, w['q_norm_g'], w['k_norm_g'],
                        w['rope_cos'], w['rope_sin'], layer)
        (z,) = _ssm(p, w['ssm'], layer, nseq, seq_len, h0=h0)
        (hg,) = _hgrn(p, w['hg_lb'], w['hg_norm_g'], w['hg_m'], w['hg_lev'], layer, seq_len, s0=s0)
        ctx = None
    xm, h2, comb = _merge(x, att, z, hg, mg, w['mod'], w['norm2_g'], w, layer, seq_len, cond_row0)
    x = _moe(h2, comb, xm, w['mod'], w, layer, seq_len, cond_row0)
    return x, ctx


def kernel(x_prompt, x_sample, cache_k, cache_v, state_ssm_re, state_ssm_im, state_hgrn, c, c_ctx, w_mod, b_mod, norm1_g, norm2_g, w_in, q_norm_g, k_norm_g, attn_sink, w_attn_o, ssm_a_re, ssm_a_im, ssm_log_dt, ssm_b_re, ssm_b_im, ssm_c_re, ssm_c_im, ssm_d, w_glu_a, w_glu_b, hg_lb, hg_norm_g, w_hg_o, w_out, w_group, b_group, w_router, b_router, w_e_gate, w_e_up, w_e_down):
    w = {}
    w['mod'] = _modulation(c, c_ctx, w_mod, b_mod)
    w['norm1_g'] = norm1_g.reshape(DEPTH, 1, D_MODEL)
    w['norm2_g'] = norm2_g.reshape(DEPTH, 1, D_MODEL)
    w['w_in_mg'] = w_in[:, :, 3840:].astype(BF16)
    w['w_in_p'] = jnp.concatenate([w_in[:, :, 1280:3840], w_in[:, :, 0:768], w_in[:, :, 768:1280]],
                                  axis=-1).astype(BF16)
    w['q_norm_g'] = jnp.tile(q_norm_g, (1, LANES // HEAD_DIM)).reshape(DEPTH, 1, LANES)
    w['k_norm_g'] = jnp.tile(k_norm_g, (1, LANES // HEAD_DIM)).reshape(DEPTH, 1, LANES)
    w['attn_sink'] = attn_sink
    w['rope_cos'], w['rope_sin'] = _rope_tables()
    w['ssm'] = _ssm_prep(ssm_a_re, ssm_a_im, ssm_log_dt, ssm_b_re, ssm_b_im, ssm_c_re, ssm_c_im, ssm_d)
    w['hg_lb'] = _lower_bounds(hg_lb)
    w['hg_norm_g'] = hg_norm_g.reshape(DEPTH, 1, HG_VDIM)
    w['hg_m'], w['hg_lev'] = _hgrn_consts()
    for name, val in (('w_attn_o', w_attn_o), ('w_glu_a', w_glu_a), ('w_glu_b', w_glu_b), ('w_hg_o', w_hg_o),
                      ('w_out', w_out)):
        w[name] = val.astype(BF16)
    ff_all = N_EXPERTS * EXPERT_FF
    w['w_e_gate'] = w_e_gate.astype(BF16).transpose(0, 2, 1, 3).reshape(DEPTH, D_MODEL, ff_all)
    w['w_e_up'] = w_e_up.astype(BF16).transpose(0, 2, 1, 3).reshape(DEPTH, D_MODEL, ff_all)
    w['w_e_down'] = w_e_down.astype(BF16).reshape(DEPTH, ff_all, D_MODEL)
    pad = LANES - N_GROUPS - N_EXPERTS
    w_route = jnp.concatenate([w_group, w_router, jnp.zeros((DEPTH, D_MODEL, pad), F32)], axis=-1)
    w_route_hi = w_route.astype(BF16)
    w_route_lo = (w_route - w_route_hi.astype(F32)).astype(BF16)
    w['w_route'] = jnp.concatenate([w_route_hi, w_route_hi, w_route_lo], axis=1)
    w['b_route'] = jnp.concatenate([b_group, b_router, jnp.zeros((DEPTH, pad), F32)], axis=-1).reshape(DEPTH, 1, LANES)

    ck = cache_k.reshape(DEC_BATCH, DEPTH, PAST_LEN, KV_W)
    cv = cache_v.reshape(DEC_BATCH, DEPTH, PAST_LEN, KV_W)

    ng = LANES // SSM_GROUP
    h0 = jnp.stack([state_ssm_re, state_ssm_im]).transpose(2, 4, 0, 1, 3, 5).reshape(
        DEPTH, SSM_GROUPS // ng, ng, 2, DEC_BATCH, 2 * SSM_STATE)
    h0 = jnp.pad(h0, ((0, 0),) * 4 + ((0, SUBLANES - DEC_BATCH), (0, 0)))

    xp = x_prompt.reshape(BATCH * SEQ, D_MODEL)
    xs = x_sample.reshape(DEC_BATCH * DEC_SEQ, D_MODEL)
    new_k, new_v, new_re, new_im, new_hg = [], [], [], [], []
    for l in range(DEPTH):
        xp, ctx = _layer(xp, l, w, SEQ, 0, None)
        new_k.append(ctx[0].reshape(BATCH, SEQ, N_KV_HEADS, HEAD_DIM))
        new_v.append(ctx[1].reshape(BATCH, SEQ, N_KV_HEADS, HEAD_DIM))
        fin = ctx[2].reshape(BATCH // SUBLANES, SSM_GROUPS // ng, ng, 2, SUBLANES, 2, SSM_STATE)
        fin = fin.transpose(3, 0, 4, 5, 1, 2, 6).reshape(2, BATCH, 2, SSM_GROUPS, SSM_STATE)
        new_re.append(fin[0])
        new_im.append(fin[1])
        new_hg.append(ctx[3])
        xs, _ = _layer(xs, l, w, DEC_SEQ, 1, (ck, cv, h0, state_hgrn))
    return (xp.reshape(BATCH, SEQ, D_MODEL), xs.reshape(DEC_BATCH, DEC_SEQ, D_MODEL),
            jnp.stack(new_k, axis=1), jnp.stack(new_v, axis=1), jnp.stack(new_re, axis=1),
            jnp.stack(new_im, axis=1), jnp.stack(new_hg, axis=1))
```

```python
import functools
import math

import numpy as np
import jax
import jax.numpy as jnp
from jax import lax
from jax.experimental import pallas as pl
from jax.experimental.pallas import tpu as pltpu

F32 = jnp.float32
BF16 = jnp.bfloat16

D_MODEL = 1024
BATCH = 32
SEQ = 256
DEPTH = 4
DEC_BATCH = 4
DEC_SEQ = 1024
PAST_LEN = 512
GRID_W = 64
N_HEADS = 8
N_KV_HEADS = 2
HEAD_DIM = 64
KV_GROUP = N_HEADS // N_KV_HEADS
WINDOW = 128
LAT_QBLOCK = 128
ROPE_BASE = 10000.0
SSM_WIDTH = 512
SSM_GROUP = 16
SSM_GROUPS = SSM_WIDTH // SSM_GROUP
SSM_STATE = 64
HG_HEADS = 4
HG_KDIM = 128
HG_VDIM = 128
N_BRANCHES = 3
N_GROUPS = 4
EXPERTS_PER_GROUP = 4
N_EXPERTS = N_GROUPS * EXPERTS_PER_GROUP
EXPERT_FF = 256
EPS = 1e-6
Q_W = N_HEADS * HEAD_DIM
KV_W = N_KV_HEADS * HEAD_DIM
HG_KW = HG_HEADS * HG_KDIM
HG_VW = HG_HEADS * HG_VDIM

LANES = 128
SUBLANES = 8
VMEM_LIMIT = 56 * 1024 * 1024

MG_W = N_BRANCHES * D_MODEL
COL_HG = 0
COL_AQ = COL_HG + 3 * HG_KW + 2 * HG_VW
COL_AK = COL_AQ + Q_W
COL_SU = COL_AK + 2 * KV_W
P_WIDTH = COL_SU + SSM_WIDTH
IN_WIDTH = P_WIDTH
INPROJ_STEPS = 3

SSM_T = 16
SSM_PREP_GROUPS = 4
HG_CHUNK = 128
HG_LEVELS = 7
HG_CHAINS = 8
HG_MXU_LEVELS = (1, 2)
ROUTER_OFF = N_GROUPS
MASK_NEG = -1e30


def _cparams(sem):
    return pltpu.CompilerParams(dimension_semantics=sem, vmem_limit_bytes=VMEM_LIMIT)


def _sigmoid(x):
    return 1.0 / (1.0 + jnp.exp(-x))


def _silu(x):
    return x * _sigmoid(x)


def _gelu_tanh(x):
    return 0.5 * x * (1.0 + jnp.tanh(math.sqrt(2.0 / math.pi) * (x + 0.044715 * (x * x * x))))


def _dot(a, b):
    return jnp.dot(a, b, preferred_element_type=F32)


def _dot_nt(a, b):
    return lax.dot_general(a, b, (((1,), (1,)), ((), ())), preferred_element_type=F32)


def _dot_tn(a, b):
    return lax.dot_general(a, b, (((0,), (0,)), ((), ())), preferred_element_type=F32)


def _shift_lanes(a, s):
    if s == 0:
        return a
    lo, hi = a[:, :LANES], a[:, LANES:]
    zero = jnp.zeros_like(lo)
    lane = lax.broadcasted_iota(jnp.int32, lo.shape, 1)
    if s > 0:
        if s >= LANES:
            t = s - LANES
            out_hi = lo if t == 0 else jnp.where(lane >= t, pltpu.roll(lo, t, 1), 0.0)
            return jnp.concatenate([zero, out_hi], axis=1)
        r_lo, r_hi = pltpu.roll(lo, s, 1), pltpu.roll(hi, s, 1)
        return jnp.concatenate([jnp.where(lane >= s, r_lo, 0.0), jnp.where(lane >= s, r_hi, r_lo)], axis=1)
    s = -s
    if s >= LANES:
        t = s - LANES
        out_lo = hi if t == 0 else jnp.where(lane < LANES - t, pltpu.roll(hi, LANES - t, 1), 0.0)
        return jnp.concatenate([out_lo, zero], axis=1)
    r_lo, r_hi = pltpu.roll(lo, LANES - s, 1), pltpu.roll(hi, LANES - s, 1)
    keep = lane < LANES - s
    return jnp.concatenate([jnp.where(keep, r_lo, r_hi), jnp.where(keep, r_hi, 0.0)], axis=1)


def _ssm_prep_kernel(*refs):
    for g in range(SSM_PREP_GROUPS):
        _ssm_prep_group(*[r.at[g] for r in refs])


def _ssm_prep_group(are_ref, aim_ref, ldt_ref, btre_ref, btim_ref, cre_ref, cim_ref,
                    w_ref, bst_ref, cst_ref, lamt_ref):
    t_blk = SSM_T
    ns = SSM_STATE
    hp = lax.Precision.HIGHEST
    jf = lax.broadcasted_iota(jnp.int32, (2 * t_blk, ns), 0).astype(F32)
    bt_re = btre_ref[...]
    bt_im = btim_ref[...]
    c_re = cre_ref[...]
    c_im = cim_ref[...]
    p_re, p_im, bb_re, bb_im, kt = [], [], [], [], []
    for d in range(2):
        a_re = are_ref[d:d + 1, :]
        a_im = aim_ref[d:d + 1, :]
        dt = jnp.exp(ldt_ref[d:d + 1, :])
        mag = jnp.exp(jf * (a_re * dt))
        ang = jf * (a_im * dt)
        p_re.append(mag * jnp.cos(ang))
        p_im.append(mag * jnp.sin(ang))
        nr = p_re[d][1:2] - 1.0
        ni = p_im[d][1:2]
        den = a_re * a_re + a_im * a_im
        f_re = (nr * a_re + ni * a_im) / den
        f_im = (ni * a_re - nr * a_im) / den
        bb_re.append(f_re * bt_re - f_im * bt_im)
        bb_im.append(f_re * bt_im + f_im * bt_re)
        lags = range(t_blk) if d == 0 else range(t_blk - 1, -1, -1)
        a_parts_re = [c_re * p_re[d][j:j + 1] - c_im * p_im[d][j:j + 1] for j in lags]
        a_parts_im = [c_re * p_im[d][j:j + 1] + c_im * p_re[d][j:j + 1] for j in lags]
        nt = (((1,), (1,)), ((), ()))
        kt.append(lax.dot_general(bb_re[d], jnp.concatenate(a_parts_re, axis=0), nt,
                                  precision=hp, preferred_element_type=F32)
                  - lax.dot_general(bb_im[d], jnp.concatenate(a_parts_im, axis=0), nt,
                                    precision=hp, preferred_element_type=F32))
        lamt_ref[0:1, d * ns:(d + 1) * ns] = p_re[d][t_blk:t_blk + 1]
        lamt_ref[1:2, d * ns:(d + 1) * ns] = p_im[d][t_blk:t_blk + 1]
    for t in range(t_blk):
        rows = slice(t * SSM_GROUP, (t + 1) * SSM_GROUP)
        w_ref[rows, :] = (_shift_lanes(kt[0], t * SSM_GROUP)
                          + _shift_lanes(kt[1], -(t_blk - 1 - t) * SSM_GROUP)).astype(BF16)
        ef, eb = t_blk - 1 - t, t
        bst = [bb_re[0] * p_re[0][ef:ef + 1] - bb_im[0] * p_im[0][ef:ef + 1],
               bb_re[1] * p_re[1][eb:eb + 1] - bb_im[1] * p_im[1][eb:eb + 1],
               bb_re[0] * p_im[0][ef:ef + 1] + bb_im[0] * p_re[0][ef:ef + 1],
               bb_re[1] * p_im[1][eb:eb + 1] + bb_im[1] * p_re[1][eb:eb + 1]]
        bst_ref[rows, :] = jnp.concatenate(bst, axis=1).astype(BF16)
        ef, eb = t + 1, t_blk - t
        cst = [c_re * p_re[0][ef:ef + 1] - c_im * p_im[0][ef:ef + 1],
               c_re * p_re[1][eb:eb + 1] - c_im * p_im[1][eb:eb + 1],
               -(c_re * p_im[0][ef:ef + 1] + c_im * p_re[0][ef:ef + 1]),
               -(c_re * p_im[1][eb:eb + 1] + c_im * p_re[1][eb:eb + 1])]
        cst_ref[rows, :] = jnp.concatenate(cst, axis=1).astype(BF16)


def _ssm_prep(ssm_a_re, ssm_a_im, ssm_log_dt, ssm_b_re, ssm_b_im, ssm_c_re, ssm_c_im, ssm_d):
    t_blk = SSM_T
    bw = t_blk * SSM_GROUP
    lgd = lambda a: a.transpose(0, 2, 1, 3)
    ldt = jnp.broadcast_to(ssm_log_dt[..., None], ssm_a_re.shape)
    per = SSM_PREP_GROUPS
    vec = pl.BlockSpec((None, per, 2, SSM_STATE), lambda l, g: (l, g, 0, 0))
    mat = pl.BlockSpec((None, per, SSM_GROUP, SSM_STATE), lambda l, g: (l, g, 0, 0))
    op = lambda width: pl.BlockSpec((None, per, bw, width), lambda l, g: (l, g, 0, 0))
    w, bst, cst, lamt = pl.pallas_call(
        _ssm_prep_kernel,
        grid=(DEPTH, SSM_GROUPS // per),
        in_specs=[vec, vec, vec, mat, mat, mat, mat],
        out_specs=[op(bw), op(4 * SSM_STATE), op(4 * SSM_STATE),
                   pl.BlockSpec((None, per, 2, 2 * SSM_STATE), lambda l, g: (l, g, 0, 0))],
        out_shape=[
            jax.ShapeDtypeStruct((DEPTH, SSM_GROUPS, bw, bw), BF16),
            jax.ShapeDtypeStruct((DEPTH, SSM_GROUPS, bw, 4 * SSM_STATE), BF16),
            jax.ShapeDtypeStruct((DEPTH, SSM_GROUPS, bw, 4 * SSM_STATE), BF16),
            jax.ShapeDtypeStruct((DEPTH, SSM_GROUPS, 2, 2 * SSM_STATE), F32),
        ],
        compiler_params=_cparams(("parallel", "parallel")),
        name="ssm_prep",
    )(lgd(ssm_a_re), lgd(ssm_a_im), lgd(ldt), jnp.swapaxes(ssm_b_re, -1, -2), jnp.swapaxes(ssm_b_im, -1, -2),
      ssm_c_re, ssm_c_im)
    d_row = jnp.tile(ssm_d, (1, 1, t_blk)).reshape(DEPTH, SSM_GROUPS, 1, bw)
    return dict(w=w, bst=bst, cst=cst, lamt=lamt, d=d_row)


def _lb_kernel(x_ref, o_ref):
    x = x_ref[...]
    m = jnp.max(x, axis=0, keepdims=True)
    e = jnp.exp(x - m)
    s = e / jnp.sum(e, axis=0, keepdims=True)
    run = jnp.zeros_like(s[0:1])
    o_ref[0:1, :] = run
    for l in range(1, DEPTH):
        run = run + s[l:l + 1]
        o_ref[l:l + 1, :] = run


def _lower_bounds(hg_lb):
    w = 2 * HG_KW
    out = pl.pallas_call(
        _lb_kernel,
        out_shape=jax.ShapeDtypeStruct((DEPTH, w), F32),
        name="hgrn_lower_bounds",
    )(hg_lb.reshape(DEPTH, w))
    return out.reshape(DEPTH, 2, HG_HEADS, HG_KDIM).transpose(0, 2, 1, 3)


def _mod_kernel(c_ref, w_ref, b_ref, o_ref):
    c = c_ref[...]
    a = _silu(c).astype(BF16)
    o_ref[...] = _dot(a, w_ref[...].astype(BF16)) + b_ref[...]


def _modulation(c, c_ctx, w_mod, b_mod):
    rows = SUBLANES
    cond = jnp.concatenate([c_ctx[None, :], c, jnp.zeros((rows - 1 - DEC_BATCH, D_MODEL), F32)], axis=0)
    tn = D_MODEL
    out = pl.pallas_call(
        _mod_kernel,
        grid=(DEPTH, 6),
        in_specs=[
            pl.BlockSpec((rows, D_MODEL), lambda l, j: (0, 0)),
            pl.BlockSpec((None, D_MODEL, tn), lambda l, j: (l, 0, j)),
            pl.BlockSpec((None, 1, tn), lambda l, j: (l, 0, j)),
        ],
        out_specs=pl.BlockSpec((None, rows, tn), lambda l, j: (l, 0, j)),
        out_shape=jax.ShapeDtypeStruct((DEPTH, rows, 6 * D_MODEL), F32),
        compiler_params=_cparams(("parallel", "parallel")),
        name="modulation",
    )(cond, w_mod, b_mod.reshape(DEPTH, 1, 6 * D_MODEL))
    return out.reshape(DEPTH, rows, 6, D_MODEL)


def _inproj_kernel(x_ref, mod_ref, g_ref, wmg_ref, wp_ref, mg_ref, p_ref):
    x = x_ref[...]
    ms = jnp.mean(x * x, axis=-1, keepdims=True)
    y = x * lax.rsqrt(ms + EPS) * g_ref[...]
    h = (y * (1.0 + mod_ref[1:2, :]) + mod_ref[0:1, :]).astype(BF16)
    tn_mg = MG_W // INPROJ_STEPS
    tn_p = P_WIDTH // INPROJ_STEPS
    for c in range(INPROJ_STEPS):
        mg_ref[:, c * tn_mg:(c + 1) * tn_mg] = _dot(h, wmg_ref[:, c * tn_mg:(c + 1) * tn_mg]).astype(BF16)
    for c in range(INPROJ_STEPS):
        p_ref[:, c * tn_p:(c + 1) * tn_p] = _dot(h, wp_ref[:, c * tn_p:(c + 1) * tn_p])


def _in_proj(x, mod, norm_g, w_mg, w_p, layer, seq_len, cond_row0):
    n = x.shape[0]
    tm = 512
    rows_per_cond = seq_len if cond_row0 else n
    mod_idx = (lambda i: (layer, cond_row0 + (i * tm) // rows_per_cond, 0, 0))
    once = pl.Buffered(1)
    return pl.pallas_call(
        _inproj_kernel,
        grid=(n // tm,),
        in_specs=[
            pl.BlockSpec((tm, D_MODEL), lambda i: (i, 0)),
            pl.BlockSpec((None, None, 6, D_MODEL), mod_idx),
            pl.BlockSpec((None, 1, D_MODEL), lambda i: (layer, 0, 0)),
            pl.BlockSpec((None, D_MODEL, MG_W), lambda i: (layer, 0, 0), pipeline_mode=once),
            pl.BlockSpec((None, D_MODEL, P_WIDTH), lambda i: (layer, 0, 0), pipeline_mode=once),
        ],
        out_specs=[pl.BlockSpec((tm, MG_W), lambda i: (i, 0)), pl.BlockSpec((tm, P_WIDTH), lambda i: (i, 0))],
        out_shape=[jax.ShapeDtypeStruct((n, MG_W), BF16), jax.ShapeDtypeStruct((n, P_WIDTH), F32)],
        compiler_params=_cparams(("parallel",)),
        name="in_proj",
    )(x, mod, norm_g, w_mg, w_p)


def _pair_norm(x, g):
    xx = x * x
    s_a = jnp.sum(xx[:, :HEAD_DIM], axis=-1, keepdims=True)
    s_b = jnp.sum(xx[:, HEAD_DIM:], axis=-1, keepdims=True)
    lane = lax.broadcasted_iota(jnp.int32, x.shape, 1)
    ms = jnp.where(lane < HEAD_DIM, s_a, s_b) * (1.0 / HEAD_DIM)
    return x * lax.rsqrt(ms + EPS) * g


def _rope(x, cos, sin_signed):
    lane = lax.broadcasted_iota(jnp.int32, x.shape, 1)
    first = (lane & 31) < 16
    partner = jnp.where(first, pltpu.roll(x, LANES - 16, 1), pltpu.roll(x, 16, 1))
    return x * cos + partner * sin_signed


def _sink_column(sink_ref, layer, kh, t):
    row = lax.broadcasted_iota(jnp.int32, (KV_GROUP * t, 1), 0)
    col = jnp.full((KV_GROUP * t, 1), sink_ref[layer, kh * KV_GROUP], F32)
    for g in range(1, KV_GROUP):
        col = jnp.where(row >= g * t, sink_ref[layer, kh * KV_GROUP + g], col)
    return col


def _sink_softmax_av(heads):
    ms = []
    for scores, _, sink_col in heads:
        m = sink_col
        for s in scores:
            m = jnp.maximum(m, jnp.max(s, axis=-1, keepdims=True))
        ms.append(m)
    ps = [[jnp.exp(s - m) for s in scores] for (scores, _, _), m in zip(heads, ms)]
    outs = []
    for (scores, values, sink_col), m, p_list in zip(heads, ms, ps):
        den = jnp.exp(sink_col - m)
        o = None
        for p, v in zip(p_list, values):
            den = den + jnp.sum(p, axis=-1, keepdims=True)
            pv = _dot(p.astype(BF16), v.astype(BF16))
            o = pv if o is None else o + pv
        outs.append(o / den)
    return outs


def _attn_ctx_kernel(sink_ref, q_ref, kv_ref, gq_ref, gk_ref, att_ref, k_ref, v_ref, *, layer):
    t = q_ref.shape[0]
    kv = kv_ref[...]
    k = _pair_norm(kv[:, :KV_W], gk_ref[...])
    v = kv[:, KV_W:]
    k_ref[...] = k
    v_ref[...] = v
    q = q_ref[...]
    scale = 1.0 / math.sqrt(HEAD_DIM)
    work = []
    for kh in range(N_KV_HEADS):
        heads = []
        for j in range(KV_GROUP // 2):
            c0 = (kh * (KV_GROUP // 2) + j) * LANES
            qn = _pair_norm(q[:, c0:c0 + LANES], gq_ref[...])
            heads += [qn[:, :HEAD_DIM], qn[:, HEAD_DIM:]]
        qs = jnp.concatenate(heads, axis=0).astype(BF16)
        sl = slice(kh * HEAD_DIM, (kh + 1) * HEAD_DIM)
        s = _dot_nt(qs, k[:, sl].astype(BF16)) * scale
        work.append(([s], [v[:, sl]], _sink_column(sink_ref, layer, kh, t)))
    outs = [o[g * t:(g + 1) * t] for o in _sink_softmax_av(work) for g in range(KV_GROUP)]
    att_ref[...] = jnp.concatenate(outs, axis=-1).astype(BF16)


def _attn_ctx(sink, p, gq, gk, layer):
    n = p.shape[0]
    t = SEQ
    kern = functools.partial(_attn_ctx_kernel, layer=layer)
    return pl.pallas_call(
        kern,
        grid=(n // t,),
        in_specs=[
            pl.BlockSpec(memory_space=pltpu.SMEM),
            pl.BlockSpec((t, Q_W), lambda b: (b, COL_AQ // Q_W)),
            pl.BlockSpec((t, 2 * KV_W), lambda b: (b, COL_AK // (2 * KV_W))),
            pl.BlockSpec((None, 1, LANES), lambda b: (layer, 0, 0)),
            pl.BlockSpec((None, 1, LANES), lambda b: (layer, 0, 0)),
        ],
        out_specs=[
            pl.BlockSpec((t, Q_W), lambda b: (b, 0)),
            pl.BlockSpec((t, KV_W), lambda b: (b, 0)),
            pl.BlockSpec((t, KV_W), lambda b: (b, 0)),
        ],
        out_shape=[
            jax.ShapeDtypeStruct((n, Q_W), BF16),
            jax.ShapeDtypeStruct((n, KV_W), F32),
            jax.ShapeDtypeStruct((n, KV_W), F32),
        ],
        compiler_params=_cparams(("parallel",)),
        name="attn_ctx",
    )(sink, p, p, gq, gk)


def _attn_lat_kernel(sink_ref, q_ref, kv_ref, kc_ref, vc_ref, gq_ref, gk_ref, cos_ref, sin_ref, att_ref, *, layer):
    t = LAT_QBLOCK
    span = t + 2 * WINDOW
    start = pl.program_id(1) * t
    ks = pl.multiple_of(jnp.clip(start - WINDOW, 0, DEC_SEQ - span), WINDOW)
    q0 = pl.multiple_of(start, t)
    kvw = kv_ref[pl.ds(ks, span), :]
    k = _rope(_pair_norm(kvw[:, :KV_W], gk_ref[...]), cos_ref[pl.ds(ks, span), :], sin_ref[pl.ds(ks, span), :])
    v = kvw[:, KV_W:]
    cos_q = cos_ref[pl.ds(q0, t), :]
    sin_q = sin_ref[pl.ds(q0, t), :]
    qpos = start + (lax.broadcasted_iota(jnp.int32, (KV_GROUP * t, span), 0) & (t - 1))
    kpos = ks + lax.broadcasted_iota(jnp.int32, (KV_GROUP * t, span), 1)
    band = jnp.abs(qpos - kpos) <= WINDOW
    q = q_ref[...]
    kc = kc_ref[...]
    vc = vc_ref[...]
    scale = 1.0 / math.sqrt(HEAD_DIM)
    work = []
    for kh in range(N_KV_HEADS):
        heads = []
        for j in range(KV_GROUP // 2):
            c0 = (kh * (KV_GROUP // 2) + j) * LANES
            qn = _rope(_pair_norm(q[:, c0:c0 + LANES], gq_ref[...]), cos_q, sin_q)
            heads += [qn[:, :HEAD_DIM], qn[:, HEAD_DIM:]]
        qs = jnp.concatenate(heads, axis=0).astype(BF16)
        sl = slice(kh * HEAD_DIM, (kh + 1) * HEAD_DIM)
        s_w = jnp.where(band, _dot_nt(qs, k[:, sl].astype(BF16)) * scale, MASK_NEG)
        s_c = _dot_nt(qs, kc[:, sl].astype(BF16)) * scale
        work.append(([s_w, s_c], [v[:, sl], vc[:, sl]], _sink_column(sink_ref, layer, kh, t)))
    outs = [o[g * t:(g + 1) * t] for o in _sink_softmax_av(work) for g in range(KV_GROUP)]
    att_ref[...] = jnp.concatenate(outs, axis=-1).astype(BF16)


def _attn_lat(sink, p, cache_k, cache_v, gq, gk, cos_t, sin_t, layer):
    n = p.shape[0]
    t = LAT_QBLOCK
    nb = DEC_SEQ // t
    kern = functools.partial(_attn_lat_kernel, layer=layer)
    return pl.pallas_call(
        kern,
        grid=(DEC_BATCH, nb),
        in_specs=[
            pl.BlockSpec(memory_space=pltpu.SMEM),
            pl.BlockSpec((t, Q_W), lambda b, j: (b * nb + j, COL_AQ // Q_W)),
            pl.BlockSpec((DEC_SEQ, 2 * KV_W), lambda b, j: (b, COL_AK // (2 * KV_W))),
            pl.BlockSpec((None, None, PAST_LEN, KV_W), lambda b, j: (b, layer, 0, 0)),
            pl.BlockSpec((None, None, PAST_LEN, KV_W), lambda b, j: (b, layer, 0, 0)),
            pl.BlockSpec((None, 1, LANES), lambda b, j: (layer, 0, 0)),
            pl.BlockSpec((None, 1, LANES), lambda b, j: (layer, 0, 0)),
            pl.BlockSpec((DEC_SEQ, LANES), lambda b, j: (0, 0)),
            pl.BlockSpec((DEC_SEQ, LANES), lambda b, j: (0, 0)),
        ],
        out_specs=pl.BlockSpec((t, Q_W), lambda b, j: (b * nb + j, 0)),
        out_shape=jax.ShapeDtypeStruct((n, Q_W), BF16),
        compiler_params=_cparams(("parallel", "parallel")),
        name="attn_lat",
    )(sink, p, p, cache_k, cache_v, gq, gk, cos_t, sin_t)


def _rope_tables():
    pos = jnp.arange(DEC_SEQ)
    rows = (pos // GRID_W).astype(F32)
    cols = (pos % GRID_W).astype(F32)
    half = HEAD_DIM // 2
    freqs = ROPE_BASE ** (-jnp.arange(0, half, 2, dtype=F32) / half)
    ang_r = rows[:, None] * freqs[None, :]
    ang_c = cols[:, None] * freqs[None, :]
    ang = jnp.concatenate([ang_r, ang_r, ang_c, ang_c], axis=-1)
    sign = jnp.tile(jnp.concatenate([-jnp.ones((16,), F32), jnp.ones((16,), F32)]), 2)
    cos_t = jnp.tile(jnp.cos(ang), (1, LANES // HEAD_DIM))
    sin_t = jnp.tile(jnp.sin(ang) * sign[None, :], (1, LANES // HEAD_DIM))
    return cos_t, sin_t


def _ssm_kernel(*refs, nseq, seq_len, has_h0):
    if has_h0:
        x_ref, w_ref, bst_ref, cst_ref, lamt_ref, d_ref, h0_ref, z_ref, loc_scr, ent_scr, cols_scr = refs
    else:
        x_ref, w_ref, bst_ref, cst_ref, lamt_ref, d_ref, z_ref, fin_ref, loc_scr, ent_scr, cols_scr = refs
    ns = SSM_STATE
    t_blk = SSM_T
    nblk = seq_len // t_blk
    ng = LANES // SSM_GROUP
    rows = nblk * SUBLANES
    per_half = LANES // SSM_GROUP
    if nseq < SUBLANES:
        cols_scr[...] = jnp.zeros_like(cols_scr)
    for t in range(t_blk):
        for b in range(nblk):
            cols_scr[t, b * SUBLANES:b * SUBLANES + nseq, :] = x_ref[pl.ds(b * t_blk + t, nseq, stride=seq_len), :]
    cols = [cols_scr[t] for t in range(t_blk)]
    chunk = lax.broadcasted_iota(jnp.int32, (rows, LANES), 1) // SSM_GROUP
    is_fwd = lax.broadcasted_iota(jnp.int32, (SUBLANES, 2 * ns), 1) < ns

    def roll(a, chunks):
        shift = (chunks * SSM_GROUP) % LANES
        return a if shift == 0 else pltpu.roll(a, shift, 1)

    xg = []
    for g in range(ng):
        halves = []
        for h in range(t_blk // per_half):
            acc = None
            for j in range(per_half):
                r = roll(cols[h * per_half + j], j - g)
                acc = r if acc is None else jnp.where(chunk == j, r, acc)
            halves.append(acc)
        xg.append(jnp.concatenate(halves, axis=1))
        loc_scr[g] = _dot(xg[g].astype(BF16), bst_ref[g])
    l_re = [jnp.broadcast_to(lamt_ref[g, 0:1, :], (SUBLANES, 2 * ns)) for g in range(ng)]
    l_im = [jnp.broadcast_to(lamt_ref[g, 1:2, :], (SUBLANES, 2 * ns)) for g in range(ng)]
    if has_h0:
        s_re = [h0_ref[g, 0] for g in range(ng)]
        s_im = [h0_ref[g, 1] for g in range(ng)]
    else:
        s_re = [jnp.zeros((SUBLANES, 2 * ns), F32) for _ in range(ng)]
        s_im = [jnp.zeros((SUBLANES, 2 * ns), F32) for _ in range(ng)]
    for j in range(nblk):
        rf = slice(j * SUBLANES, (j + 1) * SUBLANES)
        rb = slice((nblk - 1 - j) * SUBLANES, (nblk - j) * SUBLANES)
        for g in range(ng):
            ent_scr[g, rf, 0:ns] = s_re[g][:, 0:ns]
            ent_scr[g, rb, ns:2 * ns] = s_re[g][:, ns:2 * ns]
            ent_scr[g, rf, 2 * ns:3 * ns] = s_im[g][:, 0:ns]
            ent_scr[g, rb, 3 * ns:4 * ns] = s_im[g][:, ns:2 * ns]
            loc_re = jnp.where(is_fwd, loc_scr[g, rf, 0:2 * ns], loc_scr[g, rb, 0:2 * ns])
            loc_im = jnp.where(is_fwd, loc_scr[g, rf, 2 * ns:4 * ns], loc_scr[g, rb, 2 * ns:4 * ns])
            n_re = l_re[g] * s_re[g] - l_im[g] * s_im[g] + loc_re
            n_im = l_re[g] * s_im[g] + l_im[g] * s_re[g] + loc_im
            s_re[g], s_im[g] = n_re, n_im
    if not has_h0:
        for g in range(ng):
            fin_ref[g, 0] = s_re[g]
            fin_ref[g, 1] = s_im[g]
    out_cols = [None] * t_blk
    for g in range(ng):
        y = (_dot(xg[g].astype(BF16), w_ref[g]) + _dot_nt(ent_scr[g].astype(BF16), cst_ref[g])
             + d_ref[g] * xg[g])
        zg = _gelu_tanh(y)
        for h in range(t_blk // per_half):
            zh = zg[:, h * LANES:(h + 1) * LANES]
            for j in range(per_half):
                t = h * per_half + j
                r = roll(zh, g - j)
                out_cols[t] = r if out_cols[t] is None else jnp.where(chunk == g, r, out_cols[t])
    for t in range(t_blk):
        for b in range(nblk):
            z_ref[pl.ds(b * t_blk + t, nseq, stride=seq_len), :] = out_cols[t][b * SUBLANES:b * SUBLANES + nseq]


def _ssm(p, sw, layer, nseq, seq_len, h0=None):
    n = p.shape[0]
    t_blk = SSM_T
    nblk = seq_len // t_blk
    bw = t_blk * SSM_GROUP
    nsg = SSM_WIDTH // LANES
    ng = LANES // SSM_GROUP
    per_step = min(nseq, SUBLANES)
    has_h0 = h0 is not None
    kern = functools.partial(_ssm_kernel, nseq=per_step, seq_len=seq_len, has_h0=has_h0)
    wspec = lambda a, b: pl.BlockSpec((None, None, ng, a, b), lambda s, g: (layer, g, 0, 0, 0))
    grp = lambda a: a.reshape((DEPTH, nsg, ng) + a.shape[2:])
    tok = per_step * seq_len
    in_specs = [pl.BlockSpec((tok, LANES), lambda s, g: (s, COL_SU // LANES + g)),
                wspec(bw, bw), wspec(bw, 4 * SSM_STATE), wspec(bw, 4 * SSM_STATE),
                wspec(2, 2 * SSM_STATE), wspec(1, bw)]
    args = [p, grp(sw['w']), grp(sw['bst']), grp(sw['cst']), grp(sw['lamt']), grp(sw['d'])]
    out_specs = [pl.BlockSpec((tok, LANES), lambda s, g: (s, g))]
    out_shape = [jax.ShapeDtypeStruct((n, SSM_WIDTH), F32)]
    st_block = (ng, 2, SUBLANES, 2 * SSM_STATE)
    if has_h0:
        in_specs.append(pl.BlockSpec((None, None) + st_block, lambda s, g: (layer, g, 0, 0, 0, 0)))
        args.append(h0)
    else:
        out_specs.append(pl.BlockSpec((None, None) + st_block, lambda s, g: (s, g, 0, 0, 0, 0)))
        out_shape.append(jax.ShapeDtypeStruct((nseq // per_step, nsg) + st_block, F32))
    rows = nblk * SUBLANES
    return pl.pallas_call(
        kern,
        grid=(nseq // per_step, nsg),
        in_specs=in_specs,
        out_specs=out_specs,
        out_shape=out_shape,
        scratch_shapes=[pltpu.VMEM((ng, rows, 4 * SSM_STATE), F32), pltpu.VMEM((ng, rows, 4 * SSM_STATE), F32),
                        pltpu.VMEM((t_blk, rows, LANES), F32)],
        compiler_params=_cparams(("parallel", "parallel")),
        name="ssm",
    )(*args)


def _hgrn_consts():
    c = HG_CHUNK
    i = np.arange(c)[:, None]
    r = np.arange(c)[None, :]
    m_all = np.zeros((2, HG_LEVELS, c, c), np.float32)
    lev_of = np.full((2, c, c), -1, np.int32)
    for lev in range(HG_LEVELS):
        n = 2 << lev
        half = n // 2
        mid = (i // n) * n + half
        upper = (i % n) >= half
        m_all[0, lev] = np.where(upper, (r >= mid) & (r <= i), (r > i) & (r < mid))
        m_all[1, lev] = np.where(upper, (r >= mid) & (r < i), (r >= i) & (r < mid))
        same = (i // n) == (r // n)
        r_upper = (r % n) >= half
        lev_of[0][same & upper & ~r_upper] = lev
        lev_of[1][same & ~upper & r_upper] = lev
    cum = np.stack([r <= i, r >= i]).astype(np.float32)
    m_small = np.concatenate([cum] + [m_all[:, lev] for lev in HG_MXU_LEVELS], axis=1)
    return (jnp.asarray(np.concatenate([m_small, m_small], axis=-1), BF16), jnp.asarray(lev_of))


def _hgrn_chunks(chains, states, m_ref, lev_ref):
    c = HG_CHUNK
    nb = c // SUBLANES
    w = []
    for d, key, r0, hq_ref, f_ref, hi_ref, lb, cum_ref in chains:
        q = _silu(hq_ref[pl.ds(r0, c), :])
        fg = lb + (1.0 - lb) * _sigmoid(f_ref[pl.ds(r0, c), :])
        k = 1.0 - fg
        log2f = jnp.log(fg) * (1.0 / math.log(2.0))
        v = hi_ref[pl.ds(r0, c), :]
        hi = log2f.astype(BF16)
        lo = (log2f - hi.astype(F32)).astype(BF16)
        sums = _dot(m_ref[d], jnp.concatenate([hi, lo], axis=0))
        w.append(dict(d=d, key=key, q=q, fg=fg, k=k, v=v, vb=v.astype(BF16), sums=sums, cum=sums[0:c],
                      cum_ref=cum_ref, att=[None] * nb))
    for x in w:
        x['cum_ref'][...] = x['cum']

    def select_into(x, rows, lev, a):
        att = x['att']
        for j in range(rows.stop // SUBLANES - rows.start // SUBLANES):
            i = rows.start // SUBLANES + j
            blk = a[j * SUBLANES:(j + 1) * SUBLANES]
            mask = lev_ref[x['d'], i * SUBLANES:(i + 1) * SUBLANES, :] == lev
            att[i] = jnp.where(mask, blk, 0.0 if att[i] is None else att[i])

    full = slice(0, c)
    prods = [_dot_nt((x['q'] * x['fg']).astype(BF16), x['k'].astype(BF16)) for x in w]
    for x, a in zip(w, prods):
        select_into(x, full, 0, a)
    for n_lev, lev in enumerate(HG_MXU_LEVELS):
        prods = []
        for x in w:
            g = jnp.exp2(x['sums'][(n_lev + 1) * c:(n_lev + 2) * c])
            prods.append(_dot_nt((x['q'] * g).astype(BF16), (x['k'] * g).astype(BF16)))
        for x, a in zip(w, prods):
            select_into(x, full, lev, a)
    for lev in range(HG_MXU_LEVELS[-1] + 1, HG_LEVELS):
        half = 1 << lev
        n = 2 * half
        prods = []
        for x in w:
            d, q, k, cum = x['d'], x['q'], x['k'], x['cum']
            q_parts, k_parts, q_rows = [], [], []
            for j in range(c // n):
                lower = slice(j * n, j * n + half)
                upper = slice(j * n + half, (j + 1) * n)
                qr, kr = (upper, lower) if d == 0 else (lower, upper)
                ref_row = j * n + half - 1 if d == 0 else j * n + half
                ref = jnp.broadcast_to(x['cum_ref'][ref_row:ref_row + 1, :], (half, HG_KDIM))
                q_parts.append(q[qr] * jnp.exp2(cum[qr] - ref))
                k_scaled = k[kr] * jnp.exp2(ref - cum[kr])
                k_parts += [k_scaled, k[qr]] if d == 0 else [k[qr], k_scaled]
                q_rows.append(qr)
            prods.append((q_rows, _dot_nt(jnp.concatenate(q_parts, axis=0).astype(BF16),
                                          jnp.concatenate(k_parts, axis=0).astype(BF16))))
        for x, (q_rows, a) in zip(w, prods):
            for j, qr in enumerate(q_rows):
                select_into(x, qr, lev, a[j * half:(j + 1) * half])
    for x in w:
        d, q, k, cum = x['d'], x['q'], x['k'], x['cum']
        last = c - 1 if d == 0 else 0
        tot_row = cum[last:last + 1, :]
        x['lhs'] = jnp.concatenate([jnp.concatenate(x['att'], axis=0).astype(BF16),
                                    (q * jnp.exp2(cum)).astype(BF16)], axis=1)
        x['kg'] = (k * jnp.exp2(tot_row - cum)).astype(BF16)
        x['dv'] = jnp.sum(q * k, axis=-1, keepdims=True) * x['v']
        x['tot_col'] = jnp.broadcast_to(jnp.exp2(tot_row), (c, HG_KDIM)).T
    states = list(states)
    outs = []
    for x in w:
        st = states[x['key']]
        outs.append(x['dv'] + _dot(x['lhs'], jnp.concatenate([x['vb'], st.astype(BF16)], axis=0)))
        states[x['key']] = st * x['tot_col'] + _dot_tn(x['kg'], x['vb'])
    return outs, states


def _hgrn_kernel(*refs, seq_len, nh, unroll, has_s0):
    if has_s0:
        (hq_ref, ff_ref, fb_ref, hi_ref, hgt_ref, lb_ref, ng_ref, m_ref, lev_ref, s0_ref,
         o_ref, of_scr, ob_scr, cum_scr) = refs
    else:
        (hq_ref, ff_ref, fb_ref, hi_ref, hgt_ref, lb_ref, ng_ref, m_ref, lev_ref,
         o_ref, sfin_ref, of_scr, ob_scr, cum_scr) = refs
    c = HG_CHUNK
    nch = seq_len // c
    head = lambda ref, h: ref.at[:, h * HG_KDIM:(h + 1) * HG_KDIM]

    def step(ci, carry):
        chains, rows = [], []
        for u in range(unroll):
            cf = ci * unroll + u
            rf = pl.multiple_of(cf * c, c)
            rb = pl.multiple_of((nch - 1 - cf) * c, c)
            for h in range(nh):
                hq, hi = head(hq_ref, h), head(hi_ref, h)
                slot = (u * nh + h) * 2
                chains.append((0, 2 * h, rf, hq, head(ff_ref, h), hi, lb_ref[h, 0:1, :], cum_scr.at[slot]))
                chains.append((1, 2 * h + 1, rb, hq, head(fb_ref, h), hi, lb_ref[h, 1:2, :], cum_scr.at[slot + 1]))
                rows += [(head(of_scr, h), rf), (head(ob_scr, h), rb)]
        outs, states = _hgrn_chunks(chains, carry, m_ref, lev_ref)
        for (scr, r0), o in zip(rows, outs):
            scr[pl.ds(r0, c), :] = o
        return tuple(states)

    if has_s0:
        init = tuple(s0_ref[d, h] for h in range(nh) for d in range(2))
    else:
        init = tuple(jnp.zeros((HG_KDIM, HG_VDIM), F32) for _ in range(2 * nh))
    if nch == unroll:
        fin = step(0, init)
    else:
        fin = lax.fori_loop(0, nch // unroll, step, init)
    if not has_s0:
        for h in range(nh):
            for d in range(2):
                sfin_ref[d, h] = fin[2 * h + d]

    for h in range(nh):
        o = head(of_scr, h)[...] + head(ob_scr, h)[...]
        ms = jnp.mean(o * o, axis=-1, keepdims=True)
        y = o * lax.rsqrt(ms + EPS) * ng_ref[...]
        head(o_ref, h)[...] = (y * _silu(head(hgt_ref, h)[...])).astype(BF16)


def _hgrn(p, lb, norm_g, m_all, lev_of, layer, seq_len, s0=None):
    n = p.shape[0]
    nseq = n // seq_len
    has_s0 = s0 is not None
    nch = seq_len // HG_CHUNK
    unroll = min(nch, HG_CHAINS // 2)
    nh = HG_CHAINS // (2 * unroll)
    kern = functools.partial(_hgrn_kernel, seq_len=seq_len, nh=nh, unroll=unroll, has_s0=has_s0)
    base = COL_HG // (nh * LANES)
    per = HG_HEADS // nh

    def col(k):
        return pl.BlockSpec((seq_len, nh * LANES), lambda b, h, k=k: (b, base + k * per + h))

    c = HG_CHUNK
    in_specs = [col(0), col(1), col(2), col(3), col(4),
                pl.BlockSpec((None, nh, 2, HG_KDIM), lambda b, h: (layer, h, 0, 0)),
                pl.BlockSpec((None, 1, HG_VDIM), lambda b, h: (layer, 0, 0)),
                pl.BlockSpec((2, (len(HG_MXU_LEVELS) + 1) * c, 2 * c), lambda b, h: (0, 0, 0)),
                pl.BlockSpec((2, c, c), lambda b, h: (0, 0, 0))]
    args = [p, p, p, p, p, lb, norm_g, m_all, lev_of]
    out_specs = [pl.BlockSpec((seq_len, nh * HG_VDIM), lambda b, h: (b, h))]
    out_shape = [jax.ShapeDtypeStruct((n, HG_VW), BF16)]
    if has_s0:
        in_specs.append(pl.BlockSpec((None, None, 2, nh, HG_KDIM, HG_VDIM), lambda b, h: (b, layer, 0, h, 0, 0)))
        args.append(s0)
    else:
        out_specs.append(pl.BlockSpec((None, 2, nh, HG_KDIM, HG_VDIM), lambda b, h: (b, 0, h, 0, 0)))
        out_shape.append(jax.ShapeDtypeStruct((nseq, 2, HG_HEADS, HG_KDIM, HG_VDIM), F32))
    return pl.pallas_call(
        kern,
        grid=(nseq, per),
        in_specs=in_specs,
        out_specs=out_specs,
        out_shape=out_shape,
        scratch_shapes=[pltpu.VMEM((seq_len, nh * HG_VDIM), F32), pltpu.VMEM((seq_len, nh * HG_VDIM), F32),
                        pltpu.VMEM((HG_CHAINS, HG_CHUNK, HG_KDIM), F32)],
        compiler_params=_cparams(("parallel", "parallel")),
        name="hgrn",
    )(*args)


ROUTER_ROWS = 32


def _router(logits_t):
    row = lax.broadcasted_iota(jnp.int32, logits_t.shape, 0)
    big = jnp.int32(ROUTER_ROWS)
    red = dict(axis=0, keepdims=True)
    is_g = row < N_GROUPS
    gl = jnp.where(is_g, logits_t, -jnp.inf)
    gmax = jnp.max(gl, **red)
    gsum = jnp.sum(jnp.exp(gl - gmax), **red)
    g_idx = jnp.min(jnp.where(is_g & (gl == gmax), row, big), **red)
    g_prob = 1.0 / gsum
    e_row = row - ROUTER_OFF
    sel = (e_row >= 0) & (e_row < N_EXPERTS) & ((e_row >> 2) == g_idx)
    el = jnp.where(sel, logits_t, -jnp.inf)
    emax = jnp.max(el, **red)
    eexp = jnp.exp(el - emax)
    ep = eexp / jnp.sum(eexp, **red)
    p1 = jnp.max(ep, **red)
    i1 = jnp.min(jnp.where(sel & (ep == p1), row, big), **red)
    rest = sel & (row != i1)
    ep2 = jnp.where(rest, ep, -1.0)
    p2 = jnp.max(ep2, **red)
    i2 = jnp.min(jnp.where(rest & (ep2 == p2), row, big), **red)
    den = p1 + p2
    w1 = g_prob * (p1 / den)
    w2 = g_prob * (p2 / den)
    return jnp.where(row == i1, w1, jnp.where(row == i2, w2, 0.0))


def _merge_kernel(x_ref, att_ref, z_ref, hg_ref, mg_ref, mod_ref, n2_ref,
                  wao_ref, wga_ref, wgb_ref, who_ref, wout_ref, wr_ref, br_ref,
                  xm_ref, h2_ref, comb_ref):
    y_att = _dot(att_ref[...], wao_ref[...])
    z = z_ref[...].astype(BF16)
    y_ssm = _dot(z, wga_ref[...]) * _sigmoid(_dot(z, wgb_ref[...]))
    y_hg = _dot(hg_ref[...], who_ref[...])
    gate = lambda k: _sigmoid(mg_ref[:, k * D_MODEL:(k + 1) * D_MODEL].astype(F32))
    merged = gate(0) * y_att + gate(1) * y_ssm + gate(2) * y_hg
    xm = x_ref[...] + mod_ref[2:3, :] * _dot(merged.astype(BF16), wout_ref[...])
    xm_ref[...] = xm
    ms = jnp.mean(xm * xm, axis=-1, keepdims=True)
    h2 = xm * lax.rsqrt(ms + EPS) * n2_ref[...] * (1.0 + mod_ref[4:5, :]) + mod_ref[3:4, :]
    h2_hi = h2.astype(BF16)
    h2_ref[...] = h2_hi
    h2_lo = (h2 - h2_hi.astype(F32)).astype(BF16)
    logits = _dot(jnp.concatenate([h2_hi, h2_lo, h2_hi], axis=1), wr_ref[...]) + br_ref[...]
    comb_t = _router(logits.T[0:ROUTER_ROWS])
    comb_t = jnp.concatenate([comb_t, jnp.zeros((LANES - ROUTER_ROWS, comb_t.shape[1]), F32)], axis=0)
    comb_ref[...] = comb_t.T


def _merge(x, att, z, hg, p, mod, n2g, w, layer, seq_len, cond_row0):
    n = x.shape[0]
    tm = 256
    rows_per_cond = seq_len if cond_row0 else n
    mod_idx = (lambda i: (layer, cond_row0 + (i * tm) // rows_per_cond, 0, 0))
    row = lambda width: pl.BlockSpec((tm, width), lambda i: (i, 0))
    wspec = lambda a, b: pl.BlockSpec((None, a, b), lambda i: (layer, 0, 0))
    return pl.pallas_call(
        _merge_kernel,
        grid=(n // tm,),
        in_specs=[
            row(D_MODEL), row(Q_W), row(SSM_WIDTH), row(HG_VW),
            row(MG_W),
            pl.BlockSpec((None, None, 6, D_MODEL), mod_idx),
            wspec(1, D_MODEL),
            wspec(Q_W, D_MODEL), wspec(SSM_WIDTH, D_MODEL), wspec(SSM_WIDTH, D_MODEL),
            wspec(HG_VW, D_MODEL), wspec(D_MODEL, D_MODEL), wspec(3 * D_MODEL, LANES), wspec(1, LANES),
        ],
        out_specs=[row(D_MODEL), row(D_MODEL), row(LANES)],
        out_shape=[
            jax.ShapeDtypeStruct((n, D_MODEL), F32),
            jax.ShapeDtypeStruct((n, D_MODEL), BF16),
            jax.ShapeDtypeStruct((n, LANES), F32),
        ],
        compiler_params=_cparams(("parallel",)),
        name="merge",
    )(x, att, z, hg, p, mod, n2g, w['w_attn_o'], w['w_glu_a'], w['w_glu_b'], w['w_hg_o'], w['w_out'],
      w['w_route'], w['b_route'])


def _moe_kernel(h_ref, comb_ref, xm_ref, mod_ref, wg_ref, wu_ref, wd_ref, o_ref, act_scr):
    comb = comb_ref[...]
    lane = lax.broadcasted_iota(jnp.int32, comb.shape, 1)
    h = h_ref[...]
    gw = EXPERTS_PER_GROUP * EXPERT_FF
    for grp in range(N_GROUPS):
        cols = slice(grp * gw, (grp + 1) * gw)
        hg = _dot(h, wg_ref[:, cols])
        hu = _dot(h, wu_ref[:, cols])
        for e in range(EXPERTS_PER_GROUP):
            cw = jnp.sum(jnp.where(lane == ROUTER_OFF + grp * EXPERTS_PER_GROUP + e, comb, 0.0),
                         axis=-1, keepdims=True)
            sl = slice(e * EXPERT_FF, (e + 1) * EXPERT_FF)
            act_scr[:, grp * gw + e * EXPERT_FF:grp * gw + (e + 1) * EXPERT_FF] = (
                _silu(hg[:, sl]) * hu[:, sl] * cw).astype(BF16)
    o_ref[...] = xm_ref[...] + mod_ref[5:6, :] * _dot(act_scr[...], wd_ref[...])


def _moe(h2, comb, xm, mod, w, layer, seq_len, cond_row0):
    n = h2.shape[0]
    tm = 512
    ff_all = N_EXPERTS * EXPERT_FF
    rows_per_cond = seq_len if cond_row0 else n
    mod_idx = (lambda i: (layer, cond_row0 + (i * tm) // rows_per_cond, 0, 0))
    once = pl.Buffered(1)
    return pl.pallas_call(
        _moe_kernel,
        grid=(n // tm,),
        in_specs=[
            pl.BlockSpec((tm, D_MODEL), lambda i: (i, 0)),
            pl.BlockSpec((tm, LANES), lambda i: (i, 0)),
            pl.BlockSpec((tm, D_MODEL), lambda i: (i, 0)),
            pl.BlockSpec((None, None, 6, D_MODEL), mod_idx),
            pl.BlockSpec((None, D_MODEL, ff_all), lambda i: (layer, 0, 0), pipeline_mode=once),
            pl.BlockSpec((None, D_MODEL, ff_all), lambda i: (layer, 0, 0), pipeline_mode=once),
            pl.BlockSpec((None, ff_all, D_MODEL), lambda i: (layer, 0, 0), pipeline_mode=once),
        ],
        out_specs=pl.BlockSpec((tm, D_MODEL), lambda i: (i, 0)),
        out_shape=jax.ShapeDtypeStruct((n, D_MODEL), F32),
        scratch_shapes=[pltpu.VMEM((tm, ff_all), BF16)],
        compiler_params=_cparams(("arbitrary",)),
        name="moe",
    )(h2, comb, xm, mod, w['w_e_gate'], w['w_e_up'], w['w_e_down'])


def _layer(x, layer, w, seq_len, cond_row0, cache):
    mg, p = _in_proj(x, w['mod'], w['norm1_g'], w['w_in_mg'], w['w_in_p'], layer, seq_len, cond_row0)
    nseq = x.shape[0] // seq_len
    if cache is None:
        att, k_new, v_new = _attn_ctx(w['attn_sink'], p, w['q_norm_g'], w['k_norm_g'], layer)
        z, ssm_fin = _ssm(p, w['ssm'], layer, nseq, seq_len)
        hg, hg_fin = _hgrn(p, w['hg_lb'], w['hg_norm_g'], w['hg_m'], w['hg_lev'], layer, seq_len)
        ctx = (k_new, v_new, ssm_fin, hg_fin)
    else:
        cache_k, cache_v, h0, s0 = cache
        att = _attn_lat(w['attn_sink'], p, cache_k, cache_v, w['q_norm_g'], w['k_norm_g'],
                        w['rope_cos'], w['rope_sin'], layer)
        (z,) = _ssm(p, w['ssm'], layer, nseq, seq_len, h0=h0)
        (hg,) = _hgrn(p, w['hg_lb'], w['hg_norm_g'], w['hg_m'], w['hg_lev'], layer, seq_len, s0=s0)
        ctx = None
    xm, h2, comb = _merge(x, att, z, hg, mg, w['mod'], w['norm2_g'], w, layer, seq_len, cond_row0)
    x = _moe(h2, comb, xm, w['mod'], w, layer, seq_len, cond_row0)
    return x, ctx


def kernel(x_prompt, x_sample, cache_k, cache_v, state_ssm_re, state_ssm_im, state_hgrn, c, c_ctx, w_mod, b_mod, norm1_g, norm2_g, w_in, q_norm_g, k_norm_g, attn_sink, w_attn_o, ssm_a_re, ssm_a_im, ssm_log_dt, ssm_b_re, ssm_b_im, ssm_c_re, ssm_c_im, ssm_d, w_glu_a, w_glu_b, hg_lb, hg_norm_g, w_hg_o, w_out, w_group, b_group, w_router, b_router, w_e_gate, w_e_up, w_e_down):
    w = {}
    w['mod'] = _modulation(c, c_ctx, w_mod, b_mod)
    w['norm1_g'] = norm1_g.reshape(DEPTH, 1, D_MODEL)
    w['norm2_g'] = norm2_g.reshape(DEPTH, 1, D_MODEL)
    w['w_in_mg'] = w_in[:, :, 3840:].astype(BF16)
    w['w_in_p'] = jnp.concatenate([w_in[:, :, 1280:3840], w_in[:, :, 0:768], w_in[:, :, 768:1280]],
                                  axis=-1).astype(BF16)
    w['q_norm_g'] = jnp.tile(q_norm_g, (1, LANES // HEAD_DIM)).reshape(DEPTH, 1, LANES)
    w['k_norm_g'] = jnp.tile(k_norm_g, (1, LANES // HEAD_DIM)).reshape(DEPTH, 1, LANES)
    w['attn_sink'] = attn_sink
    w['rope_cos'], w['rope_sin'] = _rope_tables()
    w['ssm'] = _ssm_prep(ssm_a_re, ssm_a_im, ssm_log_dt, ssm_b_re, ssm_b_im, ssm_c_re, ssm_c_im, ssm_d)
    w['hg_lb'] = _lower_bounds(hg_lb)
    w['hg_norm_g'] = hg_norm_g.reshape(DEPTH, 1, HG_VDIM)
    w['hg_m'], w['hg_lev'] = _hgrn_consts()
    for name, val in (('w_attn_o', w_attn_o), ('w_glu_a', w_glu_a), ('w_glu_b', w_glu_b), ('w_hg_o', w_hg_o),
                      ('w_out', w_out)):
        w[name] = val.astype(BF16)
    ff_all = N_EXPERTS * EXPERT_FF
    w['w_e_gate'] = w_e_gate.astype(BF16).transpose(0, 2, 1, 3).reshape(DEPTH, D_MODEL, ff_all)
    w['w_e_up'] = w_e_up.astype(BF16).transpose(0, 2, 1, 3).reshape(DEPTH, D_MODEL, ff_all)
    w['w_e_down'] = w_e_down.astype(BF16).reshape(DEPTH, ff_all, D_MODEL)
    pad = LANES - N_GROUPS - N_EXPERTS
    w_route = jnp.concatenate([w_group, w_router, jnp.zeros((DEPTH, D_MODEL, pad), F32)], axis=-1)
    w_route_hi = w_route.astype(BF16)
    w_route_lo = (w_route - w_route_hi.astype(F32)).astype(BF16)
    w['w_route'] = jnp.concatenate([w_route_hi, w_route_hi, w_route_lo], axis=1)
    w['b_route'] = jnp.concatenate([b_group, b_router, jnp.zeros((DEPTH, pad), F32)], axis=-1).reshape(DEPTH, 1, LANES)

    ck = cache_k.reshape(DEC_BATCH, DEPTH, PAST_LEN, KV_W)
    cv = cache_v.reshape(DEC_BATCH, DEPTH, PAST_LEN, KV_W)

    ng = LANES // SSM_GROUP
    h0 = jnp.stack([state_ssm_re, state_ssm_im]).transpose(2, 4, 0, 1, 3, 5).reshape(
        DEPTH, SSM_GROUPS // ng, ng, 2, DEC_BATCH, 2 * SSM_STATE)
    h0 = jnp.pad(h0, ((0, 0),) * 4 + ((0, SUBLANES - DEC_BATCH), (0, 0)))

    xp = x_prompt.reshape(BATCH * SEQ, D_MODEL)
    xs = x_sample.reshape(DEC_BATCH * DEC_SEQ, D_MODEL)
    new_k, new_v, new_re, new_im, new_hg = [], [], [], [], []
    for l in range(DEPTH):
        xp, ctx = _layer(xp, l, w, SEQ, 0, None)
        new_k.append(ctx[0].reshape(BATCH, SEQ, N_KV_HEADS, HEAD_DIM))
        new_v.append(ctx[1].reshape(BATCH, SEQ, N_KV_HEADS, HEAD_DIM))
        fin = ctx[2].reshape(BATCH // SUBLANES, SSM_GROUPS // ng, ng, 2, SUBLANES, 2, SSM_STATE)
        fin = fin.transpose(3, 0, 4, 5, 1, 2, 6).reshape(2, BATCH, 2, SSM_GROUPS, SSM_STATE)
        new_re.append(fin[0])
        new_im.append(fin[1])
        new_hg.append(ctx[3])
        xs, _ = _layer(xs, l, w, DEC_SEQ, 1, (ck, cv, h0, state_hgrn))
    return (xp.reshape(BATCH, SEQ, D_MODEL), xs.reshape(DEC_BATCH, DEC_SEQ, D_MODEL),
            jnp.stack(new_k, axis=1), jnp.stack(new_v, axis=1), jnp.stack(new_re, axis=1),
            jnp.stack(new_im, axis=1), jnp.stack(new_hg, axis=1))
```

```python
import functools
import math

import numpy as np
import jax
import jax.numpy as jnp
from jax import lax
from jax.experimental import pallas as pl
from jax.experimental.pallas import tpu as pltpu

F32 = jnp.float32
BF16 = jnp.bfloat16

D_MODEL = 1024
BATCH = 32
SEQ = 256
DEPTH = 4
DEC_BATCH = 4
DEC_SEQ = 1024
PAST_LEN = 512
GRID_W = 64
N_HEADS = 8
N_KV_HEADS = 2
HEAD_DIM = 64
KV_GROUP = N_HEADS // N_KV_HEADS
WINDOW = 128
LAT_QBLOCK = 128
ROPE_BASE = 10000.0
SSM_WIDTH = 512
SSM_GROUP = 16
SSM_GROUPS = SSM_WIDTH // SSM_GROUP
SSM_STATE = 64
HG_HEADS = 4
HG_KDIM = 128
HG_VDIM = 128
N_BRANCHES = 3
N_GROUPS = 4
EXPERTS_PER_GROUP = 4
N_EXPERTS = N_GROUPS * EXPERTS_PER_GROUP
EXPERT_FF = 256
EPS = 1e-6
Q_W = N_HEADS * HEAD_DIM
KV_W = N_KV_HEADS * HEAD_DIM
HG_KW = HG_HEADS * HG_KDIM
HG_VW = HG_HEADS * HG_VDIM

LANES = 128
SUBLANES = 8
VMEM_LIMIT = 56 * 1024 * 1024

MG_W = N_BRANCHES * D_MODEL
COL_HG = 0
COL_AQ = COL_HG + 3 * HG_KW + 2 * HG_VW
COL_AK = COL_AQ + Q_W
COL_SU = COL_AK + 2 * KV_W
P_WIDTH = COL_SU + SSM_WIDTH
INPROJ_STEPS = 3

SSM_T = 16
SSM_PREP_GROUPS = 4
HG_CHUNK = 128
HG_LEVELS = 7
HG_CHAINS = 8
HG_MXU_LEVELS = (1, 2)
ROUTER_OFF = N_GROUPS
MASK_NEG = -1e30


def _cparams(sem):
    return pltpu.CompilerParams(dimension_semantics=sem, vmem_limit_bytes=VMEM_LIMIT)


def _sigmoid(x):
    return 1.0 / (1.0 + jnp.exp(-x))


def _silu(x):
    return x * _sigmoid(x)


def _gelu_tanh(x):
    return 0.5 * x * (1.0 + jnp.tanh(math.sqrt(2.0 / math.pi) * (x + 0.044715 * (x * x * x))))


def _dot(a, b):
    return jnp.dot(a, b, preferred_element_type=F32)


def _dot_nt(a, b):
    return lax.dot_general(a, b, (((1,), (1,)), ((), ())), preferred_element_type=F32)


def _dot_tn(a, b):
    return lax.dot_general(a, b, (((0,), (0,)), ((), ())), preferred_element_type=F32)


def _shift_lanes(a, s):
    if s == 0:
        return a
    lo, hi = a[:, :LANES], a[:, LANES:]
    zero = jnp.zeros_like(lo)
    lane = lax.broadcasted_iota(jnp.int32, lo.shape, 1)
    if s > 0:
        if s >= LANES:
            t = s - LANES
            out_hi = lo if t == 0 else jnp.where(lane >= t, pltpu.roll(lo, t, 1), 0.0)
            return jnp.concatenate([zero, out_hi], axis=1)
        r_lo, r_hi = pltpu.roll(lo, s, 1), pltpu.roll(hi, s, 1)
        return jnp.concatenate([jnp.where(lane >= s, r_lo, 0.0), jnp.where(lane >= s, r_hi, r_lo)], axis=1)
    s = -s
    if s >= LANES:
        t = s - LANES
        out_lo = hi if t == 0 else jnp.where(lane < LANES - t, pltpu.roll(hi, LANES - t, 1), 0.0)
        return jnp.concatenate([out_lo, zero], axis=1)
    r_lo, r_hi = pltpu.roll(lo, LANES - s, 1), pltpu.roll(hi, LANES - s, 1)
    keep = lane < LANES - s
    return jnp.concatenate([jnp.where(keep, r_lo, r_hi), jnp.where(keep, r_hi, 0.0)], axis=1)


def _ssm_prep_kernel(*refs):
    for g in range(SSM_PREP_GROUPS):
        _ssm_prep_group(*[r.at[g] for r in refs])


def _ssm_prep_group(are_ref, aim_ref, ldt_ref, btre_ref, btim_ref, cre_ref, cim_ref,
                    w_ref, bst_ref, cst_ref, lamt_ref):
    t_blk = SSM_T
    ns = SSM_STATE
    hp = lax.Precision.HIGHEST
    jf = lax.broadcasted_iota(jnp.int32, (2 * t_blk, ns), 0).astype(F32)
    bt_re = btre_ref[...]
    bt_im = btim_ref[...]
    c_re = cre_ref[...]
    c_im = cim_ref[...]
    p_re, p_im, bb_re, bb_im, kt = [], [], [], [], []
    for d in range(2):
        a_re = are_ref[d:d + 1, :]
        a_im = aim_ref[d:d + 1, :]
        dt = jnp.exp(ldt_ref[d:d + 1, :])
        mag = jnp.exp(jf * (a_re * dt))
        ang = jf * (a_im * dt)
        p_re.append(mag * jnp.cos(ang))
        p_im.append(mag * jnp.sin(ang))
        nr = p_re[d][1:2] - 1.0
        ni = p_im[d][1:2]
        den = a_re * a_re + a_im * a_im
        f_re = (nr * a_re + ni * a_im) / den
        f_im = (ni * a_re - nr * a_im) / den
        bb_re.append(f_re * bt_re - f_im * bt_im)
        bb_im.append(f_re * bt_im + f_im * bt_re)
        lags = range(t_blk) if d == 0 else range(t_blk - 1, -1, -1)
        a_parts_re = [c_re * p_re[d][j:j + 1] - c_im * p_im[d][j:j + 1] for j in lags]
        a_parts_im = [c_re * p_im[d][j:j + 1] + c_im * p_re[d][j:j + 1] for j in lags]
        nt = (((1,), (1,)), ((), ()))
        kt.append(lax.dot_general(bb_re[d], jnp.concatenate(a_parts_re, axis=0), nt,
                                  precision=hp, preferred_element_type=F32)
                  - lax.dot_general(bb_im[d], jnp.concatenate(a_parts_im, axis=0), nt,
                                    precision=hp, preferred_element_type=F32))
        lamt_ref[0:1, d * ns:(d + 1) * ns] = p_re[d][t_blk:t_blk + 1]
        lamt_ref[1:2, d * ns:(d + 1) * ns] = p_im[d][t_blk:t_blk + 1]
    for t in range(t_blk):
        rows = slice(t * SSM_GROUP, (t + 1) * SSM_GROUP)
        w_ref[rows, :] = (_shift_lanes(kt[0], t * SSM_GROUP)
                          + _shift_lanes(kt[1], -(t_blk - 1 - t) * SSM_GROUP)).astype(BF16)
        ef, eb = t_blk - 1 - t, t
        bst = [bb_re[0] * p_re[0][ef:ef + 1] - bb_im[0] * p_im[0][ef:ef + 1],
               bb_re[1] * p_re[1][eb:eb + 1] - bb_im[1] * p_im[1][eb:eb + 1],
               bb_re[0] * p_im[0][ef:ef + 1] + bb_im[0] * p_re[0][ef:ef + 1],
               bb_re[1] * p_im[1][eb:eb + 1] + bb_im[1] * p_re[1][eb:eb + 1]]
        bst_ref[rows, :] = jnp.concatenate(bst, axis=1).astype(BF16)
        ef, eb = t + 1, t_blk - t
        cst = [c_re * p_re[0][ef:ef + 1] - c_im * p_im[0][ef:ef + 1],
               c_re * p_re[1][eb:eb + 1] - c_im * p_im[1][eb:eb + 1],
               -(c_re * p_im[0][ef:ef + 1] + c_im * p_re[0][ef:ef + 1]),
               -(c_re * p_im[1][eb:eb + 1] + c_im * p_re[1][eb:eb + 1])]
        cst_ref[rows, :] = jnp.concatenate(cst, axis=1).astype(BF16)


def _ssm_prep(ssm_a_re, ssm_a_im, ssm_log_dt, ssm_b_re, ssm_b_im, ssm_c_re, ssm_c_im, ssm_d):
    t_blk = SSM_T
    bw = t_blk * SSM_GROUP
    lgd = lambda a: a.transpose(0, 2, 1, 3)
    ldt = jnp.broadcast_to(ssm_log_dt[..., None], ssm_a_re.shape)
    per = SSM_PREP_GROUPS
    vec = pl.BlockSpec((None, per, 2, SSM_STATE), lambda l, g: (l, g, 0, 0))
    mat = pl.BlockSpec((None, per, SSM_GROUP, SSM_STATE), lambda l, g: (l, g, 0, 0))
    op = lambda width: pl.BlockSpec((None, per, bw, width), lambda l, g: (l, g, 0, 0))
    w, bst, cst, lamt = pl.pallas_call(
        _ssm_prep_kernel,
        grid=(DEPTH, SSM_GROUPS // per),
        in_specs=[vec, vec, vec, mat, mat, mat, mat],
        out_specs=[op(bw), op(4 * SSM_STATE), op(4 * SSM_STATE),
                   pl.BlockSpec((None, per, 2, 2 * SSM_STATE), lambda l, g: (l, g, 0, 0))],
        out_shape=[
            jax.ShapeDtypeStruct((DEPTH, SSM_GROUPS, bw, bw), BF16),
            jax.ShapeDtypeStruct((DEPTH, SSM_GROUPS, bw, 4 * SSM_STATE), BF16),
            jax.ShapeDtypeStruct((DEPTH, SSM_GROUPS, bw, 4 * SSM_STATE), BF16),
            jax.ShapeDtypeStruct((DEPTH, SSM_GROUPS, 2, 2 * SSM_STATE), F32),
        ],
        compiler_params=_cparams(("parallel", "parallel")),
        name="ssm_prep",
    )(lgd(ssm_a_re), lgd(ssm_a_im), lgd(ldt), jnp.swapaxes(ssm_b_re, -1, -2), jnp.swapaxes(ssm_b_im, -1, -2),
      ssm_c_re, ssm_c_im)
    d_row = jnp.tile(ssm_d, (1, 1, t_blk)).reshape(DEPTH, SSM_GROUPS, 1, bw)
    return dict(w=w, bst=bst, cst=cst, lamt=lamt, d=d_row)


def _lb_kernel(x_ref, o_ref):
    x = x_ref[...]
    m = jnp.max(x, axis=0, keepdims=True)
    e = jnp.exp(x - m)
    s = e / jnp.sum(e, axis=0, keepdims=True)
    run = jnp.zeros_like(s[0:1])
    o_ref[0:1, :] = run
    for l in range(1, DEPTH):
        run = run + s[l:l + 1]
        o_ref[l:l + 1, :] = run


def _lower_bounds(hg_lb):
    w = 2 * HG_KW
    out = pl.pallas_call(
        _lb_kernel,
        out_shape=jax.ShapeDtypeStruct((DEPTH, w), F32),
        name="hgrn_lower_bounds",
    )(hg_lb.reshape(DEPTH, w))
    return out.reshape(DEPTH, 2, HG_HEADS, HG_KDIM).transpose(0, 2, 1, 3)


def _mod_kernel(c_ref, w_ref, b_ref, o_ref):
    c = c_ref[...]
    a = _silu(c).astype(BF16)
    o_ref[...] = _dot(a, w_ref[...].astype(BF16)) + b_ref[...]


def _modulation(c, c_ctx, w_mod, b_mod):
    rows = SUBLANES
    cond = jnp.concatenate([c_ctx[None, :], c, jnp.zeros((rows - 1 - DEC_BATCH, D_MODEL), F32)], axis=0)
    tn = D_MODEL
    out = pl.pallas_call(
        _mod_kernel,
        grid=(DEPTH, 6),
        in_specs=[
            pl.BlockSpec((rows, D_MODEL), lambda l, j: (0, 0)),
            pl.BlockSpec((None, D_MODEL, tn), lambda l, j: (l, 0, j)),
            pl.BlockSpec((None, 1, tn), lambda l, j: (l, 0, j)),
        ],
        out_specs=pl.BlockSpec((None, rows, tn), lambda l, j: (l, 0, j)),
        out_shape=jax.ShapeDtypeStruct((DEPTH, rows, 6 * D_MODEL), F32),
        compiler_params=_cparams(("parallel", "parallel")),
        name="modulation",
    )(cond, w_mod, b_mod.reshape(DEPTH, 1, 6 * D_MODEL))
    return out.reshape(DEPTH, rows, 6, D_MODEL)


def _inproj_kernel(x_ref, mod_ref, g_ref, wmg_ref, wp_ref, mg_ref, p_ref):
    x = x_ref[...]
    ms = jnp.mean(x * x, axis=-1, keepdims=True)
    y = x * lax.rsqrt(ms + EPS) * g_ref[...]
    h = (y * (1.0 + mod_ref[1:2, :]) + mod_ref[0:1, :]).astype(BF16)
    tn_mg = MG_W // INPROJ_STEPS
    tn_p = P_WIDTH // INPROJ_STEPS
    for c in range(INPROJ_STEPS):
        mg_ref[:, c * tn_mg:(c + 1) * tn_mg] = _dot(h, wmg_ref[:, c * tn_mg:(c + 1) * tn_mg]).astype(BF16)
    for c in range(INPROJ_STEPS):
        p_ref[:, c * tn_p:(c + 1) * tn_p] = _dot(h, wp_ref[:, c * tn_p:(c + 1) * tn_p])


def _in_proj(x, mod, norm_g, w_mg, w_p, layer, seq_len, cond_row0):
    n = x.shape[0]
    tm = 512
    rows_per_cond = seq_len if cond_row0 else n
    mod_idx = (lambda i: (layer, cond_row0 + (i * tm) // rows_per_cond, 0, 0))
    once = pl.Buffered(1)
    return pl.pallas_call(
        _inproj_kernel,
        grid=(n // tm,),
        in_specs=[
            pl.BlockSpec((tm, D_MODEL), lambda i: (i, 0)),
            pl.BlockSpec((None, None, 6, D_MODEL), mod_idx),
            pl.BlockSpec((None, 1, D_MODEL), lambda i: (layer, 0, 0)),
            pl.BlockSpec((None, D_MODEL, MG_W), lambda i: (layer, 0, 0), pipeline_mode=once),
            pl.BlockSpec((None, D_MODEL, P_WIDTH), lambda i: (layer, 0, 0), pipeline_mode=once),
        ],
        out_specs=[pl.BlockSpec((tm, MG_W), lambda i: (i, 0)), pl.BlockSpec((tm, P_WIDTH), lambda i: (i, 0))],
        out_shape=[jax.ShapeDtypeStruct((n, MG_W), BF16), jax.ShapeDtypeStruct((n, P_WIDTH), F32)],
        compiler_params=_cparams(("parallel",)),
        name="in_proj",
    )(x, mod, norm_g, w_mg, w_p)


def _pair_norm(x, g):
    xx = x * x
    s_a = jnp.sum(xx[:, :HEAD_DIM], axis=-1, keepdims=True)
    s_b = jnp.sum(xx[:, HEAD_DIM:], axis=-1, keepdims=True)
    lane = lax.broadcasted_iota(jnp.int32, x.shape, 1)
    ms = jnp.where(lane < HEAD_DIM, s_a, s_b) * (1.0 / HEAD_DIM)
    return x * lax.rsqrt(ms + EPS) * g


def _rope(x, cos, sin_signed):
    lane = lax.broadcasted_iota(jnp.int32, x.shape, 1)
    first = (lane & 31) < 16
    partner = jnp.where(first, pltpu.roll(x, LANES - 16, 1), pltpu.roll(x, 16, 1))
    return x * cos + partner * sin_signed


def _sink_column(sink_ref, layer, kh, t):
    row = lax.broadcasted_iota(jnp.int32, (KV_GROUP * t, 1), 0)
    col = jnp.full((KV_GROUP * t, 1), sink_ref[layer, kh * KV_GROUP], F32)
    for g in range(1, KV_GROUP):
        col = jnp.where(row >= g * t, sink_ref[layer, kh * KV_GROUP + g], col)
    return col


def _sink_softmax_av(heads):
    ms = []
    for scores, _, sink_col in heads:
        m = sink_col
        for s in scores:
            m = jnp.maximum(m, jnp.max(s, axis=-1, keepdims=True))
        ms.append(m)
    ps = [[jnp.exp(s - m) for s in scores] for (scores, _, _), m in zip(heads, ms)]
    outs = []
    for (scores, values, sink_col), m, p_list in zip(heads, ms, ps):
        den = jnp.exp(sink_col - m)
        o = None
        for p, v in zip(p_list, values):
            den = den + jnp.sum(p, axis=-1, keepdims=True)
            pv = _dot(p.astype(BF16), v.astype(BF16))
            o = pv if o is None else o + pv
        outs.append(o / den)
    return outs


def _attn_ctx_kernel(sink_ref, q_ref, kv_ref, gq_ref, gk_ref, att_ref, k_ref, v_ref, *, layer):
    t = q_ref.shape[0]
    kv = kv_ref[...]
    k = _pair_norm(kv[:, :KV_W], gk_ref[...])
    v = kv[:, KV_W:]
    k_ref[...] = k
    v_ref[...] = v
    q = q_ref[...]
    scale = 1.0 / math.sqrt(HEAD_DIM)
    work = []
    for kh in range(N_KV_HEADS):
        heads = []
        for j in range(KV_GROUP // 2):
            c0 = (kh * (KV_GROUP // 2) + j) * LANES
            qn = _pair_norm(q[:, c0:c0 + LANES], gq_ref[...])
            heads += [qn[:, :HEAD_DIM], qn[:, HEAD_DIM:]]
        qs = jnp.concatenate(heads, axis=0).astype(BF16)
        sl = slice(kh * HEAD_DIM, (kh + 1) * HEAD_DIM)
        s = _dot_nt(qs, k[:, sl].astype(BF16)) * scale
        work.append(([s], [v[:, sl]], _sink_column(sink_ref, layer, kh, t)))
    outs = [o[g * t:(g + 1) * t] for o in _sink_softmax_av(work) for g in range(KV_GROUP)]
    att_ref[...] = jnp.concatenate(outs, axis=-1).astype(BF16)


def _attn_ctx(sink, p, gq, gk, layer):
    n = p.shape[0]
    t = SEQ
    kern = functools.partial(_attn_ctx_kernel, layer=layer)
    return pl.pallas_call(
        kern,
        grid=(n // t,),
        in_specs=[
            pl.BlockSpec(memory_space=pltpu.SMEM),
            pl.BlockSpec((t, Q_W), lambda b: (b, COL_AQ // Q_W)),
            pl.BlockSpec((t, 2 * KV_W), lambda b: (b, COL_AK // (2 * KV_W))),
            pl.BlockSpec((None, 1, LANES), lambda b: (layer, 0, 0)),
            pl.BlockSpec((None, 1, LANES), lambda b: (layer, 0, 0)),
        ],
        out_specs=[
            pl.BlockSpec((t, Q_W), lambda b: (b, 0)),
            pl.BlockSpec((t, KV_W), lambda b: (b, 0)),
            pl.BlockSpec((t, KV_W), lambda b: (b, 0)),
        ],
        out_shape=[
            jax.ShapeDtypeStruct((n, Q_W), BF16),
            jax.ShapeDtypeStruct((n, KV_W), F32),
            jax.ShapeDtypeStruct((n, KV_W), F32),
        ],
        compiler_params=_cparams(("parallel",)),
        name="attn_ctx",
    )(sink, p, p, gq, gk)


def _attn_lat_kernel(sink_ref, q_ref, kv_ref, kc_ref, vc_ref, gq_ref, gk_ref, cos_ref, sin_ref, att_ref, *, layer):
    t = LAT_QBLOCK
    span = t + 2 * WINDOW
    start = pl.program_id(1) * t
    ks = pl.multiple_of(jnp.clip(start - WINDOW, 0, DEC_SEQ - span), WINDOW)
    q0 = pl.multiple_of(start, t)
    kvw = kv_ref[pl.ds(ks, span), :]
    k = _rope(_pair_norm(kvw[:, :KV_W], gk_ref[...]), cos_ref[pl.ds(ks, span), :], sin_ref[pl.ds(ks, span), :])
    v = kvw[:, KV_W:]
    cos_q = cos_ref[pl.ds(q0, t), :]
    sin_q = sin_ref[pl.ds(q0, t), :]
    qpos = start + (lax.broadcasted_iota(jnp.int32, (KV_GROUP * t, span), 0) & (t - 1))
    kpos = ks + lax.broadcasted_iota(jnp.int32, (KV_GROUP * t, span), 1)
    band = jnp.abs(qpos - kpos) <= WINDOW
    q = q_ref[...]
    kc = kc_ref[...]
    vc = vc_ref[...]
    scale = 1.0 / math.sqrt(HEAD_DIM)
    work = []
    for kh in range(N_KV_HEADS):
        heads = []
        for j in range(KV_GROUP // 2):
            c0 = (kh * (KV_GROUP // 2) + j) * LANES
            qn = _rope(_pair_norm(q[:, c0:c0 + LANES], gq_ref[...]), cos_q, sin_q)
            heads += [qn[:, :HEAD_DIM], qn[:, HEAD_DIM:]]
        qs = jnp.concatenate(heads, axis=0).astype(BF16)
        sl = slice(kh * HEAD_DIM, (kh + 1) * HEAD_DIM)
        s_w = jnp.where(band, _dot_nt(qs, k[:, sl].astype(BF16)) * scale, MASK_NEG)
        s_c = _dot_nt(qs, kc[:, sl].astype(BF16)) * scale
        work.append(([s_w, s_c], [v[:, sl], vc[:, sl]], _sink_column(sink_ref, layer, kh, t)))
    outs = [o[g * t:(g + 1) * t] for o in _sink_softmax_av(work) for g in range(KV_GROUP)]
    att_ref[...] = jnp.concatenate(outs, axis=-1).astype(BF16)


def _attn_lat(sink, p, cache_k, cache_v, gq, gk, cos_t, sin_t, layer):
    n = p.shape[0]
    t = LAT_QBLOCK
    nb = DEC_SEQ // t
    kern = functools.partial(_attn_lat_kernel, layer=layer)
    return pl.pallas_call(
        kern,
        grid=(DEC_BATCH, nb),
        in_specs=[
            pl.BlockSpec(memory_space=pltpu.SMEM),
            pl.BlockSpec((t, Q_W), lambda b, j: (b * nb + j, COL_AQ // Q_W)),
            pl.BlockSpec((DEC_SEQ, 2 * KV_W), lambda b, j: (b, COL_AK // (2 * KV_W))),
            pl.BlockSpec((None, None, PAST_LEN, KV_W), lambda b, j: (b, layer, 0, 0)),
            pl.BlockSpec((None, None, PAST_LEN, KV_W), lambda b, j: (b, layer, 0, 0)),
            pl.BlockSpec((None, 1, LANES), lambda b, j: (layer, 0, 0)),
            pl.BlockSpec((None, 1, LANES), lambda b, j: (layer, 0, 0)),
            pl.BlockSpec((DEC_SEQ, LANES), lambda b, j: (0, 0)),
            pl.BlockSpec((DEC_SEQ, LANES), lambda b, j: (0, 0)),
        ],
        out_specs=pl.BlockSpec((t, Q_W), lambda b, j: (b * nb + j, 0)),
        out_shape=jax.ShapeDtypeStruct((n, Q_W), BF16),
        compiler_params=_cparams(("parallel", "parallel")),
        name="attn_lat",
    )(sink, p, p, cache_k, cache_v, gq, gk, cos_t, sin_t)


def _rope_tables():
    pos = jnp.arange(DEC_SEQ)
    rows = (pos // GRID_W).astype(F32)
    cols = (pos % GRID_W).astype(F32)
    half = HEAD_DIM // 2
    freqs = ROPE_BASE ** (-jnp.arange(0, half, 2, dtype=F32) / half)
    ang_r = rows[:, None] * freqs[None, :]
    ang_c = cols[:, None] * freqs[None, :]
    ang = jnp.concatenate([ang_r, ang_r, ang_c, ang_c], axis=-1)
    sign = jnp.tile(jnp.concatenate([-jnp.ones((16,), F32), jnp.ones((16,), F32)]), 2)
    cos_t = jnp.tile(jnp.cos(ang), (1, LANES // HEAD_DIM))
    sin_t = jnp.tile(jnp.sin(ang) * sign[None, :], (1, LANES // HEAD_DIM))
    return cos_t, sin_t


def _ssm_kernel(*refs, nseq, seq_len, has_h0):
    if has_h0:
        x_ref, w_ref, bst_ref, cst_ref, lamt_ref, d_ref, h0_ref, z_ref, loc_scr, ent_scr, cols_scr = refs
    else:
        x_ref, w_ref, bst_ref, cst_ref, lamt_ref, d_ref, z_ref, fin_ref, loc_scr, ent_scr, cols_scr = refs
    ns = SSM_STATE
    t_blk = SSM_T
    nblk = seq_len // t_blk
    ng = LANES // SSM_GROUP
    rows = nblk * SUBLANES
    per_half = LANES // SSM_GROUP
    if nseq < SUBLANES:
        cols_scr[...] = jnp.zeros_like(cols_scr)
    for t in range(t_blk):
        for b in range(nblk):
            cols_scr[t, b * SUBLANES:b * SUBLANES + nseq, :] = x_ref[pl.ds(b * t_blk + t, nseq, stride=seq_len), :]
    cols = [cols_scr[t] for t in range(t_blk)]
    chunk = lax.broadcasted_iota(jnp.int32, (rows, LANES), 1) // SSM_GROUP
    is_fwd = lax.broadcasted_iota(jnp.int32, (SUBLANES, 2 * ns), 1) < ns

    def roll(a, chunks):
        shift = (chunks * SSM_GROUP) % LANES
        return a if shift == 0 else pltpu.roll(a, shift, 1)

    xg = []
    for g in range(ng):
        halves = []
        for h in range(t_blk // per_half):
            acc = None
            for j in range(per_half):
                r = roll(cols[h * per_half + j], j - g)
                acc = r if acc is None else jnp.where(chunk == j, r, acc)
            halves.append(acc)
        xg.append(jnp.concatenate(halves, axis=1))
        loc_scr[g] = _dot(xg[g].astype(BF16), bst_ref[g])
    l_re = [jnp.broadcast_to(lamt_ref[g, 0:1, :], (SUBLANES, 2 * ns)) for g in range(ng)]
    l_im = [jnp.broadcast_to(lamt_ref[g, 1:2, :], (SUBLANES, 2 * ns)) for g in range(ng)]
    if has_h0:
        s_re = [h0_ref[g, 0] for g in range(ng)]
        s_im = [h0_ref[g, 1] for g in range(ng)]
    else:
        s_re = [jnp.zeros((SUBLANES, 2 * ns), F32) for _ in range(ng)]
        s_im = [jnp.zeros((SUBLANES, 2 * ns), F32) for _ in range(ng)]
    for j in range(nblk):
        rf = slice(j * SUBLANES, (j + 1) * SUBLANES)
        rb = slice((nblk - 1 - j) * SUBLANES, (nblk - j) * SUBLANES)
        for g in range(ng):
            ent_scr[g, rf, 0:ns] = s_re[g][:, 0:ns]
            ent_scr[g, rb, ns:2 * ns] = s_re[g][:, ns:2 * ns]
            ent_scr[g, rf, 2 * ns:3 * ns] = s_im[g][:, 0:ns]
            ent_scr[g, rb, 3 * ns:4 * ns] = s_im[g][:, ns:2 * ns]
            loc_re = jnp.where(is_fwd, loc_scr[g, rf, 0:2 * ns], loc_scr[g, rb, 0:2 * ns])
            loc_im = jnp.where(is_fwd, loc_scr[g, rf, 2 * ns:4 * ns], loc_scr[g, rb, 2 * ns:4 * ns])
            n_re = l_re[g] * s_re[g] - l_im[g] * s_im[g] + loc_re
            n_im = l_re[g] * s_im[g] + l_im[g] * s_re[g] + loc_im
            s_re[g], s_im[g] = n_re, n_im
    if not has_h0:
        for g in range(ng):
            fin_ref[g, 0] = s_re[g]
            fin_ref[g, 1] = s_im[g]
    out_cols = [None] * t_blk
    for g in range(ng):
        y = (_dot(xg[g].astype(BF16), w_ref[g]) + _dot_nt(ent_scr[g].astype(BF16), cst_ref[g])
             + d_ref[g] * xg[g])
        zg = _gelu_tanh(y)
        for h in range(t_blk // per_half):
            zh = zg[:, h * LANES:(h + 1) * LANES]
            for j in range(per_half):
                t = h * per_half + j
                r = roll(zh, g - j)
                out_cols[t] = r if out_cols[t] is None else jnp.where(chunk == g, r, out_cols[t])
    for t in range(t_blk):
        for b in range(nblk):
            z_ref[pl.ds(b * t_blk + t, nseq, stride=seq_len), :] = out_cols[t][b * SUBLANES:b * SUBLANES + nseq]


def _ssm(p, sw, layer, nseq, seq_len, h0=None):
    n = p.shape[0]
    t_blk = SSM_T
    nblk = seq_len // t_blk
    bw = t_blk * SSM_GROUP
    nsg = SSM_WIDTH // LANES
    ng = LANES // SSM_GROUP
    per_step = min(nseq, SUBLANES)
    has_h0 = h0 is not None
    kern = functools.partial(_ssm_kernel, nseq=per_step, seq_len=seq_len, has_h0=has_h0)
    wspec = lambda a, b: pl.BlockSpec((None, None, ng, a, b), lambda s, g: (layer, g, 0, 0, 0))
    grp = lambda a: a.reshape((DEPTH, nsg, ng) + a.shape[2:])
    tok = per_step * seq_len
    in_specs = [pl.BlockSpec((tok, LANES), lambda s, g: (s, COL_SU // LANES + g)),
                wspec(bw, bw), wspec(bw, 4 * SSM_STATE), wspec(bw, 4 * SSM_STATE),
                wspec(2, 2 * SSM_STATE), wspec(1, bw)]
    args = [p, grp(sw['w']), grp(sw['bst']), grp(sw['cst']), grp(sw['lamt']), grp(sw['d'])]
    out_specs = [pl.BlockSpec((tok, LANES), lambda s, g: (s, g))]
    out_shape = [jax.ShapeDtypeStruct((n, SSM_WIDTH), F32)]
    st_block = (ng, 2, SUBLANES, 2 * SSM_STATE)
    if has_h0:
        in_specs.append(pl.BlockSpec((None, None) + st_block, lambda s, g: (layer, g, 0, 0, 0, 0)))
        args.append(h0)
    else:
        out_specs.append(pl.BlockSpec((None, None) + st_block, lambda s, g: (s, g, 0, 0, 0, 0)))
        out_shape.append(jax.ShapeDtypeStruct((nseq // per_step, nsg) + st_block, F32))
    rows = nblk * SUBLANES
    return pl.pallas_call(
        kern,
        grid=(nseq // per_step, nsg),
        in_specs=in_specs,
        out_specs=out_specs,
        out_shape=out_shape,
        scratch_shapes=[pltpu.VMEM((ng, rows, 4 * SSM_STATE), F32), pltpu.VMEM((ng, rows, 4 * SSM_STATE), F32),
                        pltpu.VMEM((t_blk, rows, LANES), F32)],
        compiler_params=_cparams(("parallel", "parallel")),
        name="ssm",
    )(*args)


def _hgrn_consts():
    c = HG_CHUNK
    i = np.arange(c)[:, None]
    r = np.arange(c)[None, :]
    m_all = np.zeros((2, HG_LEVELS, c, c), np.float32)
    lev_of = np.full((2, c, c), -1, np.int32)
    for lev in range(HG_LEVELS):
        n = 2 << lev
        half = n // 2
        mid = (i // n) * n + half
        upper = (i % n) >= half
        m_all[0, lev] = np.where(upper, (r >= mid) & (r <= i), (r > i) & (r < mid))
        m_all[1, lev] = np.where(upper, (r >= mid) & (r < i), (r >= i) & (r < mid))
        same = (i // n) == (r // n)
        r_upper = (r % n) >= half
        lev_of[0][same & upper & ~r_upper] = lev
        lev_of[1][same & ~upper & r_upper] = lev
    cum = np.stack([r <= i, r >= i]).astype(np.float32)
    m_small = np.concatenate([cum] + [m_all[:, lev] for lev in HG_MXU_LEVELS], axis=1)
    return (jnp.asarray(np.concatenate([m_small, m_small], axis=-1), BF16), jnp.asarray(lev_of))


def _hgrn_chunks(chains, states, m_ref, lev_ref):
    c = HG_CHUNK
    nb = c // SUBLANES
    w = []
    for d, key, r0, hq_ref, f_ref, hi_ref, lb, cum_ref in chains:
        q = _silu(hq_ref[pl.ds(r0, c), :])
        fg = lb + (1.0 - lb) * _sigmoid(f_ref[pl.ds(r0, c), :])
        k = 1.0 - fg
        log2f = jnp.log(fg) * (1.0 / math.log(2.0))
        v = hi_ref[pl.ds(r0, c), :]
        hi = log2f.astype(BF16)
        lo = (log2f - hi.astype(F32)).astype(BF16)
        sums = _dot(m_ref[d], jnp.concatenate([hi, lo], axis=0))
        w.append(dict(d=d, key=key, q=q, fg=fg, k=k, v=v, vb=v.astype(BF16), sums=sums, cum=sums[0:c],
                      cum_ref=cum_ref, att=[None] * nb))
    for x in w:
        x['cum_ref'][...] = x['cum']

    def select_into(x, rows, lev, a):
        att = x['att']
        for j in range(rows.stop // SUBLANES - rows.start // SUBLANES):
            i = rows.start // SUBLANES + j
            blk = a[j * SUBLANES:(j + 1) * SUBLANES]
            mask = lev_ref[x['d'], i * SUBLANES:(i + 1) * SUBLANES, :] == lev
            att[i] = jnp.where(mask, blk, 0.0 if att[i] is None else att[i])

    full = slice(0, c)
    prods = [_dot_nt((x['q'] * x['fg']).astype(BF16), x['k'].astype(BF16)) for x in w]
    for x, a in zip(w, prods):
        select_into(x, full, 0, a)
    for n_lev, lev in enumerate(HG_MXU_LEVELS):
        prods = []
        for x in w:
            g = jnp.exp2(x['sums'][(n_lev + 1) * c:(n_lev + 2) * c])
            prods.append(_dot_nt((x['q'] * g).astype(BF16), (x['k'] * g).astype(BF16)))
        for x, a in zip(w, prods):
            select_into(x, full, lev, a)
    for lev in range(HG_MXU_LEVELS[-1] + 1, HG_LEVELS):
        half = 1 << lev
        n = 2 * half
        prods = []
        for x in w:
            d, q, k, cum = x['d'], x['q'], x['k'], x['cum']
            q_parts, k_parts, q_rows = [], [], []
            for j in range(c // n):
                lower = slice(j * n, j * n + half)
                upper = slice(j * n + half, (j + 1) * n)
                qr, kr = (upper, lower) if d == 0 else (lower, upper)
                ref_row = j * n + half - 1 if d == 0 else j * n + half
                ref = jnp.broadcast_to(x['cum_ref'][ref_row:ref_row + 1, :], (half, HG_KDIM))
                q_parts.append(q[qr] * jnp.exp2(cum[qr] - ref))
                k_scaled = k[kr] * jnp.exp2(ref - cum[kr])
                k_parts += [k_scaled, k[qr]] if d == 0 else [k[qr], k_scaled]
                q_rows.append(qr)
            prods.append((q_rows, _dot_nt(jnp.concatenate(q_parts, axis=0).astype(BF16),
                                          jnp.concatenate(k_parts, axis=0).astype(BF16))))
        for x, (q_rows, a) in zip(w, prods):
            for j, qr in enumerate(q_rows):
                select_into(x, qr, lev, a[j * half:(j + 1) * half])
    for x in w:
        d, q, k, cum = x['d'], x['q'], x['k'], x['cum']
        last = c - 1 if d == 0 else 0
        tot_row = cum[last:last + 1, :]
        x['lhs'] = jnp.concatenate([jnp.concatenate(x['att'], axis=0).astype(BF16),
                                    (q * jnp.exp2(cum)).astype(BF16)], axis=1)
        x['kg'] = (k * jnp.exp2(tot_row - cum)).astype(BF16)
        x['dv'] = jnp.sum(q * k, axis=-1, keepdims=True) * x['v']
        x['tot_col'] = jnp.broadcast_to(jnp.exp2(tot_row), (c, HG_KDIM)).T
    states = list(states)
    outs = []
    for x in w:
        st = states[x['key']]
        outs.append(x['dv'] + _dot(x['lhs'], jnp.concatenate([x['vb'], st.astype(BF16)], axis=0)))
        states[x['key']] = st * x['tot_col'] + _dot_tn(x['kg'], x['vb'])
    return outs, states


def _hgrn_kernel(*refs, seq_len, nh, unroll, has_s0):
    if has_s0:
        (hq_ref, ff_ref, fb_ref, hi_ref, hgt_ref, lb_ref, ng_ref, m_ref, lev_ref, s0_ref,
         o_ref, of_scr, ob_scr, cum_scr) = refs
    else:
        (hq_ref, ff_ref, fb_ref, hi_ref, hgt_ref, lb_ref, ng_ref, m_ref, lev_ref,
         o_ref, sfin_ref, of_scr, ob_scr, cum_scr) = refs
    c = HG_CHUNK
    nch = seq_len // c
    head = lambda ref, h: ref.at[:, h * HG_KDIM:(h + 1) * HG_KDIM]

    def step(ci, carry):
        chains, rows = [], []
        for u in range(unroll):
            cf = ci * unroll + u
            rf = pl.multiple_of(cf * c, c)
            rb = pl.multiple_of((nch - 1 - cf) * c, c)
            for h in range(nh):
                hq, hi = head(hq_ref, h), head(hi_ref, h)
                slot = (u * nh + h) * 2
                chains.append((0, 2 * h, rf, hq, head(ff_ref, h), hi, lb_ref[h, 0:1, :], cum_scr.at[slot]))
                chains.append((1, 2 * h + 1, rb, hq, head(fb_ref, h), hi, lb_ref[h, 1:2, :], cum_scr.at[slot + 1]))
                rows += [(head(of_scr, h), rf), (head(ob_scr, h), rb)]
        outs, states = _hgrn_chunks(chains, carry, m_ref, lev_ref)
        for (scr, r0), o in zip(rows, outs):
            scr[pl.ds(r0, c), :] = o
        return tuple(states)

    if has_s0:
        init = tuple(s0_ref[d, h] for h in range(nh) for d in range(2))
    else:
        init = tuple(jnp.zeros((HG_KDIM, HG_VDIM), F32) for _ in range(2 * nh))
    if nch == unroll:
        fin = step(0, init)
    else:
        fin = lax.fori_loop(0, nch // unroll, step, init)
    if not has_s0:
        for h in range(nh):
            for d in range(2):
                sfin_ref[d, h] = fin[2 * h + d]

    for h in range(nh):
        o = head(of_scr, h)[...] + head(ob_scr, h)[...]
        ms = jnp.mean(o * o, axis=-1, keepdims=True)
        y = o * lax.rsqrt(ms + EPS) * ng_ref[...]
        head(o_ref, h)[...] = (y * _silu(head(hgt_ref, h)[...])).astype(BF16)


def _hgrn(p, lb, norm_g, m_all, lev_of, layer, seq_len, s0=None):
    n = p.shape[0]
    nseq = n // seq_len
    has_s0 = s0 is not None
    nch = seq_len // HG_CHUNK
    unroll = min(nch, HG_CHAINS // 2)
    nh = HG_CHAINS // (2 * unroll)
    kern = functools.partial(_hgrn_kernel, seq_len=seq_len, nh=nh, unroll=unroll, has_s0=has_s0)
    base = COL_HG // (nh * LANES)
    per = HG_HEADS // nh

    def col(k):
        return pl.BlockSpec((seq_len, nh * LANES), lambda b, h, k=k: (b, base + k * per + h))

    c = HG_CHUNK
    in_specs = [col(0), col(1), col(2), col(3), col(4),
                pl.BlockSpec((None, nh, 2, HG_KDIM), lambda b, h: (layer, h, 0, 0)),
                pl.BlockSpec((None, 1, HG_VDIM), lambda b, h: (layer, 0, 0)),
                pl.BlockSpec((2, (len(HG_MXU_LEVELS) + 1) * c, 2 * c), lambda b, h: (0, 0, 0)),
                pl.BlockSpec((2, c, c), lambda b, h: (0, 0, 0))]
    args = [p, p, p, p, p, lb, norm_g, m_all, lev_of]
    out_specs = [pl.BlockSpec((seq_len, nh * HG_VDIM), lambda b, h: (b, h))]
    out_shape = [jax.ShapeDtypeStruct((n, HG_VW), BF16)]
    if has_s0:
        in_specs.append(pl.BlockSpec((None, None, 2, nh, HG_KDIM, HG_VDIM), lambda b, h: (b, layer, 0, h, 0, 0)))
        args.append(s0)
    else:
        out_specs.append(pl.BlockSpec((None, 2, nh, HG_KDIM, HG_VDIM), lambda b, h: (b, 0, h, 0, 0)))
        out_shape.append(jax.ShapeDtypeStruct((nseq, 2, HG_HEADS, HG_KDIM, HG_VDIM), F32))
    return pl.pallas_call(
        kern,
        grid=(nseq, per),
        in_specs=in_specs,
        out_specs=out_specs,
        out_shape=out_shape,
        scratch_shapes=[pltpu.VMEM((seq_len, nh * HG_VDIM), F32), pltpu.VMEM((seq_len, nh * HG_VDIM), F32),
                        pltpu.VMEM((HG_CHAINS, HG_CHUNK, HG_KDIM), F32)],
        compiler_params=_cparams(("parallel", "parallel")),
        name="hgrn",
    )(*args)


ROUTER_ROWS = 32


def _router(logits_t):
    row = lax.broadcasted_iota(jnp.int32, logits_t.shape, 0)
    big = jnp.int32(ROUTER_ROWS)
    red = dict(axis=0, keepdims=True)
    is_g = row < N_GROUPS
    gl = jnp.where(is_g, logits_t, -jnp.inf)
    gmax = jnp.max(gl, **red)
    gsum = jnp.sum(jnp.exp(gl - gmax), **red)
    g_idx = jnp.min(jnp.where(is_g & (gl == gmax), row, big), **red)
    g_prob = 1.0 / gsum
    e_row = row - ROUTER_OFF
    sel = (e_row >= 0) & (e_row < N_EXPERTS) & ((e_row >> 2) == g_idx)
    el = jnp.where(sel, logits_t, -jnp.inf)
    emax = jnp.max(el, **red)
    eexp = jnp.exp(el - emax)
    ep = eexp / jnp.sum(eexp, **red)
    p1 = jnp.max(ep, **red)
    i1 = jnp.min(jnp.where(sel & (ep == p1), row, big), **red)
    rest = sel & (row != i1)
    ep2 = jnp.where(rest, ep, -1.0)
    p2 = jnp.max(ep2, **red)
    i2 = jnp.min(jnp.where(rest & (ep2 == p2), row, big), **red)
    den = p1 + p2
    w1 = g_prob * (p1 / den)
    w2 = g_prob * (p2 / den)
    return jnp.where(row == i1, w1, jnp.where(row == i2, w2, 0.0))


def _merge_kernel(x_ref, att_ref, z_ref, hg_ref, mg_ref, mod_ref, n2_ref,
                  wao_ref, wga_ref, wgb_ref, who_ref, wout_ref, wr_ref, br_ref,
                  xm_ref, h2_ref, comb_ref):
    y_att = _dot(att_ref[...], wao_ref[...])
    z = z_ref[...].astype(BF16)
    y_ssm = _dot(z, wga_ref[...]) * _sigmoid(_dot(z, wgb_ref[...]))
    y_hg = _dot(hg_ref[...], who_ref[...])
    gate = lambda k: _sigmoid(mg_ref[:, k * D_MODEL:(k + 1) * D_MODEL].astype(F32))
    merged = gate(0) * y_att + gate(1) * y_ssm + gate(2) * y_hg
    xm = x_ref[...] + mod_ref[2:3, :] * _dot(merged.astype(BF16), wout_ref[...])
    xm_ref[...] = xm
    ms = jnp.mean(xm * xm, axis=-1, keepdims=True)
    h2 = xm * lax.rsqrt(ms + EPS) * n2_ref[...] * (1.0 + mod_ref[4:5, :]) + mod_ref[3:4, :]
    h2_hi = h2.astype(BF16)
    h2_ref[...] = h2_hi
    h2_lo = (h2 - h2_hi.astype(F32)).astype(BF16)
    logits = _dot(jnp.concatenate([h2_hi, h2_lo, h2_hi], axis=1), wr_ref[...]) + br_ref[...]
    comb_t = _router(logits.T[0:ROUTER_ROWS])
    comb_t = jnp.concatenate([comb_t, jnp.zeros((LANES - ROUTER_ROWS, comb_t.shape[1]), F32)], axis=0)
    comb_ref[...] = comb_t.T


def _merge(x, att, z, hg, p, mod, n2g, w, layer, seq_len, cond_row0):
    n = x.shape[0]
    tm = 256
    rows_per_cond = seq_len if cond_row0 else n
    mod_idx = (lambda i: (layer, cond_row0 + (i * tm) // rows_per_cond, 0, 0))
    row = lambda width: pl.BlockSpec((tm, width), lambda i: (i, 0))
    wspec = lambda a, b: pl.BlockSpec((None, a, b), lambda i: (layer, 0, 0))
    return pl.pallas_call(
        _merge_kernel,
        grid=(n // tm,),
        in_specs=[
            row(D_MODEL), row(Q_W), row(SSM_WIDTH), row(HG_VW),
            row(MG_W),
            pl.BlockSpec((None, None, 6, D_MODEL), mod_idx),
            wspec(1, D_MODEL),
            wspec(Q_W, D_MODEL), wspec(SSM_WIDTH, D_MODEL), wspec(SSM_WIDTH, D_MODEL),
            wspec(HG_VW, D_MODEL), wspec(D_MODEL, D_MODEL), wspec(3 * D_MODEL, LANES), wspec(1, LANES),
        ],
        out_specs=[row(D_MODEL), row(D_MODEL), row(LANES)],
        out_shape=[
            jax.ShapeDtypeStruct((n, D_MODEL), F32),
            jax.ShapeDtypeStruct((n, D_MODEL), BF16),
            jax.ShapeDtypeStruct((n, LANES), F32),
        ],
        compiler_params=_cparams(("parallel",)),
        name="merge",
    )(x, att, z, hg, p, mod, n2g, w['w_attn_o'], w['w_glu_a'], w['w_glu_b'], w['w_hg_o'], w['w_out'],
      w['w_route'], w['b_route'])


def _moe_kernel(h_ref, comb_ref, xm_ref, mod_ref, wg_ref, wu_ref, wd_ref, o_ref, act_scr):
    comb = comb_ref[...]
    lane = lax.broadcasted_iota(jnp.int32, comb.shape, 1)
    h = h_ref[...]
    gw = EXPERTS_PER_GROUP * EXPERT_FF
    for grp in range(N_GROUPS):
        cols = slice(grp * gw, (grp + 1) * gw)
        hg = _dot(h, wg_ref[:, cols])
        hu = _dot(h, wu_ref[:, cols])
        for e in range(EXPERTS_PER_GROUP):
            cw = jnp.sum(jnp.where(lane == ROUTER_OFF + grp * EXPERTS_PER_GROUP + e, comb, 0.0),
                         axis=-1, keepdims=True)
            sl = slice(e * EXPERT_FF, (e + 1) * EXPERT_FF)
            act_scr[:, grp * gw + e * EXPERT_FF:grp * gw + (e + 1) * EXPERT_FF] = (
                _silu(hg[:, sl]) * hu[:, sl] * cw).astype(BF16)
    o_ref[...] = xm_ref[...] + mod_ref[5:6, :] * _dot(act_scr[...], wd_ref[...])


def _moe(h2, comb, xm, mod, w, layer, seq_len, cond_row0):
    n = h2.shape[0]
    tm = 512
    ff_all = N_EXPERTS * EXPERT_FF
    rows_per_cond = seq_len if cond_row0 else n
    mod_idx = (lambda i: (layer, cond_row0 + (i * tm) // rows_per_cond, 0, 0))
    once = pl.Buffered(1)
    return pl.pallas_call(
        _moe_kernel,
        grid=(n // tm,),
        in_specs=[
            pl.BlockSpec((tm, D_MODEL), lambda i: (i, 0)),
            pl.BlockSpec((tm, LANES), lambda i: (i, 0)),
            pl.BlockSpec((tm, D_MODEL), lambda i: (i, 0)),
            pl.BlockSpec((None, None, 6, D_MODEL), mod_idx),
            pl.BlockSpec((None, D_MODEL, ff_all), lambda i: (layer, 0, 0), pipeline_mode=once),
            pl.BlockSpec((None, D_MODEL, ff_all), lambda i: (layer, 0, 0), pipeline_mode=once),
            pl.BlockSpec((None, ff_all, D_MODEL), lambda i: (layer, 0, 0), pipeline_mode=once),
        ],
        out_specs=pl.BlockSpec((tm, D_MODEL), lambda i: (i, 0)),
        out_shape=jax.ShapeDtypeStruct((n, D_MODEL), F32),
        scratch_shapes=[pltpu.VMEM((tm, ff_all), BF16)],
        compiler_params=_cparams(("arbitrary",)),
        name="moe",
    )(h2, comb, xm, mod, w['w_e_gate'], w['w_e_up'], w['w_e_down'])


def _layer(x, layer, w, seq_len, cond_row0, cache):
    mg, p = _in_proj(x, w['mod'], w['norm1_g'], w['w_in_mg'], w['w_in_p'], layer, seq_len, cond_row0)
    nseq = x.shape[0] // seq_len
    if cache is None:
        att, k_new, v_new = _attn_ctx(w['attn_sink'], p, w['q_norm_g'], w['k_norm_g'], layer)
        z, ssm_fin = _ssm(p, w['ssm'], layer, nseq, seq_len)
        hg, hg_fin = _hgrn(p, w['hg_lb'], w['hg_norm_g'], w['hg_m'], w['hg_lev'], layer, seq_len)
        ctx = (k_new, v_new, ssm_fin, hg_fin)
    else:
        cache_k, cache_v, h0, s0 = cache
        att = _attn_lat(w['attn_sink'], p, cache_k, cache_v, w['q_norm_g'], w['k_norm_g'],
                        w['rope_cos'], w['rope_sin'], layer)
        (z,) = _ssm(p, w['ssm'], layer, nseq, seq_len, h0=h0)
        (hg,) = _hgrn(p, w['hg_lb'], w['hg_norm_g'], w['hg_m'], w['hg_lev'], layer, seq_len, s0=s0)
        ctx = None
    xm, h2, comb = _merge(x, att, z, hg, mg, w['mod'], w['norm2_g'], w, layer, seq_len, cond_row0)
    x = _moe(h2, comb, xm, w['mod'], w, layer, seq_len, cond_row0)
    return x, ctx


def kernel(x_prompt, x_sample, cache_k, cache_v, state_ssm_re, state_ssm_im, state_hgrn, c, c_ctx, w_mod, b_mod, norm1_g, norm2_g, w_in, q_norm_g, k_norm_g, attn_sink, w_attn_o, ssm_a_re, ssm_a_im, ssm_log_dt, ssm_b_re, ssm_b_im, ssm_c_re, ssm_c_im, ssm_d, w_glu_a, w_glu_b, hg_lb, hg_norm_g, w_hg_o, w_out, w_group, b_group, w_router, b_router, w_e_gate, w_e_up, w_e_down):
    w = {}
    w['mod'] = _modulation(c, c_ctx, w_mod, b_mod)
    w['norm1_g'] = norm1_g.reshape(DEPTH, 1, D_MODEL)
    w['norm2_g'] = norm2_g.reshape(DEPTH, 1, D_MODEL)
    w['w_in_mg'] = w_in[:, :, 3840:].astype(BF16)
    w['w_in_p'] = jnp.concatenate([w_in[:, :, 1280:3840], w_in[:, :, 0:768], w_in[:, :, 768:1280]],
                                  axis=-1).astype(BF16)
    w['q_norm_g'] = jnp.tile(q_norm_g, (1, LANES // HEAD_DIM)).reshape(DEPTH, 1, LANES)
    w['k_norm_g'] = jnp.tile(k_norm_g, (1, LANES // HEAD_DIM)).reshape(DEPTH, 1, LANES)
    w['attn_sink'] = attn_sink
    w['rope_cos'], w['rope_sin'] = _rope_tables()
    w['ssm'] = _ssm_prep(ssm_a_re, ssm_a_im, ssm_log_dt, ssm_b_re, ssm_b_im, ssm_c_re, ssm_c_im, ssm_d)
    w['hg_lb'] = _lower_bounds(hg_lb)
    w['hg_norm_g'] = hg_norm_g.reshape(DEPTH, 1, HG_VDIM)
    w['hg_m'], w['hg_lev'] = _hgrn_consts()
    for name, val in (('w_attn_o', w_attn_o), ('w_glu_a', w_glu_a), ('w_glu_b', w_glu_b), ('w_hg_o', w_hg_o),
                      ('w_out', w_out)):
        w[name] = val.astype(BF16)
    ff_all = N_EXPERTS * EXPERT_FF
    w['w_e_gate'] = w_e_gate.astype(BF16).transpose(0, 2, 1, 3).reshape(DEPTH, D_MODEL, ff_all)
    w['w_e_up'] = w_e_up.astype(BF16).transpose(0, 2, 1, 3).reshape(DEPTH, D_MODEL, ff_all)
    w['w_e_down'] = w_e_down.astype(BF16).reshape(DEPTH, ff_all, D_MODEL)
    pad = LANES - N_GROUPS - N_EXPERTS
    w_route = jnp.concatenate([w_group, w_router, jnp.zeros((DEPTH, D_MODEL, pad), F32)], axis=-1)
    w_route_hi = w_route.astype(BF16)
    w_route_lo = (w_route - w_route_hi.astype(F32)).astype(BF16)
    w['w_route'] = jnp.concatenate([w_route_hi, w_route_hi, w_route_lo], axis=1)
    w['b_route'] = jnp.concatenate([b_group, b_router, jnp.zeros((DEPTH, pad), F32)], axis=-1).reshape(DEPTH, 1, LANES)

    ck = cache_k.reshape(DEC_BATCH, DEPTH, PAST_LEN, KV_W)
    cv = cache_v.reshape(DEC_BATCH, DEPTH, PAST_LEN, KV_W)

    ng = LANES // SSM_GROUP
    h0 = jnp.stack([state_ssm_re, state_ssm_im]).transpose(2, 4, 0, 1, 3, 5).reshape(
        DEPTH, SSM_GROUPS // ng, ng, 2, DEC_BATCH, 2 * SSM_STATE)
    h0 = jnp.pad(h0, ((0, 0),) * 4 + ((0, SUBLANES - DEC_BATCH), (0, 0)))

    xp = x_prompt.reshape(BATCH * SEQ, D_MODEL)
    xs = x_sample.reshape(DEC_BATCH * DEC_SEQ, D_MODEL)
    new_k, new_v, new_re, new_im, new_hg = [], [], [], [], []
    for l in range(DEPTH):
        xp, ctx = _layer(xp, l, w, SEQ, 0, None)
        new_k.append(ctx[0].reshape(BATCH, SEQ, N_KV_HEADS, HEAD_DIM))
        new_v.append(ctx[1].reshape(BATCH, SEQ, N_KV_HEADS, HEAD_DIM))
        fin = ctx[2].reshape(BATCH // SUBLANES, SSM_GROUPS // ng, ng, 2, SUBLANES, 2, SSM_STATE)
        fin = fin.transpose(3, 0, 4, 5, 1, 2, 6).reshape(2, BATCH, 2, SSM_GROUPS, SSM_STATE)
        new_re.append(fin[0])
        new_im.append(fin[1])
        new_hg.append(ctx[3])
        xs, _ = _layer(xs, l, w, DEC_SEQ, 1, (ck, cv, h0, state_hgrn))
    return (xp.reshape(BATCH, SEQ, D_MODEL), xs.reshape(DEC_BATCH, DEC_SEQ, D_MODEL),
            jnp.stack(new_k, axis=1), jnp.stack(new_v, axis=1), jnp.stack(new_re, axis=1),
            jnp.stack(new_im, axis=1), jnp.stack(new_hg, axis=1))
```
